```python
import jax, jax.numpy as jnp
from jax import lax
import numpy as np

D_MODEL = 1024
BATCH = 2
SEQ = 8192
DEPTH = 2
DEC_BATCH = 128
DEC_SEQ = 8
PAST_LEN = 2048
PAGE_SIZE = 128

HEAD_DIM = 64
ROT_DIM = HEAD_DIM // 4
ROPE_THETA = 500000.0
N_A_LAYERS = DEPTH // 2
N_B_LAYERS = DEPTH - N_A_LAYERS
A_HEADS = D_MODEL // HEAD_DIM
MOBA_BLOCK = 256
MOBA_TOPK = 3
B_CONFIGS = ((128, 1), (512, 4), (2048, 16))
B_GROUPS = len(B_CONFIGS)
B_HEADS = D_MODEL // (2 * HEAD_DIM)
B_WINDOW = max(w for w, _ in B_CONFIGS)
D_FF = 2816
EPS = 1e-6
NEG = -1e30
PROMPT_Q_CHUNK_A = 16
PROMPT_Q_CHUNK_B = 64

kernel_name = 'yoco_moba_dilated_macaron_adaln_step'


def rmsnorm(x, g):
    xf = x.astype(jnp.float32)
    y = xf * lax.rsqrt(jnp.mean(xf * xf, axis=-1, keepdims=True) + EPS)
    return (y * g.astype(jnp.float32)).astype(x.dtype)


def modulate(x, shift, scale):
    return x * (1 + scale[:, None, :]) + shift[:, None, :]


def rope(x, pos):
    half = ROT_DIM // 2
    t = x.shape[1]
    freq = ROPE_THETA ** (-jnp.arange(half, dtype=jnp.float32) / half)
    ang = pos.astype(jnp.float32)[:, None] * freq[None, :]
    shp = (1, t) + (1,) * (x.ndim - 3) + (half,)
    cos = jnp.cos(ang).reshape(shp)
    sin = jnp.sin(ang).reshape(shp)
    xf = x.astype(jnp.float32)
    x1 = xf[..., :half]
    x2 = xf[..., half:ROT_DIM]
    out = jnp.concatenate([x1 * cos - x2 * sin, x2 * cos + x1 * sin, xf[..., ROT_DIM:]], axis=-1)
    return out.astype(x.dtype)


def swiglu(x, wg, wu, wd):
    return (jax.nn.silu(x @ wg) * (x @ wu)) @ wd


def moba_attention(q, k, v, q_pos, q_chunk):
    bsz, t, nh, dh = q.shape
    length = k.shape[1]
    nb = -(-length // MOBA_BLOCK)
    pad = nb * MOBA_BLOCK - length
    kp = jnp.pad(k, ((0, 0), (0, pad), (0, 0), (0, 0)))
    vp = jnp.pad(v, ((0, 0), (0, pad), (0, 0), (0, 0)))
    kb = kp.reshape(bsz, nb, MOBA_BLOCK, nh, dh).transpose(0, 3, 1, 2, 4)
    vb = vp.reshape(bsz, nb, MOBA_BLOCK, nh, dh).transpose(0, 3, 1, 2, 4)
    kmean = jnp.mean(kb.astype(jnp.float32), axis=3)
    n_sel = min(MOBA_TOPK, nb)
    scale = dh ** -0.5
    bi = jnp.arange(bsz)[:, None, None, None]
    hi = jnp.arange(nh)[None, None, :, None]
    blk_ids = jnp.arange(nb, dtype=jnp.int32)
    in_blk = jnp.arange(MOBA_BLOCK, dtype=jnp.int32)

    def one_chunk(args):
        qc, pc = args
        c = pc.shape[0]
        own = pc // MOBA_BLOCK
        gate = jnp.einsum('bchd,bhnd->bchn', qc.astype(jnp.float32), kmean)
        past = blk_ids[None, :] < own[:, None]
        gate = jnp.where(past[None, :, None, :], gate, NEG)
        _, top = lax.top_k(gate, n_sel)
        top_ok = jnp.arange(n_sel)[None, :] < own[:, None]
        sel = jnp.concatenate([top.astype(jnp.int32), jnp.broadcast_to(own[None, :, None, None], (bsz, c, nh, 1))], axis=-1)
        ok = jnp.concatenate([top_ok, jnp.ones((c, 1), bool)], axis=-1)
        kg = kb[bi, hi, sel]
        vg = vb[bi, hi, sel]
        kpos = sel[..., None] * MOBA_BLOCK + in_blk
        mask = ok[None, :, None, :, None] & (kpos <= pc[None, :, None, None, None])
        s = jnp.einsum('bchd,bchjkd->bchjk', qc, kg).astype(jnp.float32) * scale
        s = jnp.where(mask, s, NEG)
        p = jax.nn.softmax(s.reshape(bsz, c, nh, -1), axis=-1).reshape(s.shape)
        return jnp.einsum('bchjk,bchjkd->bchd', p.astype(vg.dtype), vg)

    nch = t // q_chunk
    qs = q.reshape(bsz, nch, q_chunk, nh, dh).transpose(1, 0, 2, 3, 4)
    ps = q_pos.reshape(nch, q_chunk)
    out = lax.map(one_chunk, (qs, ps))
    return out.transpose(1, 0, 2, 3, 4).reshape(bsz, t, nh, dh)


def _dilation_table():
    n = max(w // d for w, d in B_CONFIGS) + 1
    dist = np.array([[i * d for i in range(n)] for w, d in B_CONFIGS], np.int32)
    ok = np.array([[i * d <= w for i in range(n)] for w, d in B_CONFIGS], bool)
    return dist, ok


def dilated_attention(q, k, v, q_idx, q_chunk):
    bsz, t, ng, nj, dh = q.shape
    dist_np, ok_np = _dilation_table()
    dist = jnp.asarray(dist_np)
    okd = jnp.asarray(ok_np)
    scale = dh ** -0.5

    def one_chunk(args):
        qc, ic = args
        idx = ic[:, None, None] - dist[None]
        valid = okd[None] & (idx >= 0)
        idx = jnp.maximum(idx, 0)
        kg = k[:, idx]
        vg = v[:, idx]
        s = jnp.einsum('bcgjd,bcgnjd->bcgjn', qc, kg).astype(jnp.float32) * scale
        s = jnp.where(valid[None, :, :, None, :], s, NEG)
        lse = jax.nn.logsumexp(s, axis=-1)
        p = jnp.exp(s - lse[..., None])
        o = jnp.einsum('bcgjn,bcgnjd->bcgjd', p.astype(vg.dtype), vg)
        w = jax.nn.softmax(lse, axis=2)
        return jnp.einsum('bcgj,bcgjd->bcjd', w.astype(o.dtype), o)

    nch = t // q_chunk
    qs = q.reshape(bsz, nch, q_chunk, ng, nj, dh).transpose(1, 0, 2, 3, 4, 5)
    ix = q_idx.reshape(nch, q_chunk)
    out = lax.map(one_chunk, (qs, ix))
    return out.transpose(1, 0, 2, 3, 4).reshape(bsz, t, nj, dh)


def run_trunk(x, c, pos0, q_chunk_a, q_chunk_b, a_cache_k, a_cache_v, page_table, b_past_k, b_past_v,
              norm_g, w_mod, b_mod, w_ffn_gate, w_ffn_up, w_ffn_down, w_qkv_a, w_o_a,
              kv_norm_g, w_kv_mod, b_kv_mod, w_kv_b, w_q_b, w_o_b, final_norm_g):
    bsz, t, d = x.shape
    pos = pos0 + jnp.arange(t, dtype=jnp.int32)
    cs = jax.nn.silu(c)
    h = x
    a_k_rows, a_v_rows = [], []
    b_k_rows = b_v_rows = bk = bv = None
    b_base = 0
    for l in range(DEPTH):
        if l == N_A_LAYERS:
            kvm = (cs @ w_kv_mod + b_kv_mod).reshape(bsz, 2, d)
            s_in = modulate(rmsnorm(h, kv_norm_g), kvm[:, 0], kvm[:, 1])
            kv = (s_in @ w_kv_b).reshape(bsz, t, 2, B_HEADS, HEAD_DIM)
            b_k_rows = rope(kv[:, :, 0], pos)
            b_v_rows = kv[:, :, 1]
            if b_past_k is None:
                bk, bv, b_base = b_k_rows, b_v_rows, 0
            else:
                bk = jnp.concatenate([b_past_k, b_k_rows], axis=1)
                bv = jnp.concatenate([b_past_v, b_v_rows], axis=1)
                b_base = b_past_k.shape[1]
        mod = (cs @ w_mod[l] + b_mod[l]).reshape(bsz, 3, 3, d)
        u = modulate(rmsnorm(h, norm_g[l, 0]), mod[:, 0, 0], mod[:, 0, 1])
        h = h + 0.5 * mod[:, 0, 2][:, None] * swiglu(u, w_ffn_gate[l, 0], w_ffn_up[l, 0], w_ffn_down[l, 0])
        u = modulate(rmsnorm(h, norm_g[l, 1]), mod[:, 1, 0], mod[:, 1, 1])
        if l < N_A_LAYERS:
            qkv = (u @ w_qkv_a[l]).reshape(bsz, t, 3, A_HEADS, HEAD_DIM)
            q = rope(qkv[:, :, 0], pos)
            k = rope(qkv[:, :, 1], pos)
            v = qkv[:, :, 2]
            a_k_rows.append(k)
            a_v_rows.append(v)
            if a_cache_k is None:
                kf, vf = k, v
            else:
                kf = jnp.concatenate([a_cache_k[l][page_table].reshape(bsz, -1, A_HEADS, HEAD_DIM), k], axis=1)
                vf = jnp.concatenate([a_cache_v[l][page_table].reshape(bsz, -1, A_HEADS, HEAD_DIM), v], axis=1)
            o = moba_attention(q, kf, vf, pos, q_chunk_a)
            mix = o.reshape(bsz, t, -1) @ w_o_a[l]
        else:
            lb = l - N_A_LAYERS
            q = rope((u @ w_q_b[lb]).reshape(bsz, t, B_GROUPS * B_HEADS, HEAD_DIM), pos)
            q = q.reshape(bsz, t, B_GROUPS, B_HEADS, HEAD_DIM)
            o = dilated_attention(q, bk, bv, b_base + jnp.arange(t, dtype=jnp.int32), q_chunk_b)
            mix = o.reshape(bsz, t, -1) @ w_o_b[lb]
        h = h + mod[:, 1, 2][:, None] * mix
        u = modulate(rmsnorm(h, norm_g[l, 2]), mod[:, 2, 0], mod[:, 2, 1])
        h = h + 0.5 * mod[:, 2, 2][:, None] * swiglu(u, w_ffn_gate[l, 1], w_ffn_up[l, 1], w_ffn_down[l, 1])
    y = rmsnorm(h, final_norm_g)
    return y, jnp.stack(a_k_rows), jnp.stack(a_v_rows), b_k_rows, b_v_rows


def setup_inputs(seed: int = 0) -> dict:
    key = jax.random.key(seed)
    ks = jax.random.split(key, 32)
    f32 = jnp.float32
    n_pages = PAST_LEN // PAGE_SIZE
    n_pool = (DEC_BATCH * n_pages * 5) // 4
    wb = min(B_WINDOW, PAST_LEN)
    d = D_MODEL
    nrm = lambda k, shp, s: jax.random.normal(k, shp, f32) * s
    perm = jax.random.permutation(ks[6], n_pool)[: DEC_BATCH * n_pages]
    return {
        'x_prompt': nrm(ks[0], (BATCH, SEQ, d), 1.0),
        'x_sample': nrm(ks[1], (DEC_BATCH, DEC_SEQ, d), 1.0),
        'cache_a_k': nrm(ks[2], (N_A_LAYERS, n_pool, PAGE_SIZE, A_HEADS, HEAD_DIM), 1.0),
        'cache_a_v': nrm(ks[3], (N_A_LAYERS, n_pool, PAGE_SIZE, A_HEADS, HEAD_DIM), 1.0),
        'cache_b_k': nrm(ks[4], (DEC_BATCH, wb, B_HEADS, HEAD_DIM), 1.0),
        'cache_b_v': nrm(ks[5], (DEC_BATCH, wb, B_HEADS, HEAD_DIM), 1.0),
        'page_table': perm.reshape(DEC_BATCH, n_pages).astype(jnp.int32),
        'c_prompt': nrm(ks[7], (BATCH, d), 1.0),
        'c_sample': nrm(ks[8], (DEC_BATCH, d), 1.0),
        'norm_g': 1.0 + nrm(ks[9], (DEPTH, 3, d), 0.05),
        'w_mod': nrm(ks[10], (DEPTH, d, 9 * d), 0.5 * d ** -0.5),
        'b_mod': nrm(ks[11], (DEPTH, 9 * d), 0.02),
        'w_ffn_gate': nrm(ks[12], (DEPTH, 2, d, D_FF), d ** -0.5),
        'w_ffn_up': nrm(ks[13], (DEPTH, 2, d, D_FF), d ** -0.5),
        'w_ffn_down': nrm(ks[14], (DEPTH, 2, D_FF, d), D_FF ** -0.5),
        'w_qkv_a': nrm(ks[15], (N_A_LAYERS, d, 3 * A_HEADS * HEAD_DIM), d ** -0.5),
        'w_o_a': nrm(ks[16], (N_A_LAYERS, A_HEADS * HEAD_DIM, d), (A_HEADS * HEAD_DIM) ** -0.5),
        'kv_norm_g': 1.0 + nrm(ks[17], (d,), 0.05),
        'w_kv_mod': nrm(ks[18], (d, 2 * d), 0.5 * d ** -0.5),
        'b_kv_mod': nrm(ks[19], (2 * d,), 0.02),
        'w_kv_b': nrm(ks[20], (d, 2 * B_HEADS * HEAD_DIM), d ** -0.5),
        'w_q_b': nrm(ks[21], (N_B_LAYERS, d, B_GROUPS * B_HEADS * HEAD_DIM), d ** -0.5),
        'w_o_b': nrm(ks[22], (N_B_LAYERS, B_HEADS * HEAD_DIM, d), (B_HEADS * HEAD_DIM) ** -0.5),
        'final_norm_g': 1.0 + nrm(ks[23], (d,), 0.05),
    }


def reference(x_prompt, x_sample, cache_a_k, cache_a_v, cache_b_k, cache_b_v, page_table, c_prompt, c_sample,
              norm_g, w_mod, b_mod, w_ffn_gate, w_ffn_up, w_ffn_down, w_qkv_a, w_o_a,
              kv_norm_g, w_kv_mod, b_kv_mod, w_kv_b, w_q_b, w_o_b, final_norm_g):
    weights = (norm_g, w_mod, b_mod, w_ffn_gate, w_ffn_up, w_ffn_down, w_qkv_a, w_o_a,
               kv_norm_g, w_kv_mod, b_kv_mod, w_kv_b, w_q_b, w_o_b, final_norm_g)
    y_prompt, ak_p, av_p, bk_p, bv_p = run_trunk(
        x_prompt, c_prompt, 0, PROMPT_Q_CHUNK_A, PROMPT_Q_CHUNK_B,
        None, None, None, None, None, *weights)
    keep = min(B_WINDOW, x_prompt.shape[1])
    bk_p_keep = bk_p[:, -keep:]
    bv_p_keep = bv_p[:, -keep:]
    past_len = page_table.shape[1] * cache_a_k.shape[2]
    y_sample, ak_s, av_s, bk_s, bv_s = run_trunk(
        x_sample, c_sample, past_len, 1, 1,
        cache_a_k, cache_a_v, page_table, cache_b_k, cache_b_v, *weights)
    return (y_prompt, y_sample, ak_p, av_p, ak_s, av_s, bk_p_keep, bv_p_keep, bk_s, bv_s)
```

```python
import functools

import jax
import jax.numpy as jnp
from jax import lax
from jax.experimental import pallas as pl
from jax.experimental.pallas import tpu as pltpu

HEAD_DIM = 64
ROT_DIM = HEAD_DIM // 4
ROPE_THETA = 500000.0
MOBA_BLOCK = 256
MOBA_TOPK = 3
B_CONFIGS = ((128, 1), (512, 4), (2048, 16))
EPS = 1e-6
NEG = -1e30

LANES = 128
HEADS_PER_TILE = LANES // HEAD_DIM
VMEM_LIMIT = 56 * 1024 * 1024

F32 = jnp.float32
BF16 = jnp.bfloat16


def _params(*sem):
    return pltpu.CompilerParams(dimension_semantics=sem, vmem_limit_bytes=VMEM_LIMIT)


def _dot(a, b):
    return jnp.dot(a, b, preferred_element_type=F32)


def _dot_nt(a, b):
    return lax.dot_general(a, b, (((1,), (1,)), ((), ())), preferred_element_type=F32)


def _silu(x):
    return x * jax.nn.sigmoid(x)


def _rmsnorm(x, g):
    return x * lax.rsqrt(jnp.mean(x * x, axis=-1, keepdims=True) + EPS) * g


def _rms_mod(x, g, shift, scale):
    return _rmsnorm(x, g) * (1.0 + scale) + shift


def _head_lane_mask(shape, rows_per_head):
    row = lax.broadcasted_iota(jnp.int32, shape, 0)
    lane = lax.broadcasted_iota(jnp.int32, shape, 1)
    return (lane // HEAD_DIM) == (row // rows_per_head)


class _Rows:
    def __init__(self, m, tm, rows_per_seq):
        self.m, self.tm = m, tm
        self.per_row = rows_per_seq < tm
        self.rows_per_seq = rows_per_seq
        self.tiles_per_seq = max(rows_per_seq // tm, 1)

    def prep(self, vec):
        if self.per_row:
            return jnp.repeat(vec, self.rows_per_seq, axis=0)
        return vec[:, None, :]

    def mod_spec(self, d):
        if self.per_row:
            return pl.BlockSpec((self.tm, d), lambda i, j: (i, 0))
        tps = self.tiles_per_seq
        return pl.BlockSpec((None, 1, d), lambda i, j: (i // tps, 0, 0))

    def pos_spec(self, n_pos_tiles):
        return pl.BlockSpec((self.tm, LANES), lambda i, j: (i % n_pos_tiles, 0))


def _mod_kernel(c_ref, w_ref, b_ref, o_ref):
    cs = _silu(c_ref[...]).astype(BF16)
    o_ref[...] = _dot(cs, w_ref[...].astype(BF16)) + b_ref[...]


def _modulation(c, w, b):
    mc, d = c.shape
    nl, _, n = w.shape
    tn = d
    return pl.pallas_call(
        _mod_kernel,
        grid=(nl, n // tn),
        in_specs=[pl.BlockSpec((mc, d), lambda l, j: (0, 0)),
                  pl.BlockSpec((None, d, tn), lambda l, j: (l, 0, j)),
                  pl.BlockSpec((None, 1, tn), lambda l, j: (l, 0, j))],
        out_specs=pl.BlockSpec((None, mc, tn), lambda l, j: (l, 0, j)),
        out_shape=jax.ShapeDtypeStruct((nl, mc, n), F32),
        compiler_params=_params("arbitrary", "arbitrary"),
        name="modulation",
    )(c, w, b.reshape(nl, 1, n))


def _ffn_kernel(h_ref, sh_ref, sc_ref, gt_ref, g_ref, wg_ref, wu_ref, wd_ref, *rest, n_f, final):
    if final:
        fg_ref, o_ref, u_scr, acc_scr = rest
    else:
        o_ref, u_scr, acc_scr = rest
    f = pl.program_id(1)

    @pl.when(f == 0)
    def _():
        u_scr[...] = _rms_mod(h_ref[...], g_ref[...], sh_ref[...], sc_ref[...]).astype(BF16)
        acc_scr[...] = jnp.zeros_like(acc_scr)

    u = u_scr[...]
    a = _dot(u, wg_ref[...].astype(BF16))
    b = _dot(u, wu_ref[...].astype(BF16))
    act = (_silu(a) * b).astype(BF16)
    acc_scr[...] += _dot(act, wd_ref[...].astype(BF16))

    @pl.when(f == n_f - 1)
    def _():
        hn = h_ref[...] + 0.5 * gt_ref[...] * acc_scr[...]
        if final:
            hn = _rmsnorm(hn, fg_ref[...])
        o_ref[...] = hn


def _ffn(rows, h, shift, scale, gate, g, wg, wu, wd, l, s, final_g=None):
    m, d = h.shape
    ff = wg.shape[-1]
    tf = 256 if ff % 256 == 0 else LANES
    n_f = ff // tf
    tm = rows.tm
    final = final_g is not None
    in_specs = [pl.BlockSpec((tm, d), lambda i, f: (i, 0)),
                rows.mod_spec(d), rows.mod_spec(d), rows.mod_spec(d),
                pl.BlockSpec((1, d), lambda i, f: (0, 0)),
                pl.BlockSpec((None, None, d, tf), lambda i, f: (l, s, 0, f)),
                pl.BlockSpec((None, None, d, tf), lambda i, f: (l, s, 0, f)),
                pl.BlockSpec((None, None, tf, d), lambda i, f: (l, s, f, 0))]
    args = [h, shift, scale, gate, g.reshape(1, d), wg, wu, wd]
    if final:
        in_specs.append(pl.BlockSpec((1, d), lambda i, f: (0, 0)))
        args.append(final_g.reshape(1, d))
    return pl.pallas_call(
        functools.partial(_ffn_kernel, n_f=n_f, final=final),
        grid=(m // tm, n_f),
        in_specs=in_specs,
        out_specs=pl.BlockSpec((tm, d), lambda i, f: (i, 0)),
        out_shape=jax.ShapeDtypeStruct((m, d), F32),
        scratch_shapes=[pltpu.VMEM((tm, d), BF16), pltpu.VMEM((tm, d), F32)],
        compiler_params=_params("arbitrary", "arbitrary"),
        name="ffn",
    )(*args)


def _rope_tables(pos):
    half = ROT_DIM // 2
    n = pos.shape[0]
    freq = ROPE_THETA ** (-jnp.arange(half, dtype=F32) / half)
    ang = pos.astype(F32)[:, None] * freq[None, :]
    cos, sin = jnp.cos(ang), jnp.sin(ang)
    rest = HEAD_DIM - ROT_DIM
    ct = jnp.concatenate([cos, cos, jnp.ones((n, rest), F32)], axis=1)
    s_lo = jnp.concatenate([-sin, jnp.zeros((n, half + rest), F32)], axis=1)
    s_hi = jnp.concatenate([jnp.zeros((n, half), F32), sin, jnp.zeros((n, rest), F32)], axis=1)
    tile = lambda t: jnp.tile(t, (1, HEADS_PER_TILE))
    return tile(ct), tile(s_hi), tile(s_lo)


def _rope_tile(x, ct, s_hi, s_lo):
    half = ROT_DIM // 2
    return x * ct + pltpu.roll(x, half, 1) * s_hi + pltpu.roll(x, LANES - half, 1) * s_lo


def _proj_kernel(h_ref, sh_ref, sc_ref, g_ref, w_ref, ct_ref, shi_ref, slo_ref, *rest, rope_flags):
    n_parts = len(rope_flags)
    outs, u_scr = rest[:n_parts], rest[n_parts]
    j = pl.program_id(1)

    @pl.when(j == 0)
    def _():
        u_scr[...] = _rms_mod(h_ref[...], g_ref[...], sh_ref[...], sc_ref[...]).astype(BF16)

    y = _dot(u_scr[...], w_ref[...].astype(BF16))
    for p in range(n_parts):
        @pl.when(j == p)
        def _(p=p):
            if rope_flags[p]:
                ct, s_hi, s_lo = ct_ref[...], shi_ref[...], slo_ref[...]
                for c in range(y.shape[1] // LANES):
                    sl = slice(c * LANES, (c + 1) * LANES)
                    outs[p][:, sl] = _rope_tile(y[:, sl], ct, s_hi, s_lo)
            else:
                outs[p][...] = y


def _proj(rows, h, shift, scale, g, w, tables, n_pos_tiles, rope_flags):
    m, d = h.shape
    n_parts = len(rope_flags)
    n_p = w.shape[1] // n_parts
    tm = rows.tm
    return pl.pallas_call(
        functools.partial(_proj_kernel, rope_flags=rope_flags),
        grid=(m // tm, n_parts),
        in_specs=[pl.BlockSpec((tm, d), lambda i, j: (i, 0)),
                  rows.mod_spec(d), rows.mod_spec(d),
                  pl.BlockSpec((1, d), lambda i, j: (0, 0)),
                  pl.BlockSpec((d, n_p), lambda i, j: (0, j)),
                  rows.pos_spec(n_pos_tiles), rows.pos_spec(n_pos_tiles), rows.pos_spec(n_pos_tiles)],
        out_specs=[pl.BlockSpec((tm, n_p), lambda i, j: (i, 0))] * n_parts,
        out_shape=[jax.ShapeDtypeStruct((m, n_p), F32)] * n_parts,
        scratch_shapes=[pltpu.VMEM((tm, d), BF16)],
        compiler_params=_params("arbitrary", "arbitrary"),
        name="proj",
    )(h, shift, scale, g.reshape(1, d), w, *tables)


def _oproj_kernel(h_ref, gt_ref, w_ref, *rest, n_groups):
    o_ref = rest[-1]
    if n_groups == 0:
        o = rest[0][...]
    else:
        os_, lses = rest[:n_groups], rest[n_groups:2 * n_groups]
        lse = [r[...] for r in lses]
        mx = functools.reduce(jnp.maximum, lse)
        e = [jnp.exp(x - mx) for x in lse]
        den = functools.reduce(lambda a, b: a + b, e)
        o = functools.reduce(lambda a, b: a + b, [(ei / den) * r[...] for ei, r in zip(e, os_)])
    o_ref[...] = h_ref[...] + gt_ref[...] * _dot(o.astype(BF16), w_ref[...].astype(BF16))


def _oproj(rows, h, gate, w, o_list, lse_list=()):
    m, d = h.shape
    k = w.shape[0]
    tm = rows.tm
    xs = list(o_list) + list(lse_list)
    return pl.pallas_call(
        functools.partial(_oproj_kernel, n_groups=len(lse_list)),
        grid=(m // tm, 1),
        in_specs=[pl.BlockSpec((tm, d), lambda i, j: (i, 0)),
                  rows.mod_spec(d),
                  pl.BlockSpec((k, d), lambda i, j: (0, 0))]
                 + [pl.BlockSpec((tm, k), lambda i, j: (i, 0))] * len(xs),
        out_specs=pl.BlockSpec((tm, d), lambda i, j: (i, 0)),
        out_shape=jax.ShapeDtypeStruct((m, d), F32),
        compiler_params=_params("arbitrary", "arbitrary"),
        name="oproj",
    )(h, gate, w, *xs)


def _kmean_kernel(k_ref, o_ref, *, per_step):
    rows = [jnp.mean(k_ref[pl.ds(j * MOBA_BLOCK, MOBA_BLOCK), :], axis=0, keepdims=True)
            for j in range(per_step)]
    o_ref[...] = jnp.concatenate(rows, axis=0)


def _kmean(k):
    b, t, d = k.shape
    nb = t // MOBA_BLOCK
    per_step = 8
    return pl.pallas_call(
        functools.partial(_kmean_kernel, per_step=per_step),
        grid=(b, nb // per_step),
        in_specs=[pl.BlockSpec((None, per_step * MOBA_BLOCK, d), lambda i, j: (i, j, 0))],
        out_specs=pl.BlockSpec((None, per_step, d), lambda i, j: (i, j, 0)),
        out_shape=jax.ShapeDtypeStruct((b, nb, d), F32),
        compiler_params=_params("arbitrary", "arbitrary"),
        name="kmean",
    )(k)


def _split_bf16(x):
    hi = x.astype(BF16)
    lo = (x - hi.astype(F32)).astype(BF16)
    return hi, lo


def _top_blocks(gate, n_past):
    col = lax.broadcasted_iota(jnp.int32, gate.shape, 1).astype(F32)
    n_past = n_past.astype(F32)
    g = jnp.where(col < n_past, gate, NEG)
    sel = jnp.zeros(gate.shape, jnp.bool_)
    for _ in range(MOBA_TOPK):
        mx = jnp.max(g, axis=1, keepdims=True)
        idx = jnp.min(jnp.where(g == mx, col, float(LANES)), axis=1, keepdims=True)
        pick = col == idx
        sel = sel | pick
        g = jnp.where(pick, -jnp.inf, g)
    return sel & (col < n_past)


def _moba_prompt_kernel(q_ref, k_ref, v_ref, km_ref, o_ref, qs_scr, sel_scr, m_scr, l_scr, acc_scr, *, nb):
    blk = MOBA_BLOCK
    i = pl.program_id(2)
    q = q_ref[...]
    lane = lax.broadcasted_iota(jnp.int32, (blk, LANES), 1)
    km = km_ref[...]
    if nb < LANES:
        km = jnp.concatenate([km, jnp.zeros((LANES - nb, LANES), F32)], axis=0)
    km_hi, km_lo = _split_bf16(km)
    kd = k_ref[pl.ds(pl.multiple_of(i * blk, blk), blk), :].astype(BF16)
    vd = v_ref[pl.ds(pl.multiple_of(i * blk, blk), blk), :].astype(BF16)
    row_k = lax.broadcasted_iota(jnp.int32, (blk, blk), 0)
    col_k = lax.broadcasted_iota(jnp.int32, (blk, blk), 1)
    for hh in range(HEADS_PER_TILE):
        qh = jnp.where((lane // HEAD_DIM) == hh, q, 0.0)
        q_hi, q_lo = _split_bf16(qh)
        gate = _dot_nt(q_hi, km_hi) + _dot_nt(q_hi, km_lo) + _dot_nt(q_lo, km_hi)
        sel_scr[hh] = _top_blocks(gate, i).astype(F32)
        qs = (qh * HEAD_DIM ** -0.5).astype(BF16)
        qs_scr[hh] = qs
        s = jnp.where(col_k <= row_k, _dot_nt(qs, kd), NEG)
        m = jnp.max(s, axis=1, keepdims=True)
        p = jnp.exp(s - m)
        m_scr[hh] = m
        l_scr[hh] = jnp.sum(p, axis=1, keepdims=True)
        acc_scr[hh] = _dot(p.astype(BF16), vd)

    def past_block(j, carry):
        start = pl.multiple_of(j * blk, blk)
        kb = k_ref[pl.ds(start, blk), :].astype(BF16)
        vb = v_ref[pl.ds(start, blk), :].astype(BF16)
        for hh in range(HEADS_PER_TILE):
            chosen = jnp.max(jnp.where(lane == j, sel_scr[hh], 0.0), axis=1, keepdims=True) > 0.5
            s = _dot_nt(qs_scr[hh], kb)
            m_old = m_scr[hh]
            m_all = jnp.maximum(m_old, jnp.max(s, axis=1, keepdims=True))
            p = jnp.exp(s - m_all)
            m_new = jnp.where(chosen, m_all, m_old)
            alpha = jnp.exp(m_old - m_new)
            l_scr[hh] = alpha * l_scr[hh] + jnp.where(chosen, jnp.sum(p, axis=1, keepdims=True), 0.0)
            acc_scr[hh] = alpha * acc_scr[hh] + jnp.where(chosen, _dot(p.astype(BF16), vb), 0.0)
            m_scr[hh] = m_new
        return carry

    lax.fori_loop(0, i, past_block, 0)
    outs = [acc_scr[hh] / l_scr[hh] for hh in range(HEADS_PER_TILE)]
    o_ref[...] = jnp.where(lane < HEAD_DIM, outs[0], outs[1])


def _moba_prompt(q, k, v):
    b, t, d = q.shape
    blk = MOBA_BLOCK
    nb = t // blk
    assert t % blk == 0 and nb <= LANES and HEADS_PER_TILE == 2
    km = _kmean(k)
    tile_spec = pl.BlockSpec((None, blk, LANES), lambda bi, hp, i: (bi, i, hp))
    seq_spec = pl.BlockSpec((None, t, LANES), lambda bi, hp, i: (bi, 0, hp))
    return pl.pallas_call(
        functools.partial(_moba_prompt_kernel, nb=nb),
        grid=(b, d // LANES, nb),
        in_specs=[tile_spec, seq_spec, seq_spec,
                  pl.BlockSpec((None, nb, LANES), lambda bi, hp, i: (bi, 0, hp))],
        out_specs=tile_spec,
        out_shape=jax.ShapeDtypeStruct((b, t, d), F32),
        scratch_shapes=[pltpu.VMEM((HEADS_PER_TILE, blk, LANES), BF16),
                        pltpu.VMEM((HEADS_PER_TILE, blk, LANES), F32),
                        pltpu.VMEM((HEADS_PER_TILE, blk, 1), F32),
                        pltpu.VMEM((HEADS_PER_TILE, blk, 1), F32),
                        pltpu.VMEM((HEADS_PER_TILE, blk, LANES), F32)],
        compiler_params=_params("arbitrary", "arbitrary", "arbitrary"),
        name="moba_prompt",
    )(q, k, v, km)


def _pad_rows(x, n):
    return jnp.concatenate([x, jnp.zeros((n - x.shape[0], x.shape[1]), x.dtype)], axis=0)


def _moba_sample_kernel(pt_ref, q_ref, kn_ref, vn_ref, *rest, n_pages, page, n_heads):
    k_pages, v_pages = rest[:n_pages], rest[n_pages:2 * n_pages]
    o_ref, s_scr = rest[2 * n_pages], rest[2 * n_pages + 1]
    t_dec, d = q_ref.shape
    r = n_heads * t_dec
    ppb = MOBA_BLOCK // page
    n_past = n_pages // ppb
    own = lax.broadcasted_iota(jnp.int32, (r, d), 0) // t_dec
    head_mask = (lax.broadcasted_iota(jnp.int32, (r, d), 1) // HEAD_DIM) == own
    qrows = jnp.where(head_mask, jnp.concatenate([q_ref[...]] * n_heads, axis=0), 0.0)
    qs = (qrows * HEAD_DIM ** -0.5).astype(BF16)

    gates = []
    for j in range(n_past):
        ksum = jnp.zeros((1, d), F32)
        for pp in range(ppb):
            p = j * ppb + pp
            kp = k_pages[p][...]
            ksum = ksum + jnp.sum(kp, axis=0, keepdims=True)
            s_scr[:, p * page:(p + 1) * page] = _dot_nt(qs, kp.astype(BF16))
        gates.append(jnp.sum(qrows * (ksum / MOBA_BLOCK), axis=1, keepdims=True))
    for j in range(n_past):
        rank = jnp.zeros((r, 1), jnp.int32)
        for j2 in range(n_past):
            if j2 != j:
                ahead = (gates[j2] > gates[j]) | ((gates[j2] == gates[j]) & (j2 < j))
                rank = rank + ahead.astype(jnp.int32)
        chosen = rank < MOBA_TOPK
        sl = slice(j * MOBA_BLOCK, (j + 1) * MOBA_BLOCK)
        s_scr[:, sl] = jnp.where(chosen, s_scr[:, sl], NEG)
    t_row = lax.broadcasted_iota(jnp.int32, (r, page), 0) % t_dec
    t_col = lax.broadcasted_iota(jnp.int32, (r, page), 1)
    s_own = _dot_nt(qs, _pad_rows(kn_ref[...], page).astype(BF16))
    s_scr[:, n_pages * page:] = jnp.where(t_col <= t_row, s_own, NEG)

    s = s_scr[...]
    m = jnp.max(s, axis=1, keepdims=True)
    p_all = jnp.exp(s - m)
    l = jnp.sum(p_all, axis=1, keepdims=True)
    acc = _dot(p_all[:, n_pages * page:].astype(BF16), _pad_rows(vn_ref[...], page).astype(BF16))
    for p in range(n_pages):
        acc = acc + _dot(p_all[:, p * page:(p + 1) * page].astype(BF16), v_pages[p][...].astype(BF16))
    o_full = jnp.where(head_mask, acc / l, 0.0)
    out = o_full[0:t_dec]
    for h in range(1, n_heads):
        out = out + o_full[h * t_dec:(h + 1) * t_dec]
    o_ref[...] = out


def _moba_sample(q, k_new, v_new, cache_k, cache_v, page_table):
    b, t_dec, d = q.shape
    n_pages = page_table.shape[1]
    page = cache_k.shape[1]
    n_heads = d // HEAD_DIM
    assert MOBA_BLOCK % page == 0 and (n_pages * page) % MOBA_BLOCK == 0 and t_dec <= page
    tok_spec = pl.BlockSpec((None, t_dec, d), lambda i, pt: (i, 0, 0))
    page_specs = [pl.BlockSpec((None, page, d), functools.partial(lambda i, pt, p: (pt[i, p], 0, 0), p=p))
                  for p in range(n_pages)]
    grid_spec = pltpu.PrefetchScalarGridSpec(
        num_scalar_prefetch=1,
        grid=(b,),
        in_specs=[tok_spec, tok_spec, tok_spec] + page_specs + page_specs,
        out_specs=tok_spec,
        scratch_shapes=[pltpu.VMEM((n_heads * t_dec, (n_pages + 1) * page), F32)],
    )
    return pl.pallas_call(
        functools.partial(_moba_sample_kernel, n_pages=n_pages, page=page, n_heads=n_heads),
        grid_spec=grid_spec,
        out_shape=jax.ShapeDtypeStruct((b, t_dec, d), F32),
        compiler_params=_params("arbitrary"),
        name="moba_sample",
    )(page_table, q, k_new, v_new, *([cache_k] * n_pages), *([cache_v] * n_pages))


def _dil_prompt_kernel(q_ref, k_ref, v_ref, o_ref, lse_ref, *, tq, width, win):
    length = k_ref.shape[0]
    q0 = pl.program_id(3) * tq
    ks = pl.multiple_of(jnp.clip(q0 - win, 0, length - width), LANES)
    kw = k_ref[pl.ds(ks, width), :].astype(BF16)
    vw = v_ref[pl.ds(ks, width), :].astype(BF16)
    q = q_ref[...]
    lane = lax.broadcasted_iota(jnp.int32, (tq, LANES), 1)
    delta = (q0 + lax.broadcasted_iota(jnp.int32, (tq, width), 0)
             - ks - lax.broadcasted_iota(jnp.int32, (tq, width), 1))
    valid = (delta >= 0) & (delta <= win)
    outs, lses = [], []
    for hh in range(HEADS_PER_TILE):
        qh = (jnp.where((lane // HEAD_DIM) == hh, q, 0.0) * HEAD_DIM ** -0.5).astype(BF16)
        s = jnp.where(valid, _dot_nt(qh, kw), NEG)
        m = jnp.max(s, axis=1, keepdims=True)
        p = jnp.exp(s - m)
        l = jnp.sum(p, axis=1, keepdims=True)
        outs.append(_dot(p.astype(BF16), vw) / l)
        lses.append(jnp.broadcast_to(m + jnp.log(l), (tq, LANES)))
    o_ref[...] = jnp.where(lane < HEAD_DIM, outs[0], outs[1])
    lse_ref[...] = jnp.where(lane < HEAD_DIM, lses[0], lses[1])


def _dil_prompt(q, k, v, window, dil):
    b, t, dk = q.shape
    assert t % dil == 0 and HEADS_PER_TILE == 2
    length = t // dil
    win = window // dil
    tq = min(256, length)
    width = min(tq + win, length)
    assert length % tq == 0 and win % LANES == 0 and tq % LANES == 0
    hp = dk // LANES
    view = lambda x: x.reshape(b, length, dil * dk)
    tile_spec = pl.BlockSpec((None, tq, LANES), lambda bi, r, h, i: (bi, i, r * hp + h))
    seq_spec = pl.BlockSpec((None, length, LANES), lambda bi, r, h, i: (bi, 0, r * hp + h))
    o, lse = pl.pallas_call(
        functools.partial(_dil_prompt_kernel, tq=tq, width=width, win=win),
        grid=(b, dil, hp, length // tq),
        in_specs=[tile_spec, seq_spec, seq_spec],
        out_specs=[tile_spec, tile_spec],
        out_shape=[jax.ShapeDtypeStruct((b, length, dil * dk), F32)] * 2,
        compiler_params=_params("arbitrary", "arbitrary", "arbitrary", "arbitrary"),
        name="dilated_prompt",
    )(view(q), view(k), view(v))
    return o.reshape(b, t, dk), lse.reshape(b, t, dk)


def _dil_sample_kernel(q0_ref, q1_ref, q2_ref, kn_ref, vn_ref, kc_ref, vc_ref, o_ref, s_scr, p_scr, *, n_heads):
    q_refs = (q0_ref, q1_ref, q2_ref)
    t_dec, dk = kn_ref.shape
    past = kc_ref.shape[0]
    pad = LANES
    rg = n_heads * t_dec
    head_mask = _head_lane_mask((rg, dk), t_dec)
    qrows = jnp.concatenate(
        [jnp.where(head_mask, jnp.concatenate([qr[...]] * n_heads, axis=0), 0.0) for qr in q_refs], axis=0)
    qs = (qrows * HEAD_DIM ** -0.5).astype(BF16)
    s_scr[:, :past] = _dot_nt(qs, kc_ref[...].astype(BF16))
    s_scr[:, past:] = _dot_nt(qs, _pad_rows(kn_ref[...], pad).astype(BF16))

    t_row = lax.broadcasted_iota(jnp.int32, (rg, past + pad), 0) % t_dec
    col = lax.broadcasted_iota(jnp.int32, (rg, past + pad), 1)
    delta = past + t_row - col
    in_range = col < past + t_dec
    l_g, lse_g = [], []
    for g, (window, dil) in enumerate(B_CONFIGS):
        rs = slice(g * rg, (g + 1) * rg)
        assert dil & (dil - 1) == 0
        valid = in_range & (delta >= 0) & (delta <= window) & ((delta & (dil - 1)) == 0)
        s = jnp.where(valid, s_scr[rs, :], NEG)
        m = jnp.max(s, axis=1, keepdims=True)
        p = jnp.exp(s - m)
        l = jnp.sum(p, axis=1, keepdims=True)
        p_scr[rs, :] = p.astype(BF16)
        l_g.append(l)
        lse_g.append(m + jnp.log(l))
    acc = (_dot(p_scr[:, :past], vc_ref[...].astype(BF16))
           + _dot(p_scr[:, past:], _pad_rows(vn_ref[...], pad).astype(BF16)))
    mx = functools.reduce(jnp.maximum, lse_g)
    e = [jnp.exp(x - mx) for x in lse_g]
    den = functools.reduce(lambda a, b: a + b, e)
    mixed = jnp.zeros((rg, dk), F32)
    for g in range(len(B_CONFIGS)):
        mixed = mixed + (e[g] / den) * (acc[g * rg:(g + 1) * rg] / l_g[g])
    mixed = jnp.where(head_mask, mixed, 0.0)
    out = mixed[0:t_dec]
    for h in range(1, n_heads):
        out = out + mixed[h * t_dec:(h + 1) * t_dec]
    o_ref[...] = out


def _dil_sample(q_groups, k_new, v_new, cache_k, cache_v):
    b, t_dec, dk = k_new.shape
    past = cache_k.shape[1]
    n_heads = dk // HEAD_DIM
    n_rows = len(B_CONFIGS) * n_heads * t_dec
    tok_spec = pl.BlockSpec((None, t_dec, dk), lambda i: (i, 0, 0))
    cache_spec = pl.BlockSpec((None, past, dk), lambda i: (i, 0, 0))
    return pl.pallas_call(
        functools.partial(_dil_sample_kernel, n_heads=n_heads),
        grid=(b,),
        in_specs=[tok_spec] * 5 + [cache_spec] * 2,
        out_specs=tok_spec,
        out_shape=jax.ShapeDtypeStruct((b, t_dec, dk), F32),
        scratch_shapes=[pltpu.VMEM((n_rows, past + LANES), F32),
                        pltpu.VMEM((n_rows, past + LANES), BF16)],
        compiler_params=_params("arbitrary"),
        name="dilated_sample",
    )(*q_groups, k_new, v_new, cache_k, cache_v)


def _trunk(x, mods, kv_mod, pos, past, weights):
    (norm_g, w_ffn_gate, w_ffn_up, w_ffn_down, w_qkv_a, w_o_a,
     kv_norm_g, w_kv_b, w_q_b, w_o_b, final_norm_g) = weights
    n_seq, t, d = x.shape
    m = n_seq * t
    depth = norm_g.shape[0]
    n_a = w_qkv_a.shape[0]
    dk = w_kv_b.shape[1] // 2
    tm = min(1024 if past is None else 512, m, t if t >= 256 else m)
    rows = _Rows(m, tm, t)
    tables = _rope_tables(pos)
    if rows.per_row:
        tables = tuple(jnp.tile(tb, (tm // t, 1)) for tb in tables)
        n_pos_tiles = 1
    else:
        n_pos_tiles = t // tm

    h = x.reshape(m, d)
    a_k, a_v = [], []
    b_k = b_v = None
    for l in range(depth):
        if l == n_a:
            sh, sc = (rows.prep(kv_mod[:, i * d:(i + 1) * d]) for i in range(2))
            b_k, b_v = _proj(rows, h, sh, sc, kv_norm_g, w_kv_b, tables, n_pos_tiles, (True, False))
        md = [[rows.prep(mods[l][:, (3 * s + c) * d:(3 * s + c + 1) * d]) for c in range(3)] for s in range(3)]
        h = _ffn(rows, h, md[0][0], md[0][1], md[0][2], norm_g[l, 0], w_ffn_gate, w_ffn_up, w_ffn_down, l, 0)
        if l < n_a:
            q, k, v = _proj(rows, h, md[1][0], md[1][1], norm_g[l, 1], w_qkv_a[l], tables, n_pos_tiles,
                            (True, True, False))
            a_k.append(k)
            a_v.append(v)
            shp = (n_seq, t, d)
            if past is None:
                o = _moba_prompt(q.reshape(shp), k.reshape(shp), v.reshape(shp))
            else:
                cache_k, cache_v, page_table = past[0], past[1], past[2]
                n_pool, page = cache_k.shape[1], cache_k.shape[2]
                o = _moba_sample(q.reshape(shp), k.reshape(shp), v.reshape(shp),
                                 cache_k[l].reshape(n_pool, page, d), cache_v[l].reshape(n_pool, page, d),
                                 page_table)
            h = _oproj(rows, h, md[1][2], w_o_a[l], [o.reshape(m, d)])
        else:
            lb = l - n_a
            qg = _proj(rows, h, md[1][0], md[1][1], norm_g[l, 1], w_q_b[lb], tables, n_pos_tiles,
                       (True,) * len(B_CONFIGS))
            shp = (n_seq, t, dk)
            if past is None:
                res = [_dil_prompt(qi.reshape(shp), b_k.reshape(shp), b_v.reshape(shp), w, dl)
                       for qi, (w, dl) in zip(qg, B_CONFIGS)]
                h = _oproj(rows, h, md[1][2], w_o_b[lb], [r[0].reshape(m, dk) for r in res],
                           [r[1].reshape(m, dk) for r in res])
            else:
                cb_k, cb_v = past[3], past[4]
                pl_ = cb_k.shape[1]
                o = _dil_sample([qi.reshape(shp) for qi in qg], b_k.reshape(shp), b_v.reshape(shp),
                                cb_k.reshape(n_seq, pl_, dk), cb_v.reshape(n_seq, pl_, dk))
                h = _oproj(rows, h, md[1][2], w_o_b[lb], [o.reshape(m, dk)])
        h = _ffn(rows, h, md[2][0], md[2][1], md[2][2], norm_g[l, 2], w_ffn_gate, w_ffn_up, w_ffn_down, l, 1,
                 final_g=final_norm_g if l == depth - 1 else None)
    return h.reshape(n_seq, t, d), a_k, a_v, b_k, b_v


def kernel(x_prompt, x_sample, cache_a_k, cache_a_v, cache_b_k, cache_b_v, page_table, c_prompt, c_sample, norm_g, w_mod, b_mod, w_ffn_gate, w_ffn_up, w_ffn_down, w_qkv_a, w_o_a, kv_norm_g, w_kv_mod, b_kv_mod, w_kv_b, w_q_b, w_o_b, final_norm_g):
    bp, t, d = x_prompt.shape
    bs, t_dec, _ = x_sample.shape
    n_a = w_qkv_a.shape[0]
    n_heads_a = d // HEAD_DIM
    n_heads_b = w_kv_b.shape[1] // (2 * HEAD_DIM)
    past_len = page_table.shape[1] * cache_a_k.shape[2]
    assert past_len % MOBA_BLOCK == 0 and t_dec <= MOBA_BLOCK

    n_c = bp + bs
    c_all = _pad_rows(jnp.concatenate([c_prompt, c_sample], axis=0), -(-n_c // 8) * 8)
    mods = _modulation(c_all, w_mod, b_mod)
    kv_mod = _modulation(c_all, w_kv_mod[None], b_kv_mod[None])[0]

    weights = (norm_g, w_ffn_gate, w_ffn_up, w_ffn_down, w_qkv_a, w_o_a,
               kv_norm_g, w_kv_b, w_q_b, w_o_b, final_norm_g)
    y_p, ak_p, av_p, bk_p, bv_p = _trunk(
        x_prompt, mods[:, :bp], kv_mod[:bp], jnp.arange(t, dtype=jnp.int32), None, weights)
    y_s, ak_s, av_s, bk_s, bv_s = _trunk(
        x_sample, mods[:, bp:n_c], kv_mod[bp:n_c], past_len + jnp.arange(t_dec, dtype=jnp.int32),
        (cache_a_k, cache_a_v, page_table, cache_b_k, cache_b_v), weights)

    keep = min(max(w for w, _ in B_CONFIGS), t)
    stack_a = lambda xs, n, tt: jnp.stack(xs).reshape(n_a, n, tt, n_heads_a, HEAD_DIM)
    shape_b = lambda x_, n, tt: x_.reshape(n, tt, n_heads_b, HEAD_DIM)
    return (y_p, y_s,
            stack_a(ak_p, bp, t), stack_a(av_p, bp, t),
            stack_a(ak_s, bs, t_dec), stack_a(av_s, bs, t_dec),
            shape_b(bk_p, bp, t)[:, -keep:], shape_b(bv_p, bp, t)[:, -keep:],
            shape_b(bk_s, bs, t_dec), shape_b(bv_s, bs, t_dec))
```

```python
import functools

import jax
import jax.numpy as jnp
from jax import lax
from jax.experimental import pallas as pl
from jax.experimental.pallas import tpu as pltpu

HEAD_DIM = 64
ROT_DIM = HEAD_DIM // 4
ROPE_THETA = 500000.0
MOBA_BLOCK = 256
MOBA_TOPK = 3
B_CONFIGS = ((128, 1), (512, 4), (2048, 16))
EPS = 1e-6
NEG = -1e30

LANES = 128
HEADS_PER_TILE = LANES // HEAD_DIM
VMEM_LIMIT = 56 * 1024 * 1024

F32 = jnp.float32
BF16 = jnp.bfloat16


def _params(*sem):
    return pltpu.CompilerParams(dimension_semantics=sem, vmem_limit_bytes=VMEM_LIMIT)


def _dot(a, b):
    return jnp.dot(a, b, preferred_element_type=F32)


def _dot_nt(a, b):
    return lax.dot_general(a, b, (((1,), (1,)), ((), ())), preferred_element_type=F32)


def _silu(x):
    return x * jax.nn.sigmoid(x)


def _rmsnorm(x, g):
    return x * lax.rsqrt(jnp.mean(x * x, axis=-1, keepdims=True) + EPS) * g


def _rms_mod(x, g, shift, scale):
    return _rmsnorm(x, g) * (1.0 + scale) + shift


def _head_lane_mask(shape, rows_per_head):
    row = lax.broadcasted_iota(jnp.int32, shape, 0)
    lane = lax.broadcasted_iota(jnp.int32, shape, 1)
    return (lane // HEAD_DIM) == (row // rows_per_head)


class _Rows:
    def __init__(self, m, tm, rows_per_seq):
        self.m, self.tm = m, tm
        self.per_row = rows_per_seq < tm
        self.rows_per_seq = rows_per_seq
        self.tiles_per_seq = max(rows_per_seq // tm, 1)

    def prep(self, vec):
        if self.per_row:
            return jnp.repeat(vec, self.rows_per_seq, axis=0)
        return vec[:, None, :]

    def mod_spec(self, d):
        if self.per_row:
            return pl.BlockSpec((self.tm, d), lambda i, j: (i, 0))
        tps = self.tiles_per_seq
        return pl.BlockSpec((None, 1, d), lambda i, j: (i // tps, 0, 0))

    def pos_spec(self, table_rows):
        n_pos_tiles = table_rows // self.tm
        return pl.BlockSpec((self.tm, LANES), lambda i, j: (i % n_pos_tiles, 0))


def _mod_kernel(c_ref, w_ref, b_ref, o_ref):
    cs = _silu(c_ref[...]).astype(BF16)
    o_ref[...] = _dot(cs, w_ref[...].astype(BF16)) + b_ref[...]


def _modulation(c, w, b):
    mc, d = c.shape
    nl, _, n = w.shape
    tn = d
    return pl.pallas_call(
        _mod_kernel,
        grid=(nl, n // tn),
        in_specs=[pl.BlockSpec((mc, d), lambda l, j: (0, 0)),
                  pl.BlockSpec((None, d, tn), lambda l, j: (l, 0, j)),
                  pl.BlockSpec((None, 1, tn), lambda l, j: (l, 0, j))],
        out_specs=pl.BlockSpec((None, mc, tn), lambda l, j: (l, 0, j)),
        out_shape=jax.ShapeDtypeStruct((nl, mc, n), F32),
        compiler_params=_params("arbitrary", "arbitrary"),
        name="modulation",
    )(c, w, b.reshape(nl, 1, n))


def _ffn_kernel(h_ref, sh_ref, sc_ref, gt_ref, g_ref, wg_ref, wu_ref, wd_ref, *rest, n_f, final):
    if final:
        fg_ref, o_ref, u_scr, acc_scr = rest
    else:
        o_ref, u_scr, acc_scr = rest
    f = pl.program_id(1)

    @pl.when(f == 0)
    def _():
        u_scr[...] = _rms_mod(h_ref[...], g_ref[...], sh_ref[...], sc_ref[...]).astype(BF16)
        acc_scr[...] = jnp.zeros_like(acc_scr)

    u = u_scr[...]
    a = _dot(u, wg_ref[...].astype(BF16))
    b = _dot(u, wu_ref[...].astype(BF16))
    act = (_silu(a) * b).astype(BF16)
    acc_scr[...] += _dot(act, wd_ref[...].astype(BF16))

    @pl.when(f == n_f - 1)
    def _():
        hn = h_ref[...] + 0.5 * gt_ref[...] * acc_scr[...]
        if final:
            hn = _rmsnorm(hn, fg_ref[...])
        o_ref[...] = hn


def _ffn(rows, h, shift, scale, gate, g, wg, wu, wd, l, s, final_g=None):
    m, d = h.shape
    ff = wg.shape[-1]
    tf = 256 if ff % 256 == 0 else LANES
    n_f = ff // tf
    tm = rows.tm
    final = final_g is not None
    in_specs = [pl.BlockSpec((tm, d), lambda i, f: (i, 0)),
                rows.mod_spec(d), rows.mod_spec(d), rows.mod_spec(d),
                pl.BlockSpec((1, d), lambda i, f: (0, 0)),
                pl.BlockSpec((None, None, d, tf), lambda i, f: (l, s, 0, f)),
                pl.BlockSpec((None, None, d, tf), lambda i, f: (l, s, 0, f)),
                pl.BlockSpec((None, None, tf, d), lambda i, f: (l, s, f, 0))]
    args = [h, shift, scale, gate, g.reshape(1, d), wg, wu, wd]
    if final:
        in_specs.append(pl.BlockSpec((1, d), lambda i, f: (0, 0)))
        args.append(final_g.reshape(1, d))
    return pl.pallas_call(
        functools.partial(_ffn_kernel, n_f=n_f, final=final),
        grid=(m // tm, n_f),
        in_specs=in_specs,
        out_specs=pl.BlockSpec((tm, d), lambda i, f: (i, 0)),
        out_shape=jax.ShapeDtypeStruct((m, d), F32),
        scratch_shapes=[pltpu.VMEM((tm, d), BF16), pltpu.VMEM((tm, d), F32)],
        compiler_params=_params("arbitrary", "arbitrary"),
        name="ffn",
    )(*args)


def _rope_tables(pos):
    half = ROT_DIM // 2
    n = pos.shape[0]
    freq = ROPE_THETA ** (-jnp.arange(half, dtype=F32) / half)
    ang = pos.astype(F32)[:, None] * freq[None, :]
    cos, sin = jnp.cos(ang), jnp.sin(ang)
    rest = HEAD_DIM - ROT_DIM
    ct = jnp.concatenate([cos, cos, jnp.ones((n, rest), F32)], axis=1)
    s_lo = jnp.concatenate([-sin, jnp.zeros((n, half + rest), F32)], axis=1)
    s_hi = jnp.concatenate([jnp.zeros((n, half), F32), sin, jnp.zeros((n, rest), F32)], axis=1)
    tile = lambda t: jnp.tile(t, (1, HEADS_PER_TILE))
    return tile(ct), tile(s_hi), tile(s_lo)


def _rope_tile(x, ct, s_hi, s_lo):
    half = ROT_DIM // 2
    return x * ct + pltpu.roll(x, half, 1) * s_hi + pltpu.roll(x, LANES - half, 1) * s_lo


_MEAN_GROUP = 8


def _proj_kernel(h_ref, sh_ref, sc_ref, g_ref, w_ref, ct_ref, shi_ref, slo_ref, *rest, parts):
    u_scr = rest[-1]
    i, j = pl.program_id(0), pl.program_id(1)

    @pl.when(j == 0)
    def _():
        u_scr[...] = _rms_mod(h_ref[...], g_ref[...], sh_ref[...], sc_ref[...]).astype(BF16)

    y = _dot(u_scr[...], w_ref[...].astype(BF16))
    tm = y.shape[0]
    blocks = tm // MOBA_BLOCK
    n_out = 0
    for p, (rope, forms) in enumerate(parts):
        outs = rest[n_out:n_out + len(forms)]
        n_out += len(forms)

        @pl.when(j == p)
        def _(rope=rope, forms=forms, outs=outs):
            val = y
            if rope:
                ct, s_hi, s_lo = ct_ref[...], shi_ref[...], slo_ref[...]
                val = jnp.concatenate(
                    [_rope_tile(y[:, c * LANES:(c + 1) * LANES], ct, s_hi, s_lo)
                     for c in range(y.shape[1] // LANES)], axis=1)
            val_t = val.T if any(f.startswith("t_") for f in forms) else None
            for form, out in zip(forms, outs):
                if form == "f32":
                    out[...] = val
                elif form == "bf16":
                    out[...] = val.astype(BF16)
                elif form == "t_f32":
                    out[...] = val_t
                elif form == "t_bf16":
                    for c in range(blocks):
                        out[c] = val_t[:, c * MOBA_BLOCK:(c + 1) * MOBA_BLOCK].astype(BF16)
                elif form == "block_mean":
                    base = (i % (_MEAN_GROUP // blocks)) * blocks
                    for c in range(blocks):
                        out[pl.ds(base + c, 1), :] = jnp.mean(
                            val[c * MOBA_BLOCK:(c + 1) * MOBA_BLOCK], axis=0, keepdims=True)


def _proj(rows, h, shift, scale, g, w, tables, parts):
    m, d = h.shape
    n_p = w.shape[1] // len(parts)
    tm, tps, t = rows.tm, rows.tiles_per_seq, rows.rows_per_seq
    n_seq = m // t
    out_specs, out_shape = [], []
    for _, forms in parts:
        for form in forms:
            if form in ("f32", "bf16"):
                out_specs.append(pl.BlockSpec((tm, n_p), lambda i, j: (i, 0)))
                out_shape.append(jax.ShapeDtypeStruct((m, n_p), F32 if form == "f32" else BF16))
                continue
            assert not rows.per_row and tm % MOBA_BLOCK == 0
            if form == "t_f32":
                out_specs.append(pl.BlockSpec((None, n_p, tm), lambda i, j: (i // tps, 0, i % tps)))
                out_shape.append(jax.ShapeDtypeStruct((n_seq, n_p, t), F32))
            elif form == "t_bf16":
                out_specs.append(pl.BlockSpec((None, tm // MOBA_BLOCK, n_p, MOBA_BLOCK),
                                              lambda i, j: (i // tps, i % tps, 0, 0)))
                out_shape.append(jax.ShapeDtypeStruct((n_seq, t // MOBA_BLOCK, n_p, MOBA_BLOCK), BF16))
            else:
                steps = _MEAN_GROUP * MOBA_BLOCK // tm
                assert form == "block_mean" and tps % steps == 0
                out_specs.append(pl.BlockSpec((None, _MEAN_GROUP, n_p),
                                              lambda i, j: (i // tps, (i % tps) // steps, 0)))
                out_shape.append(jax.ShapeDtypeStruct((n_seq, t // MOBA_BLOCK, n_p), F32))
    pos_spec = rows.pos_spec(tables[0].shape[0])
    return pl.pallas_call(
        functools.partial(_proj_kernel, parts=parts),
        grid=(m // tm, len(parts)),
        in_specs=[pl.BlockSpec((tm, d), lambda i, j: (i, 0)),
                  rows.mod_spec(d), rows.mod_spec(d),
                  pl.BlockSpec((1, d), lambda i, j: (0, 0)),
                  pl.BlockSpec((d, n_p), lambda i, j: (0, j)),
                  pos_spec, pos_spec, pos_spec],
        out_specs=out_specs,
        out_shape=out_shape,
        scratch_shapes=[pltpu.VMEM((tm, d), BF16)],
        compiler_params=_params("arbitrary", "arbitrary"),
        name="proj",
    )(h, shift, scale, g.reshape(1, d), w, *tables)


def _oproj_kernel(h_ref, gt_ref, w_ref, *rest, n_groups):
    o_ref = rest[-1]
    if n_groups == 0:
        o = rest[0][...]
    else:
        os_, lses = rest[:n_groups], rest[n_groups:2 * n_groups]
        lse = [r[...] for r in lses]
        mx = functools.reduce(jnp.maximum, lse)
        e = [jnp.exp(x - mx) for x in lse]
        den = functools.reduce(lambda a, b: a + b, e)
        o = functools.reduce(lambda a, b: a + b, [(ei / den) * r[...] for ei, r in zip(e, os_)])
    o_ref[...] = h_ref[...] + gt_ref[...] * _dot(o.astype(BF16), w_ref[...].astype(BF16))


def _oproj(rows, h, gate, w, o_list, lse_list=()):
    m, d = h.shape
    k = w.shape[0]
    tm = rows.tm
    xs = list(o_list) + list(lse_list)
    return pl.pallas_call(
        functools.partial(_oproj_kernel, n_groups=len(lse_list)),
        grid=(m // tm, 1),
        in_specs=[pl.BlockSpec((tm, d), lambda i, j: (i, 0)),
                  rows.mod_spec(d),
                  pl.BlockSpec((k, d), lambda i, j: (0, 0))]
                 + [pl.BlockSpec((tm, k), lambda i, j: (i, 0))] * len(xs),
        out_specs=pl.BlockSpec((tm, d), lambda i, j: (i, 0)),
        out_shape=jax.ShapeDtypeStruct((m, d), F32),
        compiler_params=_params("arbitrary", "arbitrary"),
        name="oproj",
    )(h, gate, w, *xs)


def _split_bf16(x):
    hi = x.astype(BF16)
    lo = (x - hi.astype(F32)).astype(BF16)
    return hi, lo


def _top_blocks_t(gate, n_past):
    blk_id = lax.broadcasted_iota(jnp.int32, gate.shape, 0).astype(F32)
    n_past = n_past.astype(F32)
    g = jnp.where(blk_id < n_past, gate, NEG)
    sel = jnp.zeros(gate.shape, jnp.bool_)
    for _ in range(MOBA_TOPK):
        mx = jnp.max(g, axis=0, keepdims=True)
        idx = jnp.min(jnp.where(g == mx, blk_id, float(gate.shape[0])), axis=0, keepdims=True)
        pick = blk_id == idx
        sel = sel | pick
        g = jnp.where(pick, -jnp.inf, g)
    return sel & (blk_id < n_past)


def _moba_prompt_kernel(q_ref, k_ref, vt_ref, km_ref, o_ref, qt_scr, sel_scr, m_scr, l_scr, acc_scr):
    blk = MOBA_BLOCK
    n_h, width = qt_scr.shape[0], qt_scr.shape[1]
    i = pl.program_id(2)
    qt = q_ref[...].T
    feat = lax.broadcasted_iota(jnp.int32, (width, blk), 0)
    km_hi, km_lo = _split_bf16(km_ref[...])
    own = pl.multiple_of(i * blk, blk)
    kd = k_ref[pl.ds(own, blk), :]
    vtd = vt_ref[i]
    key_id = lax.broadcasted_iota(jnp.int32, (blk, blk), 0)
    qry_id = lax.broadcasted_iota(jnp.int32, (blk, blk), 1)
    heads = range(n_h)
    hs = [slice(hh * HEAD_DIM, (hh + 1) * HEAD_DIM) for hh in heads]
    for hh in heads:
        qth = jnp.where((feat // HEAD_DIM) == hh, qt, 0.0)
        q_hi, q_lo = _split_bf16(qth)
        gate = _dot(km_hi, q_hi) + _dot(km_lo, q_hi) + _dot(km_hi, q_lo)
        sel_scr[hh] = _top_blocks_t(gate, i).astype(F32)
        qs = (qth * HEAD_DIM ** -0.5).astype(BF16)
        qt_scr[hh] = qs
        s = jnp.where(key_id <= qry_id, _dot(kd, qs), NEG)
        m = jnp.max(s, axis=0, keepdims=True)
        p = jnp.exp(s - m)
        m_scr[hh] = m
        l_scr[hh] = jnp.sum(p, axis=0, keepdims=True)
        acc_scr[hh] = _dot(vtd[hs[hh], :], p.astype(BF16))

    def past_block(j, carry):
        kb = k_ref[pl.ds(pl.multiple_of(j * blk, blk), blk), :]
        vtb = vt_ref[j]
        scores = [_dot(kb, qt_scr[hh]) for hh in heads]
        ps, alphas, chosen = [], [], []
        for hh in heads:
            ch = sel_scr[hh, pl.ds(j, 1), :] > 0.5
            m_old = m_scr[hh]
            m_all = jnp.maximum(m_old, jnp.max(scores[hh], axis=0, keepdims=True))
            p = jnp.exp(scores[hh] - m_all)
            m_new = jnp.where(ch, m_all, m_old)
            alpha = jnp.exp(m_old - m_new)
            l_scr[hh] = alpha * l_scr[hh] + jnp.where(ch, jnp.sum(p, axis=0, keepdims=True), 0.0)
            m_scr[hh] = m_new
            ps.append(p.astype(BF16))
            alphas.append(alpha)
            chosen.append(ch)
        pvs = [_dot(vtb[hs[hh], :], ps[hh]) for hh in heads]
        for hh in heads:
            acc_scr[hh] = alphas[hh] * acc_scr[hh] + jnp.where(chosen[hh], pvs[hh], 0.0)
        return carry

    lax.fori_loop(0, i, past_block, 0)
    ot = jnp.concatenate([acc_scr[hh] / l_scr[hh] for hh in heads], axis=0)
    o_ref[...] = ot.T


def _moba_prompt(q, k_bf, vt_bf, km):
    b, t, d = q.shape
    blk = MOBA_BLOCK
    nb = t // blk
    width = min(d, 2 * LANES)
    n_h = width // HEAD_DIM
    assert t % blk == 0 and nb % 8 == 0 and d % width == 0
    tile_spec = pl.BlockSpec((None, blk, width), lambda bi, hg, i: (bi, i, hg))
    return pl.pallas_call(
        _moba_prompt_kernel,
        grid=(b, d // width, nb),
        in_specs=[tile_spec,
                  pl.BlockSpec((None, t, width), lambda bi, hg, i: (bi, 0, hg)),
                  pl.BlockSpec((None, nb, width, blk), lambda bi, hg, i: (bi, 0, hg, 0)),
                  pl.BlockSpec((None, nb, width), lambda bi, hg, i: (bi, 0, hg))],
        out_specs=tile_spec,
        out_shape=jax.ShapeDtypeStruct((b, t, d), F32),
        scratch_shapes=[pltpu.VMEM((n_h, width, blk), BF16),
                        pltpu.VMEM((n_h, nb, blk), F32),
                        pltpu.VMEM((n_h, 1, blk), F32),
                        pltpu.VMEM((n_h, 1, blk), F32),
                        pltpu.VMEM((n_h, HEAD_DIM, blk), F32)],
        compiler_params=_params("arbitrary", "arbitrary", "arbitrary"),
        name="moba_prompt",
    )(q, k_bf, vt_bf, km)


def _pad_rows(x, n):
    return jnp.concatenate([x, jnp.zeros((n - x.shape[0], x.shape[1]), x.dtype)], axis=0)


def _moba_sample_kernel(pt_ref, q_ref, kn_ref, vn_ref, *rest, n_pages, page, n_heads):
    k_pages, v_pages = rest[:n_pages], rest[n_pages:2 * n_pages]
    o_ref, s_scr = rest[2 * n_pages], rest[2 * n_pages + 1]
    t_dec, d = q_ref.shape
    r = n_heads * t_dec
    ppb = MOBA_BLOCK // page
    n_past = n_pages // ppb
    own = lax.broadcasted_iota(jnp.int32, (r, d), 0) // t_dec
    head_mask = (lax.broadcasted_iota(jnp.int32, (r, d), 1) // HEAD_DIM) == own
    qrows = jnp.where(head_mask, jnp.concatenate([q_ref[...]] * n_heads, axis=0), 0.0)
    qs = (qrows * HEAD_DIM ** -0.5).astype(BF16)

    gates = []
    for j in range(n_past):
        for p in range(j * ppb, (j + 1) * ppb):
            s_scr[:, p * page:(p + 1) * page] = _dot(qs, k_pages[p][...].astype(BF16))
        gates.append(jnp.sum(s_scr[:, j * MOBA_BLOCK:(j + 1) * MOBA_BLOCK], axis=1, keepdims=True))
    for j in range(n_past):
        rank = jnp.zeros((r, 1), jnp.int32)
        for j2 in range(n_past):
            if j2 != j:
                ahead = (gates[j2] > gates[j]) | ((gates[j2] == gates[j]) & (j2 < j))
                rank = rank + ahead.astype(jnp.int32)
        chosen = rank < MOBA_TOPK
        sl = slice(j * MOBA_BLOCK, (j + 1) * MOBA_BLOCK)
        s_scr[:, sl] = jnp.where(chosen, s_scr[:, sl], NEG)
    t_row = lax.broadcasted_iota(jnp.int32, (r, page), 0) % t_dec
    t_col = lax.broadcasted_iota(jnp.int32, (r, page), 1)
    s_own = _dot_nt(qs, _pad_rows(kn_ref[...], page).astype(BF16))
    s_scr[:, n_pages * page:] = jnp.where(t_col <= t_row, s_own, NEG)

    s = s_scr[...]
    m = jnp.max(s, axis=1, keepdims=True)
    p_all = jnp.exp(s - m)
    l = jnp.sum(p_all, axis=1, keepdims=True)
    acc = _dot(p_all[:, n_pages * page:].astype(BF16), _pad_rows(vn_ref[...], page).astype(BF16))
    for p in range(n_pages):
        acc = acc + _dot_nt(p_all[:, p * page:(p + 1) * page].astype(BF16), v_pages[p][...].astype(BF16))
    o_full = jnp.where(head_mask, acc / l, 0.0)
    out = o_full[0:t_dec]
    for h in range(1, n_heads):
        out = out + o_full[h * t_dec:(h + 1) * t_dec]
    o_ref[...] = out


def _moba_sample(q, k_new, v_new, cache_kt, cache_vt, layer, page_table):
    b, t_dec, d = q.shape
    n_pages = page_table.shape[1]
    page = cache_kt.shape[3]
    n_heads = d // HEAD_DIM
    assert MOBA_BLOCK % page == 0 and (n_pages * page) % MOBA_BLOCK == 0 and t_dec <= page
    tok_spec = pl.BlockSpec((None, t_dec, d), lambda i, pt: (i, 0, 0))
    page_specs = [pl.BlockSpec((None, None, d, page),
                               functools.partial(lambda i, pt, p: (layer, pt[i, p], 0, 0), p=p))
                  for p in range(n_pages)]
    grid_spec = pltpu.PrefetchScalarGridSpec(
        num_scalar_prefetch=1,
        grid=(b,),
        in_specs=[tok_spec, tok_spec, tok_spec] + page_specs + page_specs,
        out_specs=tok_spec,
        scratch_shapes=[pltpu.VMEM((n_heads * t_dec, (n_pages + 1) * page), F32)],
    )
    return pl.pallas_call(
        functools.partial(_moba_sample_kernel, n_pages=n_pages, page=page, n_heads=n_heads),
        grid_spec=grid_spec,
        out_shape=jax.ShapeDtypeStruct((b, t_dec, d), F32),
        compiler_params=_params("arbitrary"),
        name="moba_sample",
    )(page_table, q, k_new, v_new, *([cache_kt] * n_pages), *([cache_vt] * n_pages))


def _dil_prompt_kernel(q_ref, k_ref, v_ref, o_ref, lse_ref, *, dil, tq, width, win):
    length = k_ref.shape[0] // dil
    q0 = pl.program_id(2) * tq
    ks = jnp.clip(q0 - win, 0, length - width)
    lane = lax.broadcasted_iota(jnp.int32, (tq, LANES), 1)
    delta = (q0 + lax.broadcasted_iota(jnp.int32, (tq, width), 0)
             - ks - lax.broadcasted_iota(jnp.int32, (tq, width), 1))
    valid = (delta >= 0) & (delta <= win)

    def rows(start, n):
        return pl.ds(start, n) if dil == 1 else pl.ds(start, n, stride=dil)

    def one_class(r, carry):
        q = q_ref[rows(r, tq), :]
        kw = k_ref[rows(ks * dil + r, width), :].astype(BF16)
        vw = v_ref[rows(ks * dil + r, width), :].astype(BF16)
        outs, lses = [], []
        for hh in range(HEADS_PER_TILE):
            qh = (jnp.where((lane // HEAD_DIM) == hh, q, 0.0) * HEAD_DIM ** -0.5).astype(BF16)
            s = jnp.where(valid, _dot_nt(qh, kw), NEG)
            m = jnp.max(s, axis=1, keepdims=True)
            p = jnp.exp(s - m)
            l = jnp.sum(p, axis=1, keepdims=True)
            outs.append(_dot(p.astype(BF16), vw) / l)
            lses.append(jnp.broadcast_to(m + jnp.log(l), (tq, LANES)))
        o_ref[rows(r, tq), :] = jnp.where(lane < HEAD_DIM, outs[0], outs[1])
        lse_ref[rows(r, tq), :] = jnp.where(lane < HEAD_DIM, lses[0], lses[1])
        return carry

    lax.fori_loop(0, dil, one_class, 0)


def _dil_prompt(q, k, v, window, dil):
    b, t, dk = q.shape
    assert t % dil == 0 and HEADS_PER_TILE == 2
    length = t // dil
    win = window // dil
    tq = min(256, length)
    width = min(tq + win, length)
    assert length % tq == 0
    tile_spec = pl.BlockSpec((None, tq * dil, LANES), lambda bi, h, i: (bi, i, h))
    seq_spec = pl.BlockSpec((None, t, LANES), lambda bi, h, i: (bi, 0, h))
    return pl.pallas_call(
        functools.partial(_dil_prompt_kernel, dil=dil, tq=tq, width=width, win=win),
        grid=(b, dk // LANES, length // tq),
        in_specs=[tile_spec, seq_spec, seq_spec],
        out_specs=[tile_spec, tile_spec],
        out_shape=[jax.ShapeDtypeStruct((b, t, dk), F32)] * 2,
        compiler_params=_params("arbitrary", "arbitrary", "arbitrary"),
        name="dilated_prompt",
    )(q, k, v)


def _dil_sample_kernel(q0_ref, q1_ref, q2_ref, kn_ref, vn_ref, kc_ref, vc_ref, o_ref, s_scr, p_scr, *, n_heads):
    q_refs = (q0_ref, q1_ref, q2_ref)
    t_dec, dk = kn_ref.shape
    past = kc_ref.shape[1]
    pad = LANES
    rg = n_heads * t_dec
    head_mask = _head_lane_mask((rg, dk), t_dec)
    qrows = jnp.concatenate(
        [jnp.where(head_mask, jnp.concatenate([qr[...]] * n_heads, axis=0), 0.0) for qr in q_refs], axis=0)
    qs = (qrows * HEAD_DIM ** -0.5).astype(BF16)
    s_scr[:, :past] = _dot(qs, kc_ref[...].astype(BF16))
    s_scr[:, past:] = _dot_nt(qs, _pad_rows(kn_ref[...], pad).astype(BF16))

    t_row = lax.broadcasted_iota(jnp.int32, (rg, past + pad), 0) % t_dec
    col = lax.broadcasted_iota(jnp.int32, (rg, past + pad), 1)
    delta = past + t_row - col
    in_range = col < past + t_dec
    l_g, lse_g = [], []
    for g, (window, dil) in enumerate(B_CONFIGS):
        rs = slice(g * rg, (g + 1) * rg)
        assert dil & (dil - 1) == 0
        valid = in_range & (delta >= 0) & (delta <= window) & ((delta & (dil - 1)) == 0)
        s = jnp.where(valid, s_scr[rs, :], NEG)
        m = jnp.max(s, axis=1, keepdims=True)
        p = jnp.exp(s - m)
        l = jnp.sum(p, axis=1, keepdims=True)
        p_scr[rs, :] = p.astype(BF16)
        l_g.append(l)
        lse_g.append(m + jnp.log(l))
    acc = (_dot_nt(p_scr[:, :past], vc_ref[...].astype(BF16))
           + _dot(p_scr[:, past:], _pad_rows(vn_ref[...], pad).astype(BF16)))
    mx = functools.reduce(jnp.maximum, lse_g)
    e = [jnp.exp(x - mx) for x in lse_g]
    den = functools.reduce(lambda a, b: a + b, e)
    mixed = jnp.zeros((rg, dk), F32)
    for g in range(len(B_CONFIGS)):
        mixed = mixed + (e[g] / den) * (acc[g * rg:(g + 1) * rg] / l_g[g])
    mixed = jnp.where(head_mask, mixed, 0.0)
    out = mixed[0:t_dec]
    for h in range(1, n_heads):
        out = out + mixed[h * t_dec:(h + 1) * t_dec]
    o_ref[...] = out


def _dil_sample(q_groups, k_new, v_new, cache_kt, cache_vt):
    b, t_dec, dk = k_new.shape
    past = cache_kt.shape[2]
    n_heads = dk // HEAD_DIM
    n_rows = len(B_CONFIGS) * n_heads * t_dec
    tok_spec = pl.BlockSpec((None, t_dec, dk), lambda i: (i, 0, 0))
    cache_spec = pl.BlockSpec((None, dk, past), lambda i: (i, 0, 0))
    return pl.pallas_call(
        functools.partial(_dil_sample_kernel, n_heads=n_heads),
        grid=(b,),
        in_specs=[tok_spec] * 5 + [cache_spec] * 2,
        out_specs=tok_spec,
        out_shape=jax.ShapeDtypeStruct((b, t_dec, dk), F32),
        scratch_shapes=[pltpu.VMEM((n_rows, past + LANES), F32),
                        pltpu.VMEM((n_rows, past + LANES), BF16)],
        compiler_params=_params("arbitrary"),
        name="dilated_sample",
    )(*q_groups, k_new, v_new, cache_kt, cache_vt)


def _trunk(x, mods, kv_mod, pos, past, weights):
    (norm_g, w_ffn_gate, w_ffn_up, w_ffn_down, w_qkv_a, w_o_a,
     kv_norm_g, w_kv_b, w_q_b, w_o_b, final_norm_g) = weights
    n_seq, t, d = x.shape
    m = n_seq * t
    depth = norm_g.shape[0]
    n_a = w_qkv_a.shape[0]
    dk = w_kv_b.shape[1] // 2
    is_prompt = past is None
    short = t < 256
    rows = _Rows(m, min(1024 if is_prompt else 512, m if short else t), t)
    rows_p = _Rows(m, min(512, m if short else t), t)
    tables = _rope_tables(pos)
    if rows.per_row:
        tables = tuple(jnp.tile(tb, (n_seq, 1)) for tb in tables)

    def heads_last(x_t):
        return jnp.transpose(x_t.reshape(n_seq, -1, HEAD_DIM, x_t.shape[-1]), (0, 3, 1, 2))

    h = x.reshape(m, d)
    a_k, a_v = [], []
    b_k = b_v = b_k_out = b_v_out = None
    for l in range(depth):
        if l == n_a:
            sh, sc = (rows.prep(kv_mod[:, i * d:(i + 1) * d]) for i in range(2))
            if is_prompt:
                b_k, b_kt, b_v, b_vt = _proj(rows_p, h, sh, sc, kv_norm_g, w_kv_b, tables,
                                             ((True, ("f32", "t_f32")), (False, ("f32", "t_f32"))))
                b_k_out, b_v_out = heads_last(b_kt), heads_last(b_vt)
            else:
                b_k, b_v = _proj(rows_p, h, sh, sc, kv_norm_g, w_kv_b, tables,
                                 ((True, ("f32",)), (False, ("f32",))))
                b_k_out, b_v_out = (r.reshape(n_seq, t, -1, HEAD_DIM) for r in (b_k, b_v))
        md = [[rows.prep(mods[l][:, (3 * s + c) * d:(3 * s + c + 1) * d]) for c in range(3)] for s in range(3)]
        h = _ffn(rows, h, md[0][0], md[0][1], md[0][2], norm_g[l, 0], w_ffn_gate, w_ffn_up, w_ffn_down, l, 0)
        if l < n_a:
            shp = (n_seq, t, d)
            if is_prompt:
                q, k_t, k_bf, km, v_t, vt_bf = _proj(
                    rows_p, h, md[1][0], md[1][1], norm_g[l, 1], w_qkv_a[l], tables,
                    ((True, ("f32",)), (True, ("t_f32", "bf16", "block_mean")), (False, ("t_f32", "t_bf16"))))
                o = _moba_prompt(q.reshape(shp), k_bf.reshape(shp), vt_bf, km)
                a_k.append(heads_last(k_t))
                a_v.append(heads_last(v_t))
            else:
                q, k, v = _proj(rows_p, h, md[1][0], md[1][1], norm_g[l, 1], w_qkv_a[l], tables,
                                ((True, ("f32",)), (True, ("f32",)), (False, ("f32",))))
                cache_kt, cache_vt = (
                    jnp.transpose(c, (0, 1, 3, 4, 2)).reshape(c.shape[0], c.shape[1], d, c.shape[2])
                    for c in past[:2])
                o = _moba_sample(q.reshape(shp), k.reshape(shp), v.reshape(shp), cache_kt, cache_vt, l, past[2])
                a_k.append(k.reshape(n_seq, t, -1, HEAD_DIM))
                a_v.append(v.reshape(n_seq, t, -1, HEAD_DIM))
            h = _oproj(rows, h, md[1][2], w_o_a[l], [o.reshape(m, d)])
        else:
            lb = l - n_a
            qg = _proj(rows_p, h, md[1][0], md[1][1], norm_g[l, 1], w_q_b[lb], tables,
                       ((True, ("f32",)),) * len(B_CONFIGS))
            shp = (n_seq, t, dk)
            if is_prompt:
                res = [_dil_prompt(qi.reshape(shp), b_k.reshape(shp), b_v.reshape(shp), w, dl)
                       for qi, (w, dl) in zip(qg, B_CONFIGS)]
                h = _oproj(rows, h, md[1][2], w_o_b[lb], [r[0].reshape(m, dk) for r in res],
                           [r[1].reshape(m, dk) for r in res])
            else:
                cb_kt, cb_vt = (jnp.transpose(c, (0, 2, 3, 1)).reshape(n_seq, dk, c.shape[1]) for c in past[3:5])
                o = _dil_sample([qi.reshape(shp) for qi in qg], b_k.reshape(shp), b_v.reshape(shp), cb_kt, cb_vt)
                h = _oproj(rows, h, md[1][2], w_o_b[lb], [o.reshape(m, dk)])
        h = _ffn(rows, h, md[2][0], md[2][1], md[2][2], norm_g[l, 2], w_ffn_gate, w_ffn_up, w_ffn_down, l, 1,
                 final_g=final_norm_g if l == depth - 1 else None)
    return h.reshape(n_seq, t, d), jnp.stack(a_k), jnp.stack(a_v), b_k_out, b_v_out


def kernel(x_prompt, x_sample, cache_a_k, cache_a_v, cache_b_k, cache_b_v, page_table, c_prompt, c_sample, norm_g, w_mod, b_mod, w_ffn_gate, w_ffn_up, w_ffn_down, w_qkv_a, w_o_a, kv_norm_g, w_kv_mod, b_kv_mod, w_kv_b, w_q_b, w_o_b, final_norm_g):
    bp, t, d = x_prompt.shape
    bs, t_dec, _ = x_sample.shape
    n_a = w_qkv_a.shape[0]
    n_heads_a = d // HEAD_DIM
    n_heads_b = w_kv_b.shape[1] // (2 * HEAD_DIM)
    past_len = page_table.shape[1] * cache_a_k.shape[2]
    assert past_len % MOBA_BLOCK == 0 and t_dec <= MOBA_BLOCK

    n_c = bp + bs
    c_all = _pad_rows(jnp.concatenate([c_prompt, c_sample], axis=0), -(-n_c // 8) * 8)
    mods = _modulation(c_all, w_mod, b_mod)
    kv_mod = _modulation(c_all, w_kv_mod[None], b_kv_mod[None])[0]

    weights = (norm_g, w_ffn_gate, w_ffn_up, w_ffn_down, w_qkv_a, w_o_a,
               kv_norm_g, w_kv_b, w_q_b, w_o_b, final_norm_g)
    y_p, ak_p, av_p, bk_p, bv_p = _trunk(
        x_prompt, mods[:, :bp], kv_mod[:bp], jnp.arange(t, dtype=jnp.int32), None, weights)
    y_s, ak_s, av_s, bk_s, bv_s = _trunk(
        x_sample, mods[:, bp:n_c], kv_mod[bp:n_c], past_len + jnp.arange(t_dec, dtype=jnp.int32),
        (cache_a_k, cache_a_v, page_table, cache_b_k, cache_b_v), weights)

    keep = min(max(w for w, _ in B_CONFIGS), t)
    return (y_p, y_s, ak_p, av_p, ak_s, av_s, bk_p[:, -keep:], bv_p[:, -keep:], bk_s, bv_s)
```

```python
import functools

import jax
import jax.numpy as jnp
from jax import lax
from jax.experimental import pallas as pl
from jax.experimental.pallas import tpu as pltpu

HEAD_DIM = 64
ROT_DIM = HEAD_DIM // 4
ROPE_THETA = 500000.0
MOBA_BLOCK = 256
MOBA_TOPK = 3
B_CONFIGS = ((128, 1), (512, 4), (2048, 16))
EPS = 1e-6
NEG = -1e30
LOG2_E = 1.4426950408889634

LANES = 128
HEADS_PER_TILE = LANES // HEAD_DIM
VMEM_LIMIT = 56 * 1024 * 1024

F32 = jnp.float32
BF16 = jnp.bfloat16


def _params(*sem):
    return pltpu.CompilerParams(dimension_semantics=sem, vmem_limit_bytes=VMEM_LIMIT)


def _dot(a, b):
    return jnp.dot(a, b, preferred_element_type=F32)


def _dot_nt(a, b):
    return lax.dot_general(a, b, (((1,), (1,)), ((), ())), preferred_element_type=F32)


def _silu(x):
    return x * jax.nn.sigmoid(x)


def _rmsnorm(x, g):
    return x * lax.rsqrt(jnp.mean(x * x, axis=-1, keepdims=True) + EPS) * g


def _rms_mod(x, g, shift, scale):
    return _rmsnorm(x, g) * (1.0 + scale) + shift


def _head_lane_mask(shape, rows_per_head):
    row = lax.broadcasted_iota(jnp.int32, shape, 0)
    lane = lax.broadcasted_iota(jnp.int32, shape, 1)
    return (lane // HEAD_DIM) == (row // rows_per_head)


class _Rows:
    def __init__(self, m, tm, rows_per_seq):
        self.m, self.tm = m, tm
        self.per_row = rows_per_seq < tm
        self.rows_per_seq = rows_per_seq
        self.tiles_per_seq = max(rows_per_seq // tm, 1)

    def prep(self, vec):
        if self.per_row:
            return jnp.repeat(vec, self.rows_per_seq, axis=0)
        return vec[:, None, :]

    def mod_spec(self, d):
        if self.per_row:
            return pl.BlockSpec((self.tm, d), lambda i, j: (i, 0))
        tps = self.tiles_per_seq
        return pl.BlockSpec((None, 1, d), lambda i, j: (i // tps, 0, 0))

    def pos_spec(self, table_rows):
        n_pos_tiles = table_rows // self.tm
        return pl.BlockSpec((self.tm, LANES), lambda i, j: (i % n_pos_tiles, 0))


def _mod_kernel(c_ref, w_ref, b_ref, o_ref):
    cs = _silu(c_ref[...]).astype(BF16)
    o_ref[...] = _dot(cs, w_ref[...].astype(BF16)) + b_ref[...]


def _modulation(c, w, b):
    mc, d = c.shape
    nl, _, n = w.shape
    tn = d
    return pl.pallas_call(
        _mod_kernel,
        grid=(nl, n // tn),
        in_specs=[pl.BlockSpec((mc, d), lambda l, j: (0, 0)),
                  pl.BlockSpec((None, d, tn), lambda l, j: (l, 0, j)),
                  pl.BlockSpec((None, 1, tn), lambda l, j: (l, 0, j))],
        out_specs=pl.BlockSpec((None, mc, tn), lambda l, j: (l, 0, j)),
        out_shape=jax.ShapeDtypeStruct((nl, mc, n), F32),
        compiler_params=_params("arbitrary", "arbitrary"),
        name="modulation",
    )(c, w, b.reshape(nl, 1, n))


def _ffn_kernel(h_ref, sh_ref, sc_ref, gt_ref, g_ref, wg_ref, wu_ref, wd_ref, *rest, n_f, n_o, final):
    if final:
        fg_ref, o_ref, u_scr, act_scr = rest
    else:
        o_ref, u_scr, act_scr = rest
    f = pl.program_id(1)
    tf = wg_ref.shape[1]
    to = wd_ref.shape[1]

    @pl.when(f == 0)
    def _():
        u_scr[...] = _rms_mod(h_ref[...], g_ref[...], sh_ref[...], sc_ref[...]).astype(BF16)

    @pl.when(f < n_f)
    def _():
        u = u_scr[...]
        a = _dot(u, wg_ref[...].astype(BF16))
        b = _dot(u, wu_ref[...].astype(BF16))
        act = (_silu(a) * b).astype(BF16)
        for c in range(n_f):
            @pl.when(f == c)
            def _(c=c):
                act_scr[:, c * tf:(c + 1) * tf] = act

    for c in range(n_o):
        @pl.when(f == n_f + c)
        def _(c=c):
            sl = slice(c * to, (c + 1) * to)
            down = _dot(act_scr[...], wd_ref[...].astype(BF16))
            o_ref[:, sl] = h_ref[:, sl] + 0.5 * gt_ref[:, sl] * down

    if final:
        @pl.when(f == n_f + n_o - 1)
        def _():
            o_ref[...] = _rmsnorm(o_ref[...], fg_ref[...])


def _ffn(rows, h, shift, scale, gate, g, wg, wu, wd, l, s, final_g=None):
    m, d = h.shape
    ff = wg.shape[-1]
    tf = 256 if ff % 256 == 0 else LANES
    to = min(512, d)
    n_f, n_o = ff // tf, d // to
    tm = rows.tm
    final = final_g is not None
    in_specs = [pl.BlockSpec((tm, d), lambda i, f: (i, 0)),
                rows.mod_spec(d), rows.mod_spec(d), rows.mod_spec(d),
                pl.BlockSpec((1, d), lambda i, f: (0, 0)),
                pl.BlockSpec((None, None, d, tf), lambda i, f: (l, s, 0, jnp.minimum(f, n_f - 1))),
                pl.BlockSpec((None, None, d, tf), lambda i, f: (l, s, 0, jnp.minimum(f, n_f - 1))),
                pl.BlockSpec((None, None, ff, to), lambda i, f: (l, s, 0, jnp.maximum(f - n_f, 0)))]
    args = [h, shift, scale, gate, g.reshape(1, d), wg, wu, wd]
    if final:
        in_specs.append(pl.BlockSpec((1, d), lambda i, f: (0, 0)))
        args.append(final_g.reshape(1, d))
    return pl.pallas_call(
        functools.partial(_ffn_kernel, n_f=n_f, n_o=n_o, final=final),
        grid=(m // tm, n_f + n_o),
        in_specs=in_specs,
        out_specs=pl.BlockSpec((tm, d), lambda i, f: (i, 0)),
        out_shape=jax.ShapeDtypeStruct((m, d), F32),
        scratch_shapes=[pltpu.VMEM((tm, d), BF16), pltpu.VMEM((tm, ff), BF16)],
        compiler_params=_params("arbitrary", "arbitrary"),
        name="ffn",
    )(*args)


def _rope_tables(pos):
    half = ROT_DIM // 2
    n = pos.shape[0]
    freq = ROPE_THETA ** (-jnp.arange(half, dtype=F32) / half)
    ang = pos.astype(F32)[:, None] * freq[None, :]
    cos, sin = jnp.cos(ang), jnp.sin(ang)
    rest = HEAD_DIM - ROT_DIM
    ct = jnp.concatenate([cos, cos, jnp.ones((n, rest), F32)], axis=1)
    s_lo = jnp.concatenate([-sin, jnp.zeros((n, half + rest), F32)], axis=1)
    s_hi = jnp.concatenate([jnp.zeros((n, half), F32), sin, jnp.zeros((n, rest), F32)], axis=1)
    tile = lambda t: jnp.tile(t, (1, HEADS_PER_TILE))
    return tile(ct), tile(s_hi), tile(s_lo)


def _rope_tile(x, ct, s_hi, s_lo):
    half = ROT_DIM // 2
    return x * ct + pltpu.roll(x, half, 1) * s_hi + pltpu.roll(x, LANES - half, 1) * s_lo


_MEAN_GROUP = 8


def _proj_kernel(h_ref, sh_ref, sc_ref, g_ref, w_ref, ct_ref, shi_ref, slo_ref, *rest, parts):
    u_scr = rest[-1]
    i, j = pl.program_id(0), pl.program_id(1)

    @pl.when(j == 0)
    def _():
        u_scr[...] = _rms_mod(h_ref[...], g_ref[...], sh_ref[...], sc_ref[...]).astype(BF16)

    y = _dot(u_scr[...], w_ref[...].astype(BF16))
    tm = y.shape[0]
    blocks = tm // MOBA_BLOCK
    n_out = 0
    for p, (rope, forms) in enumerate(parts):
        outs = rest[n_out:n_out + len(forms)]
        n_out += len(forms)

        @pl.when(j == p)
        def _(rope=rope, forms=forms, outs=outs):
            val = y
            if rope:
                ct, s_hi, s_lo = ct_ref[...], shi_ref[...], slo_ref[...]
                val = jnp.concatenate(
                    [_rope_tile(y[:, c * LANES:(c + 1) * LANES], ct, s_hi, s_lo)
                     for c in range(y.shape[1] // LANES)], axis=1)
            val_t = val.T if any(f.startswith("t_") for f in forms) else None
            for form, out in zip(forms, outs):
                if form == "f32":
                    out[...] = val
                elif form == "bf16":
                    out[...] = val.astype(BF16)
                elif form == "t_f32":
                    out[...] = val_t
                elif form == "t_bf16":
                    for c in range(blocks):
                        out[c] = val_t[:, c * MOBA_BLOCK:(c + 1) * MOBA_BLOCK].astype(BF16)
                elif form == "block_mean":
                    base = (i % (_MEAN_GROUP // blocks)) * blocks
                    for c in range(blocks):
                        out[pl.ds(base + c, 1), :] = jnp.mean(
                            val[c * MOBA_BLOCK:(c + 1) * MOBA_BLOCK], axis=0, keepdims=True)


def _proj(rows, h, shift, scale, g, w, tables, parts):
    m, d = h.shape
    n_p = w.shape[1] // len(parts)
    tm, tps, t = rows.tm, rows.tiles_per_seq, rows.rows_per_seq
    n_seq = m // t
    out_specs, out_shape = [], []
    for _, forms in parts:
        for form in forms:
            if form in ("f32", "bf16"):
                out_specs.append(pl.BlockSpec((tm, n_p), lambda i, j: (i, 0)))
                out_shape.append(jax.ShapeDtypeStruct((m, n_p), F32 if form == "f32" else BF16))
                continue
            assert not rows.per_row and tm % MOBA_BLOCK == 0
            if form == "t_f32":
                out_specs.append(pl.BlockSpec((None, n_p, tm), lambda i, j: (i // tps, 0, i % tps)))
                out_shape.append(jax.ShapeDtypeStruct((n_seq, n_p, t), F32))
            elif form == "t_bf16":
                out_specs.append(pl.BlockSpec((None, tm // MOBA_BLOCK, n_p, MOBA_BLOCK),
                                              lambda i, j: (i // tps, i % tps, 0, 0)))
                out_shape.append(jax.ShapeDtypeStruct((n_seq, t // MOBA_BLOCK, n_p, MOBA_BLOCK), BF16))
            else:
                steps = _MEAN_GROUP * MOBA_BLOCK // tm
                assert form == "block_mean" and tps % steps == 0
                out_specs.append(pl.BlockSpec((None, _MEAN_GROUP, n_p),
                                              lambda i, j: (i // tps, (i % tps) // steps, 0)))
                out_shape.append(jax.ShapeDtypeStruct((n_seq, t // MOBA_BLOCK, n_p), F32))
    pos_spec = rows.pos_spec(tables[0].shape[0])
    return pl.pallas_call(
        functools.partial(_proj_kernel, parts=parts),
        grid=(m // tm, len(parts)),
        in_specs=[pl.BlockSpec((tm, d), lambda i, j: (i, 0)),
                  rows.mod_spec(d), rows.mod_spec(d),
                  pl.BlockSpec((1, d), lambda i, j: (0, 0)),
                  pl.BlockSpec((d, n_p), lambda i, j: (0, j)),
                  pos_spec, pos_spec, pos_spec],
        out_specs=out_specs,
        out_shape=out_shape,
        scratch_shapes=[pltpu.VMEM((tm, d), BF16)],
        compiler_params=_params("arbitrary", "arbitrary"),
        name="proj",
    )(h, shift, scale, g.reshape(1, d), w, *tables)


def _oproj_kernel(h_ref, gt_ref, w_ref, *rest, n_groups):
    o_ref = rest[-1]
    if n_groups == 0:
        o = rest[0][...]
    else:
        os_, lses = rest[:n_groups], rest[n_groups:2 * n_groups]
        lse = [r[...] for r in lses]
        mx = functools.reduce(jnp.maximum, lse)
        e = [jnp.exp(x - mx) for x in lse]
        den = functools.reduce(lambda a, b: a + b, e)
        o = functools.reduce(lambda a, b: a + b, [(ei / den) * r[...] for ei, r in zip(e, os_)])
    o_ref[...] = h_ref[...] + gt_ref[...] * _dot(o.astype(BF16), w_ref[...].astype(BF16))


def _oproj(rows, h, gate, w, o_list, lse_list=()):
    m, d = h.shape
    k = w.shape[0]
    tm = rows.tm
    xs = list(o_list) + list(lse_list)
    return pl.pallas_call(
        functools.partial(_oproj_kernel, n_groups=len(lse_list)),
        grid=(m // tm, 1),
        in_specs=[pl.BlockSpec((tm, d), lambda i, j: (i, 0)),
                  rows.mod_spec(d),
                  pl.BlockSpec((k, d), lambda i, j: (0, 0))]
                 + [pl.BlockSpec((tm, k), lambda i, j: (i, 0))] * len(xs),
        out_specs=pl.BlockSpec((tm, d), lambda i, j: (i, 0)),
        out_shape=jax.ShapeDtypeStruct((m, d), F32),
        compiler_params=_params("arbitrary", "arbitrary"),
        name="oproj",
    )(h, gate, w, *xs)


def _split_bf16(x):
    hi = x.astype(BF16)
    lo = (x - hi.astype(F32)).astype(BF16)
    return hi, lo


def _top_blocks_t(gate, n_past):
    blk_id = lax.broadcasted_iota(jnp.int32, gate.shape, 0).astype(F32)
    n_past = n_past.astype(F32)
    g = jnp.where(blk_id < n_past, gate, NEG)
    sel = jnp.zeros(gate.shape, jnp.bool_)
    for _ in range(MOBA_TOPK):
        mx = jnp.max(g, axis=0, keepdims=True)
        idx = jnp.min(jnp.where(g == mx, blk_id, float(gate.shape[0])), axis=0, keepdims=True)
        pick = blk_id == idx
        sel = sel | pick
        g = jnp.where(pick, -jnp.inf, g)
    return sel & (blk_id < n_past)


def _moba_prompt_kernel(q_ref, k_ref, vt_ref, km_ref, o_ref,
                        qt_scr, sel_scr, m_scr, l_scr, acc_scr, s_own, s_even, s_odd):
    blk = MOBA_BLOCK
    n_h, width = qt_scr.shape[0], qt_scr.shape[1]
    nb = vt_ref.shape[0]
    i = pl.program_id(2)
    heads = range(n_h)
    hs = [slice(hh * HEAD_DIM, (hh + 1) * HEAD_DIM) for hh in heads]

    def scores_into(dst, j):
        jj = jnp.minimum(j, nb - 1)
        kb = k_ref[pl.ds(pl.multiple_of(jj * blk, blk), blk), :]
        for hh in heads:
            dst[hh] = _dot(kb, qt_scr[hh])

    def consume(src, j, own):
        jj = jnp.minimum(j, nb - 1)
        vtb = vt_ref[jj]
        ps, alphas, chosen = [], [], []
        for hh in heads:
            s = src[hh]
            if own:
                key_id = lax.broadcasted_iota(jnp.int32, (blk, blk), 0)
                qry_id = lax.broadcasted_iota(jnp.int32, (blk, blk), 1)
                s = jnp.where(key_id <= qry_id, s, NEG)
                ch = jnp.full((1, blk), True)
            else:
                ch = (sel_scr[hh, pl.ds(jj, 1), :] > 0.5) & (j < i)
            m_old = m_scr[hh]
            m_all = jnp.maximum(m_old, jnp.max(s, axis=0, keepdims=True))
            p = jnp.exp2(s - m_all)
            m_new = jnp.where(ch, m_all, m_old)
            alpha = jnp.exp2(m_old - m_new)
            l_scr[hh] = alpha * l_scr[hh] + jnp.where(ch, jnp.sum(p, axis=0, keepdims=True), 0.0)
            m_scr[hh] = m_new
            ps.append(p.astype(BF16))
            alphas.append(alpha)
            chosen.append(ch)
        pvs = [_dot(vtb[hs[hh], :], ps[hh]) for hh in heads]
        for hh in heads:
            acc_scr[hh] = alphas[hh] * acc_scr[hh] + jnp.where(chosen[hh], pvs[hh], 0.0)

    qt = q_ref[...].T
    feat = lax.broadcasted_iota(jnp.int32, (width, blk), 0)
    km_hi, km_lo = _split_bf16(km_ref[...])
    for hh in heads:
        qth = jnp.where((feat // HEAD_DIM) == hh, qt, 0.0)
        qt_scr[hh] = (qth * (HEAD_DIM ** -0.5 * LOG2_E)).astype(BF16)
        q_hi, q_lo = _split_bf16(qth)
        gate = _dot(km_hi, q_hi) + _dot(km_lo, q_hi) + _dot(km_hi, q_lo)
        sel_scr[hh] = _top_blocks_t(gate, i).astype(F32)
        m_scr[hh] = jnp.full((1, blk), NEG, F32)
        l_scr[hh] = jnp.zeros((1, blk), F32)
        acc_scr[hh] = jnp.zeros((HEAD_DIM, blk), F32)
    scores_into(s_own, i)
    scores_into(s_even, 0)
    consume(s_own, i, own=True)

    def two_blocks(c, carry):
        j = 2 * c
        scores_into(s_odd, j + 1)
        consume(s_even, j, own=False)
        scores_into(s_even, j + 2)
        consume(s_odd, j + 1, own=False)
        return carry

    lax.fori_loop(0, (i + 1) // 2, two_blocks, 0)
    ot = jnp.concatenate([acc_scr[hh] / l_scr[hh] for hh in heads], axis=0)
    o_ref[...] = ot.T


def _moba_prompt(q, k_bf, vt_bf, km):
    b, t, d = q.shape
    blk = MOBA_BLOCK
    nb = t // blk
    width = min(d, 2 * LANES)
    n_h = width // HEAD_DIM
    assert t % blk == 0 and nb % 8 == 0 and d % width == 0
    tile_spec = pl.BlockSpec((None, blk, width), lambda bi, hg, i: (bi, i, hg))
    return pl.pallas_call(
        _moba_prompt_kernel,
        grid=(b, d // width, nb),
        in_specs=[tile_spec,
                  pl.BlockSpec((None, t, width), lambda bi, hg, i: (bi, 0, hg)),
                  pl.BlockSpec((None, nb, width, blk), lambda bi, hg, i: (bi, 0, hg, 0)),
                  pl.BlockSpec((None, nb, width), lambda bi, hg, i: (bi, 0, hg))],
        out_specs=tile_spec,
        out_shape=jax.ShapeDtypeStruct((b, t, d), F32),
        scratch_shapes=[pltpu.VMEM((n_h, width, blk), BF16),
                        pltpu.VMEM((n_h, nb, blk), F32),
                        pltpu.VMEM((n_h, 1, blk), F32),
                        pltpu.VMEM((n_h, 1, blk), F32),
                        pltpu.VMEM((n_h, HEAD_DIM, blk), F32)]
                       + [pltpu.VMEM((n_h, blk, blk), F32)] * 3,
        compiler_params=_params("arbitrary", "arbitrary", "arbitrary"),
        name="moba_prompt",
    )(q, k_bf, vt_bf, km)


def _pad_rows(x, n):
    return jnp.concatenate([x, jnp.zeros((n - x.shape[0], x.shape[1]), x.dtype)], axis=0)


def _moba_sample_kernel(pt_ref, q_ref, kn_ref, vn_ref, *rest, n_pages, page, n_heads):
    k_pages, v_pages = rest[:n_pages], rest[n_pages:2 * n_pages]
    o_ref, s_scr = rest[2 * n_pages], rest[2 * n_pages + 1]
    t_dec, d = q_ref.shape
    r = n_heads * t_dec
    ppb = MOBA_BLOCK // page
    n_past = n_pages // ppb
    own = lax.broadcasted_iota(jnp.int32, (r, d), 0) // t_dec
    head_mask = (lax.broadcasted_iota(jnp.int32, (r, d), 1) // HEAD_DIM) == own
    qrows = jnp.where(head_mask, jnp.concatenate([q_ref[...]] * n_heads, axis=0), 0.0)
    qs = (qrows * HEAD_DIM ** -0.5).astype(BF16)

    gates = []
    for j in range(n_past):
        for p in range(j * ppb, (j + 1) * ppb):
            s_scr[:, p * page:(p + 1) * page] = _dot(qs, k_pages[p][...].astype(BF16))
        gates.append(jnp.sum(s_scr[:, j * MOBA_BLOCK:(j + 1) * MOBA_BLOCK], axis=1, keepdims=True))
    for j in range(n_past):
        rank = jnp.zeros((r, 1), jnp.int32)
        for j2 in range(n_past):
            if j2 != j:
                ahead = (gates[j2] > gates[j]) | ((gates[j2] == gates[j]) & (j2 < j))
                rank = rank + ahead.astype(jnp.int32)
        chosen = rank < MOBA_TOPK
        sl = slice(j * MOBA_BLOCK, (j + 1) * MOBA_BLOCK)
        s_scr[:, sl] = jnp.where(chosen, s_scr[:, sl], NEG)
    t_row = lax.broadcasted_iota(jnp.int32, (r, page), 0) % t_dec
    t_col = lax.broadcasted_iota(jnp.int32, (r, page), 1)
    s_own = _dot_nt(qs, _pad_rows(kn_ref[...], page).astype(BF16))
    s_scr[:, n_pages * page:] = jnp.where(t_col <= t_row, s_own, NEG)

    s = s_scr[...]
    m = jnp.max(s, axis=1, keepdims=True)
    p_all = jnp.exp(s - m)
    l = jnp.sum(p_all, axis=1, keepdims=True)
    acc = _dot(p_all[:, n_pages * page:].astype(BF16), _pad_rows(vn_ref[...], page).astype(BF16))
    for p in range(n_pages):
        acc = acc + _dot_nt(p_all[:, p * page:(p + 1) * page].astype(BF16), v_pages[p][...].astype(BF16))
    o_full = jnp.where(head_mask, acc / l, 0.0)
    out = o_full[0:t_dec]
    for h in range(1, n_heads):
        out = out + o_full[h * t_dec:(h + 1) * t_dec]
    o_ref[...] = out


def _moba_sample(q, k_new, v_new, cache_kt, cache_vt, layer, page_table):
    b, t_dec, d = q.shape
    n_pages = page_table.shape[1]
    page = cache_kt.shape[3]
    n_heads = d // HEAD_DIM
    assert MOBA_BLOCK % page == 0 and (n_pages * page) % MOBA_BLOCK == 0 and t_dec <= page
    tok_spec = pl.BlockSpec((None, t_dec, d), lambda i, pt: (i, 0, 0))
    page_specs = [pl.BlockSpec((None, None, d, page),
                               functools.partial(lambda i, pt, p: (layer, pt[i, p], 0, 0), p=p))
                  for p in range(n_pages)]
    grid_spec = pltpu.PrefetchScalarGridSpec(
        num_scalar_prefetch=1,
        grid=(b,),
        in_specs=[tok_spec, tok_spec, tok_spec] + page_specs + page_specs,
        out_specs=tok_spec,
        scratch_shapes=[pltpu.VMEM((n_heads * t_dec, (n_pages + 1) * page), F32)],
    )
    return pl.pallas_call(
        functools.partial(_moba_sample_kernel, n_pages=n_pages, page=page, n_heads=n_heads),
        grid_spec=grid_spec,
        out_shape=jax.ShapeDtypeStruct((b, t_dec, d), F32),
        compiler_params=_params("arbitrary"),
        name="moba_sample",
    )(page_table, q, k_new, v_new, *([cache_kt] * n_pages), *([cache_vt] * n_pages))


def _dil_prompt_kernel(q_ref, k_ref, v_ref, o_ref, lse_ref, *, dil, tq, width, win):
    length = k_ref.shape[0] // dil
    q0 = pl.program_id(2) * tq
    ks = jnp.clip(q0 - win, 0, length - width)
    lane = lax.broadcasted_iota(jnp.int32, (tq, LANES), 1)
    delta = (q0 + lax.broadcasted_iota(jnp.int32, (tq, width), 0)
             - ks - lax.broadcasted_iota(jnp.int32, (tq, width), 1))
    valid = (delta >= 0) & (delta <= win)

    def rows(start, n):
        return pl.ds(start, n) if dil == 1 else pl.ds(start, n, stride=dil)

    def one_class(r, carry):
        q = q_ref[rows(r, tq), :]
        kw = k_ref[rows(ks * dil + r, width), :].astype(BF16)
        vw = v_ref[rows(ks * dil + r, width), :].astype(BF16)
        outs, lses = [], []
        for hh in range(HEADS_PER_TILE):
            qh = (jnp.where((lane // HEAD_DIM) == hh, q, 0.0) * HEAD_DIM ** -0.5).astype(BF16)
            s = jnp.where(valid, _dot_nt(qh, kw), NEG)
            m = jnp.max(s, axis=1, keepdims=True)
            p = jnp.exp(s - m)
            l = jnp.sum(p, axis=1, keepdims=True)
            outs.append(_dot(p.astype(BF16), vw) / l)
            lses.append(jnp.broadcast_to(m + jnp.log(l), (tq, LANES)))
        o_ref[rows(r, tq), :] = jnp.where(lane < HEAD_DIM, outs[0], outs[1])
        lse_ref[rows(r, tq), :] = jnp.where(lane < HEAD_DIM, lses[0], lses[1])
        return carry

    lax.fori_loop(0, dil, one_class, 0)


def _dil_prompt(q, k, v, window, dil):
    b, t, dk = q.shape
    assert t % dil == 0 and HEADS_PER_TILE == 2
    length = t // dil
    win = window // dil
    tq = min(256, length)
    width = min(tq + win, length)
    assert length % tq == 0
    tile_spec = pl.BlockSpec((None, tq * dil, LANES), lambda bi, h, i: (bi, i, h))
    seq_spec = pl.BlockSpec((None, t, LANES), lambda bi, h, i: (bi, 0, h))
    return pl.pallas_call(
        functools.partial(_dil_prompt_kernel, dil=dil, tq=tq, width=width, win=win),
        grid=(b, dk // LANES, length // tq),
        in_specs=[tile_spec, seq_spec, seq_spec],
        out_specs=[tile_spec, tile_spec],
        out_shape=[jax.ShapeDtypeStruct((b, t, dk), F32)] * 2,
        compiler_params=_params("arbitrary", "arbitrary", "arbitrary"),
        name="dilated_prompt",
    )(q, k, v)


def _dil_sample_kernel(q0_ref, q1_ref, q2_ref, kn_ref, vn_ref, kc_ref, vc_ref, o_ref, s_scr, p_scr, *, n_heads):
    q_refs = (q0_ref, q1_ref, q2_ref)
    t_dec, dk = kn_ref.shape
    past = kc_ref.shape[1]
    pad = LANES
    rg = n_heads * t_dec
    head_mask = _head_lane_mask((rg, dk), t_dec)
    qrows = jnp.concatenate(
        [jnp.where(head_mask, jnp.concatenate([qr[...]] * n_heads, axis=0), 0.0) for qr in q_refs], axis=0)
    qs = (qrows * HEAD_DIM ** -0.5).astype(BF16)
    s_scr[:, :past] = _dot(qs, kc_ref[...].astype(BF16))
    s_scr[:, past:] = _dot_nt(qs, _pad_rows(kn_ref[...], pad).astype(BF16))

    t_row = lax.broadcasted_iota(jnp.int32, (rg, past + pad), 0) % t_dec
    col = lax.broadcasted_iota(jnp.int32, (rg, past + pad), 1)
    delta = past + t_row - col
    in_range = col < past + t_dec
    l_g, lse_g = [], []
    for g, (window, dil) in enumerate(B_CONFIGS):
        rs = slice(g * rg, (g + 1) * rg)
        assert dil & (dil - 1) == 0
        valid = in_range & (delta >= 0) & (delta <= window) & ((delta & (dil - 1)) == 0)
        s = jnp.where(valid, s_scr[rs, :], NEG)
        m = jnp.max(s, axis=1, keepdims=True)
        p = jnp.exp(s - m)
        l = jnp.sum(p, axis=1, keepdims=True)
        p_scr[rs, :] = p.astype(BF16)
        l_g.append(l)
        lse_g.append(m + jnp.log(l))
    acc = (_dot_nt(p_scr[:, :past], vc_ref[...].astype(BF16))
           + _dot(p_scr[:, past:], _pad_rows(vn_ref[...], pad).astype(BF16)))
    mx = functools.reduce(jnp.maximum, lse_g)
    e = [jnp.exp(x - mx) for x in lse_g]
    den = functools.reduce(lambda a, b: a + b, e)
    mixed = jnp.zeros((rg, dk), F32)
    for g in range(len(B_CONFIGS)):
        mixed = mixed + (e[g] / den) * (acc[g * rg:(g + 1) * rg] / l_g[g])
    mixed = jnp.where(head_mask, mixed, 0.0)
    out = mixed[0:t_dec]
    for h in range(1, n_heads):
        out = out + mixed[h * t_dec:(h + 1) * t_dec]
    o_ref[...] = out


def _dil_sample(q_groups, k_new, v_new, cache_kt, cache_vt):
    b, t_dec, dk = k_new.shape
    past = cache_kt.shape[2]
    n_heads = dk // HEAD_DIM
    n_rows = len(B_CONFIGS) * n_heads * t_dec
    tok_spec = pl.BlockSpec((None, t_dec, dk), lambda i: (i, 0, 0))
    cache_spec = pl.BlockSpec((None, dk, past), lambda i: (i, 0, 0))
    return pl.pallas_call(
        functools.partial(_dil_sample_kernel, n_heads=n_heads),
        grid=(b,),
        in_specs=[tok_spec] * 5 + [cache_spec] * 2,
        out_specs=tok_spec,
        out_shape=jax.ShapeDtypeStruct((b, t_dec, dk), F32),
        scratch_shapes=[pltpu.VMEM((n_rows, past + LANES), F32),
                        pltpu.VMEM((n_rows, past + LANES), BF16)],
        compiler_params=_params("arbitrary"),
        name="dilated_sample",
    )(*q_groups, k_new, v_new, cache_kt, cache_vt)


def _trunk(x, mods, kv_mod, pos, past, weights):
    (norm_g, w_ffn_gate, w_ffn_up, w_ffn_down, w_qkv_a, w_o_a,
     kv_norm_g, w_kv_b, w_q_b, w_o_b, final_norm_g) = weights
    n_seq, t, d = x.shape
    m = n_seq * t
    depth = norm_g.shape[0]
    n_a = w_qkv_a.shape[0]
    dk = w_kv_b.shape[1] // 2
    is_prompt = past is None
    short = t < 256
    rows = _Rows(m, min(1024 if is_prompt else 512, m if short else t), t)
    rows_p = _Rows(m, min(512, m if short else t), t)
    tables = _rope_tables(pos)
    if rows.per_row:
        tables = tuple(jnp.tile(tb, (n_seq, 1)) for tb in tables)

    def heads_last(x_t):
        return jnp.transpose(x_t.reshape(n_seq, -1, HEAD_DIM, x_t.shape[-1]), (0, 3, 1, 2))

    h = x.reshape(m, d)
    a_k, a_v = [], []
    b_k = b_v = b_k_out = b_v_out = None
    for l in range(depth):
        if l == n_a:
            sh, sc = (rows.prep(kv_mod[:, i * d:(i + 1) * d]) for i in range(2))
            if is_prompt:
                b_k, b_kt, b_v, b_vt = _proj(rows_p, h, sh, sc, kv_norm_g, w_kv_b, tables,
                                             ((True, ("f32", "t_f32")), (False, ("f32", "t_f32"))))
                b_k_out, b_v_out = heads_last(b_kt), heads_last(b_vt)
            else:
                b_k, b_v = _proj(rows_p, h, sh, sc, kv_norm_g, w_kv_b, tables,
                                 ((True, ("f32",)), (False, ("f32",))))
                b_k_out, b_v_out = (r.reshape(n_seq, t, -1, HEAD_DIM) for r in (b_k, b_v))
        md = [[rows.prep(mods[l][:, (3 * s + c) * d:(3 * s + c + 1) * d]) for c in range(3)] for s in range(3)]
        h = _ffn(rows, h, md[0][0], md[0][1], md[0][2], norm_g[l, 0], w_ffn_gate, w_ffn_up, w_ffn_down, l, 0)
        if l < n_a:
            shp = (n_seq, t, d)
            if is_prompt:
                q, k_t, k_bf, km, v_t, vt_bf = _proj(
                    rows_p, h, md[1][0], md[1][1], norm_g[l, 1], w_qkv_a[l], tables,
                    ((True, ("f32",)), (True, ("t_f32", "bf16", "block_mean")), (False, ("t_f32", "t_bf16"))))
                o = _moba_prompt(q.reshape(shp), k_bf.reshape(shp), vt_bf, km)
                a_k.append(heads_last(k_t))
                a_v.append(heads_last(v_t))
            else:
                q, k, v = _proj(rows_p, h, md[1][0], md[1][1], norm_g[l, 1], w_qkv_a[l], tables,
                                ((True, ("f32",)), (True, ("f32",)), (False, ("f32",))))
                cache_kt, cache_vt = (
                    jnp.transpose(c, (0, 1, 3, 4, 2)).reshape(c.shape[0], c.shape[1], d, c.shape[2])
                    for c in past[:2])
                o = _moba_sample(q.reshape(shp), k.reshape(shp), v.reshape(shp), cache_kt, cache_vt, l, past[2])
                a_k.append(k.reshape(n_seq, t, -1, HEAD_DIM))
                a_v.append(v.reshape(n_seq, t, -1, HEAD_DIM))
            h = _oproj(rows, h, md[1][2], w_o_a[l], [o.reshape(m, d)])
        else:
            lb = l - n_a
            qg = _proj(rows_p, h, md[1][0], md[1][1], norm_g[l, 1], w_q_b[lb], tables,
                       ((True, ("f32",)),) * len(B_CONFIGS))
            shp = (n_seq, t, dk)
            if is_prompt:
                res = [_dil_prompt(qi.reshape(shp), b_k.reshape(shp), b_v.reshape(shp), w, dl)
                       for qi, (w, dl) in zip(qg, B_CONFIGS)]
                h = _oproj(rows, h, md[1][2], w_o_b[lb], [r[0].reshape(m, dk) for r in res],
                           [r[1].reshape(m, dk) for r in res])
            else:
                cb_kt, cb_vt = (jnp.transpose(c, (0, 2, 3, 1)).reshape(n_seq, dk, c.shape[1]) for c in past[3:5])
                o = _dil_sample([qi.reshape(shp) for qi in qg], b_k.reshape(shp), b_v.reshape(shp), cb_kt, cb_vt)
                h = _oproj(rows, h, md[1][2], w_o_b[lb], [o.reshape(m, dk)])
        h = _ffn(rows, h, md[2][0], md[2][1], md[2][2], norm_g[l, 2], w_ffn_gate, w_ffn_up, w_ffn_down, l, 1,
                 final_g=final_norm_g if l == depth - 1 else None)
    return h.reshape(n_seq, t, d), jnp.stack(a_k), jnp.stack(a_v), b_k_out, b_v_out


def kernel(x_prompt, x_sample, cache_a_k, cache_a_v, cache_b_k, cache_b_v, page_table, c_prompt, c_sample, norm_g, w_mod, b_mod, w_ffn_gate, w_ffn_up, w_ffn_down, w_qkv_a, w_o_a, kv_norm_g, w_kv_mod, b_kv_mod, w_kv_b, w_q_b, w_o_b, final_norm_g):
    bp, t, d = x_prompt.shape
    bs, t_dec, _ = x_sample.shape
    n_a = w_qkv_a.shape[0]
    n_heads_a = d // HEAD_DIM
    n_heads_b = w_kv_b.shape[1] // (2 * HEAD_DIM)
    past_len = page_table.shape[1] * cache_a_k.shape[2]
    assert past_len % MOBA_BLOCK == 0 and t_dec <= MOBA_BLOCK

    n_c = bp + bs
    c_all = _pad_rows(jnp.concatenate([c_prompt, c_sample], axis=0), -(-n_c // 8) * 8)
    mods = _modulation(c_all, w_mod, b_mod)
    kv_mod = _modulation(c_all, w_kv_mod[None], b_kv_mod[None])[0]

    weights = (norm_g, w_ffn_gate, w_ffn_up, w_ffn_down, w_qkv_a, w_o_a,
               kv_norm_g, w_kv_b, w_q_b, w_o_b, final_norm_g)
    y_p, ak_p, av_p, bk_p, bv_p = _trunk(
        x_prompt, mods[:, :bp], kv_mod[:bp], jnp.arange(t, dtype=jnp.int32), None, weights)
    y_s, ak_s, av_s, bk_s, bv_s = _trunk(
        x_sample, mods[:, bp:n_c], kv_mod[bp:n_c], past_len + jnp.arange(t_dec, dtype=jnp.int32),
        (cache_a_k, cache_a_v, page_table, cache_b_k, cache_b_v), weights)

    keep = min(max(w for w, _ in B_CONFIGS), t)
    return (y_p, y_s, ak_p, av_p, ak_s, av_s, bk_p[:, -keep:], bv_p[:, -keep:], bk_s, bv_s)
```

```python
import functools

import jax
import jax.numpy as jnp
from jax import lax
from jax.experimental import pallas as pl
from jax.experimental.pallas import tpu as pltpu

HEAD_DIM = 64
ROT_DIM = HEAD_DIM // 4
ROPE_THETA = 500000.0
MOBA_BLOCK = 256
MOBA_TOPK = 3
B_CONFIGS = ((128, 1), (512, 4), (2048, 16))
EPS = 1e-6
NEG = -1e30
LOG2_E = 1.4426950408889634

LANES = 128
HEADS_PER_TILE = LANES // HEAD_DIM
VMEM_LIMIT = 56 * 1024 * 1024

F32 = jnp.float32
BF16 = jnp.bfloat16


def _params(*sem):
    return pltpu.CompilerParams(dimension_semantics=sem, vmem_limit_bytes=VMEM_LIMIT)


def _dot(a, b):
    return jnp.dot(a, b, preferred_element_type=F32)


def _dot_nt(a, b):
    return lax.dot_general(a, b, (((1,), (1,)), ((), ())), preferred_element_type=F32)


def _silu(x):
    return x * jax.nn.sigmoid(x)


def _rmsnorm(x, g):
    return x * lax.rsqrt(jnp.mean(x * x, axis=-1, keepdims=True) + EPS) * g


def _rms_mod(x, g, shift, scale):
    return _rmsnorm(x, g) * (1.0 + scale) + shift


def _head_lane_mask(shape, rows_per_head):
    row = lax.broadcasted_iota(jnp.int32, shape, 0)
    lane = lax.broadcasted_iota(jnp.int32, shape, 1)
    return (lane // HEAD_DIM) == (row // rows_per_head)


class _Rows:
    def __init__(self, m, tm, rows_per_seq):
        self.m, self.tm = m, tm
        self.per_row = rows_per_seq < tm
        self.rows_per_seq = rows_per_seq
        self.tiles_per_seq = max(rows_per_seq // tm, 1)

    def prep(self, vec):
        if self.per_row:
            return jnp.repeat(vec, self.rows_per_seq, axis=0)
        return vec[:, None, :]

    def mod_spec(self, d):
        if self.per_row:
            return pl.BlockSpec((self.tm, d), lambda i, j: (i, 0))
        tps = self.tiles_per_seq
        return pl.BlockSpec((None, 1, d), lambda i, j: (i // tps, 0, 0))

    def pos_spec(self, table_rows):
        n_pos_tiles = table_rows // self.tm
        return pl.BlockSpec((self.tm, LANES), lambda i, j: (i % n_pos_tiles, 0))


def _mod_kernel(c_ref, w_ref, b_ref, o_ref):
    cs = _silu(c_ref[...]).astype(BF16)
    o_ref[...] = _dot(cs, w_ref[...].astype(BF16)) + b_ref[...]


def _modulation(c, w, b):
    mc, d = c.shape
    nl, _, n = w.shape
    tn = d
    return pl.pallas_call(
        _mod_kernel,
        grid=(nl, n // tn),
        in_specs=[pl.BlockSpec((mc, d), lambda l, j: (0, 0)),
                  pl.BlockSpec((None, d, tn), lambda l, j: (l, 0, j)),
                  pl.BlockSpec((None, 1, tn), lambda l, j: (l, 0, j))],
        out_specs=pl.BlockSpec((None, mc, tn), lambda l, j: (l, 0, j)),
        out_shape=jax.ShapeDtypeStruct((nl, mc, n), F32),
        compiler_params=_params("arbitrary", "arbitrary"),
        name="modulation",
    )(c, w, b.reshape(nl, 1, n))


def _ffn_kernel(h_ref, sh_ref, sc_ref, gt_ref, g_ref, wg_ref, wu_ref, wd_ref, *rest, n_f, n_o, final):
    if final:
        fg_ref, o_ref, u_scr, act_scr = rest
    else:
        o_ref, u_scr, act_scr = rest
    f = pl.program_id(1)
    tf = wg_ref.shape[1]
    to = wd_ref.shape[1]

    @pl.when(f == 0)
    def _():
        u_scr[...] = _rms_mod(h_ref[...], g_ref[...], sh_ref[...], sc_ref[...]).astype(BF16)

    @pl.when(f < n_f)
    def _():
        u = u_scr[...]
        a = _dot(u, wg_ref[...])
        b = _dot(u, wu_ref[...])
        act = (_silu(a) * b).astype(BF16)
        for c in range(n_f):
            @pl.when(f == c)
            def _(c=c):
                act_scr[:, c * tf:(c + 1) * tf] = act

    for c in range(n_o):
        @pl.when(f == n_f + c)
        def _(c=c):
            sl = slice(c * to, (c + 1) * to)
            down = _dot(act_scr[...], wd_ref[...])
            o_ref[:, sl] = h_ref[:, sl] + 0.5 * gt_ref[:, sl] * down

    if final:
        @pl.when(f == n_f + n_o - 1)
        def _():
            o_ref[...] = _rmsnorm(o_ref[...], fg_ref[...])


def _ffn_weights(wg, wu, wd):
    d, ff = wg.shape[-2:]
    tf = 256 if ff % 256 == 0 else LANES
    to = min(512, d)
    lead = wg.shape[:-2]
    n_lead = len(lead)
    perm = tuple(range(n_lead)) + (n_lead + 1, n_lead, n_lead + 2)
    chunk = lambda w, rows_, width: jnp.transpose(
        w.astype(BF16).reshape(lead + (rows_, w.shape[-1] // width, width)), perm)
    return chunk(wg, d, tf), chunk(wu, d, tf), chunk(wd, ff, to)


def _ffn(rows, h, shift, scale, gate, g, wg, wu, wd, l, s, final_g=None):
    m, d = h.shape
    n_f, _, tf = wg.shape[-3:]
    n_o, ff, to = wd.shape[-3:]
    tm = rows.tm
    final = final_g is not None
    in_specs = [pl.BlockSpec((tm, d), lambda i, f: (i, 0)),
                rows.mod_spec(d), rows.mod_spec(d), rows.mod_spec(d),
                pl.BlockSpec((1, d), lambda i, f: (0, 0)),
                pl.BlockSpec((None, None, None, d, tf), lambda i, f: (l, s, jnp.minimum(f, n_f - 1), 0, 0)),
                pl.BlockSpec((None, None, None, d, tf), lambda i, f: (l, s, jnp.minimum(f, n_f - 1), 0, 0)),
                pl.BlockSpec((None, None, None, ff, to), lambda i, f: (l, s, jnp.maximum(f - n_f, 0), 0, 0))]
    args = [h, shift, scale, gate, g.reshape(1, d), wg, wu, wd]
    if final:
        in_specs.append(pl.BlockSpec((1, d), lambda i, f: (0, 0)))
        args.append(final_g.reshape(1, d))
    return pl.pallas_call(
        functools.partial(_ffn_kernel, n_f=n_f, n_o=n_o, final=final),
        grid=(m // tm, n_f + n_o),
        in_specs=in_specs,
        out_specs=pl.BlockSpec((tm, d), lambda i, f: (i, 0)),
        out_shape=jax.ShapeDtypeStruct((m, d), F32),
        scratch_shapes=[pltpu.VMEM((tm, d), BF16), pltpu.VMEM((tm, ff), BF16)],
        compiler_params=_params("arbitrary", "arbitrary"),
        name="ffn",
    )(*args)


def _rope_tables(pos):
    half = ROT_DIM // 2
    n = pos.shape[0]
    freq = ROPE_THETA ** (-jnp.arange(half, dtype=F32) / half)
    ang = pos.astype(F32)[:, None] * freq[None, :]
    cos, sin = jnp.cos(ang), jnp.sin(ang)
    rest = HEAD_DIM - ROT_DIM
    ct = jnp.concatenate([cos, cos, jnp.ones((n, rest), F32)], axis=1)
    s_lo = jnp.concatenate([-sin, jnp.zeros((n, half + rest), F32)], axis=1)
    s_hi = jnp.concatenate([jnp.zeros((n, half), F32), sin, jnp.zeros((n, rest), F32)], axis=1)
    tile = lambda t: jnp.tile(t, (1, HEADS_PER_TILE))
    return (tile(ct), tile(s_hi), tile(s_lo)), (cos.T, sin.T)


def _rope_tile(x, ct, s_hi, s_lo):
    half = ROT_DIM // 2
    return x * ct + pltpu.roll(x, half, 1) * s_hi + pltpu.roll(x, LANES - half, 1) * s_lo


def _rope_t(x_t, cos_t, sin_t):
    half = ROT_DIM // 2
    pieces = []
    for base in range(0, x_t.shape[0], HEAD_DIM):
        x1, x2 = x_t[base:base + half], x_t[base + half:base + ROT_DIM]
        pieces += [x1 * cos_t - x2 * sin_t, x2 * cos_t + x1 * sin_t, x_t[base + ROT_DIM:base + HEAD_DIM]]
    return jnp.concatenate(pieces, axis=0)


_MEAN_GROUP = 8


def _proj_kernel(h_ref, sh_ref, sc_ref, g_ref, w_ref, ct_ref, shi_ref, slo_ref, cos_t_ref, sin_t_ref,
                 *outs, parts):
    i = pl.program_id(0)
    u = _rms_mod(h_ref[...], g_ref[...], sh_ref[...], sc_ref[...]).astype(BF16)
    tm = u.shape[0]
    n_p = w_ref.shape[1] // len(parts)
    blocks = tm // MOBA_BLOCK
    n_out = 0
    for p, (rope, forms) in enumerate(parts):
        part_outs = outs[n_out:n_out + len(forms)]
        n_out += len(forms)
        y = _dot(u, w_ref[:, p * n_p:(p + 1) * n_p])
        val = val_t = None
        if any(f.startswith("t_") for f in forms):
            val_t = y.T
            if rope:
                val_t = _rope_t(val_t, cos_t_ref[...], sin_t_ref[...])
            if any(not f.startswith("t_") for f in forms):
                val = val_t.T
        elif rope:
            ct, s_hi, s_lo = ct_ref[...], shi_ref[...], slo_ref[...]
            val = jnp.concatenate([_rope_tile(y[:, c * LANES:(c + 1) * LANES], ct, s_hi, s_lo)
                                   for c in range(n_p // LANES)], axis=1)
        else:
            val = y
        for form, out in zip(forms, part_outs):
            if form == "f32":
                out[...] = val
            elif form == "bf16":
                out[...] = val.astype(BF16)
            elif form == "t_f32":
                out[...] = val_t
            elif form == "t_bf16":
                for c in range(blocks):
                    out[c] = val_t[:, c * MOBA_BLOCK:(c + 1) * MOBA_BLOCK].astype(BF16)
            elif form == "block_mean":
                base = (i % (_MEAN_GROUP // blocks)) * blocks
                for c in range(blocks):
                    out[pl.ds(base + c, 1), :] = jnp.mean(
                        val[c * MOBA_BLOCK:(c + 1) * MOBA_BLOCK], axis=0, keepdims=True)


def _proj(rows, h, shift, scale, g, w, tables, parts):
    m, d = h.shape
    n_p = w.shape[1] // len(parts)
    tm, tps, t = rows.tm, rows.tiles_per_seq, rows.rows_per_seq
    n_seq = m // t
    out_specs, out_shape = [], []
    for _, forms in parts:
        for form in forms:
            if form in ("f32", "bf16"):
                out_specs.append(pl.BlockSpec((tm, n_p), lambda i, j: (i, 0)))
                out_shape.append(jax.ShapeDtypeStruct((m, n_p), F32 if form == "f32" else BF16))
                continue
            assert not rows.per_row and tm % MOBA_BLOCK == 0
            if form == "t_f32":
                out_specs.append(pl.BlockSpec((None, n_p, tm), lambda i, j: (i // tps, 0, i % tps)))
                out_shape.append(jax.ShapeDtypeStruct((n_seq, n_p, t), F32))
            elif form == "t_bf16":
                out_specs.append(pl.BlockSpec((None, tm // MOBA_BLOCK, n_p, MOBA_BLOCK),
                                              lambda i, j: (i // tps, i % tps, 0, 0)))
                out_shape.append(jax.ShapeDtypeStruct((n_seq, t // MOBA_BLOCK, n_p, MOBA_BLOCK), BF16))
            else:
                steps = _MEAN_GROUP * MOBA_BLOCK // tm
                assert form == "block_mean" and tps % steps == 0
                out_specs.append(pl.BlockSpec((None, _MEAN_GROUP, n_p),
                                              lambda i, j: (i // tps, (i % tps) // steps, 0)))
                out_shape.append(jax.ShapeDtypeStruct((n_seq, t // MOBA_BLOCK, n_p), F32))
    lane_tables, t_tables = tables
    pos_spec = rows.pos_spec(lane_tables[0].shape[0])
    n_pos_tiles = lane_tables[0].shape[0] // tm
    pos_t_spec = pl.BlockSpec((ROT_DIM // 2, tm), lambda i, j: (0, i % n_pos_tiles))
    return pl.pallas_call(
        functools.partial(_proj_kernel, parts=parts),
        grid=(m // tm, 1),
        in_specs=[pl.BlockSpec((tm, d), lambda i, j: (i, 0)),
                  rows.mod_spec(d), rows.mod_spec(d),
                  pl.BlockSpec((1, d), lambda i, j: (0, 0)),
                  pl.BlockSpec(w.shape, lambda i, j: (0, 0)),
                  pos_spec, pos_spec, pos_spec, pos_t_spec, pos_t_spec],
        out_specs=out_specs,
        out_shape=out_shape,
        compiler_params=_params("arbitrary", "arbitrary"),
        name="proj",
    )(h, shift, scale, g.reshape(1, d), w, *lane_tables, *t_tables)


def _oproj_kernel(h_ref, gt_ref, w_ref, *rest, n_groups):
    o_ref = rest[-1]
    if n_groups == 0:
        o = rest[0][...]
    else:
        os_, lses = rest[:n_groups], rest[n_groups:2 * n_groups]
        lse = [r[...] for r in lses]
        mx = functools.reduce(jnp.maximum, lse)
        e = [jnp.exp(x - mx) for x in lse]
        den = functools.reduce(lambda a, b: a + b, e)
        o = functools.reduce(lambda a, b: a + b, [(ei / den) * r[...] for ei, r in zip(e, os_)])
    o_ref[...] = h_ref[...] + gt_ref[...] * _dot(o.astype(BF16), w_ref[...])


def _oproj(rows, h, gate, w, o_list, lse_list=()):
    m, d = h.shape
    k = w.shape[0]
    tm = rows.tm
    xs = list(o_list) + list(lse_list)
    return pl.pallas_call(
        functools.partial(_oproj_kernel, n_groups=len(lse_list)),
        grid=(m // tm, 1),
        in_specs=[pl.BlockSpec((tm, d), lambda i, j: (i, 0)),
                  rows.mod_spec(d),
                  pl.BlockSpec((k, d), lambda i, j: (0, 0))]
                 + [pl.BlockSpec((tm, k), lambda i, j: (i, 0))] * len(xs),
        out_specs=pl.BlockSpec((tm, d), lambda i, j: (i, 0)),
        out_shape=jax.ShapeDtypeStruct((m, d), F32),
        compiler_params=_params("arbitrary", "arbitrary"),
        name="oproj",
    )(h, gate, w, *xs)


def _split_bf16(x):
    hi = x.astype(BF16)
    lo = (x - hi.astype(F32)).astype(BF16)
    return hi, lo


def _top_blocks_t(gate, n_past):
    blk_id = lax.broadcasted_iota(jnp.int32, gate.shape, 0).astype(F32)
    n_past = n_past.astype(F32)
    g = jnp.where(blk_id < n_past, gate, NEG)
    sel = jnp.zeros(gate.shape, jnp.bool_)
    for _ in range(MOBA_TOPK):
        mx = jnp.max(g, axis=0, keepdims=True)
        idx = jnp.min(jnp.where(g == mx, blk_id, float(gate.shape[0])), axis=0, keepdims=True)
        pick = blk_id == idx
        sel = sel | pick
        g = jnp.where(pick, -jnp.inf, g)
    return sel & (blk_id < n_past)


def _moba_prompt_kernel(q_ref, k_ref, vt_ref, km_ref, o_ref,
                        qt_scr, sel_scr, m_scr, l_scr, acc_scr, s_own, s_even, s_odd):
    blk = MOBA_BLOCK
    n_h, width = qt_scr.shape[0], qt_scr.shape[1]
    nb = vt_ref.shape[0]
    i = pl.program_id(2)
    heads = range(n_h)
    hs = [slice(hh * HEAD_DIM, (hh + 1) * HEAD_DIM) for hh in heads]

    def scores_into(dst, j):
        jj = jnp.minimum(j, nb - 1)
        kb = k_ref[pl.ds(pl.multiple_of(jj * blk, blk), blk), :]
        for hh in heads:
            dst[hh] = _dot(kb, qt_scr[hh])

    def consume(src, j, own):
        jj = jnp.minimum(j, nb - 1)
        vtb = vt_ref[jj]
        ps, alphas, chosen = [], [], []
        for hh in heads:
            s = src[hh]
            if own:
                key_id = lax.broadcasted_iota(jnp.int32, (blk, blk), 0)
                qry_id = lax.broadcasted_iota(jnp.int32, (blk, blk), 1)
                s = jnp.where(key_id <= qry_id, s, NEG)
                ch = jnp.full((1, blk), True)
            else:
                ch = (sel_scr[hh, pl.ds(jj, 1), :] > 0.5) & (j < i)
            m_old = m_scr[hh]
            m_all = jnp.maximum(m_old, jnp.max(s, axis=0, keepdims=True))
            p = jnp.exp2(s - m_all)
            m_new = jnp.where(ch, m_all, m_old)
            alpha = jnp.exp2(m_old - m_new)
            l_scr[hh] = alpha * l_scr[hh] + jnp.where(ch, jnp.sum(p, axis=0, keepdims=True), 0.0)
            m_scr[hh] = m_new
            ps.append(p.astype(BF16))
            alphas.append(alpha)
            chosen.append(ch)
        pvs = [_dot(vtb[hs[hh], :], ps[hh]) for hh in heads]
        for hh in heads:
            acc_scr[hh] = alphas[hh] * acc_scr[hh] + jnp.where(chosen[hh], pvs[hh], 0.0)

    qt = q_ref[...]
    feat = lax.broadcasted_iota(jnp.int32, (width, blk), 0)
    km_hi, km_lo = _split_bf16(km_ref[...])
    for hh in heads:
        qth = jnp.where((feat // HEAD_DIM) == hh, qt, 0.0)
        qt_scr[hh] = (qth * (HEAD_DIM ** -0.5 * LOG2_E)).astype(BF16)
        q_hi, q_lo = _split_bf16(qth)
        gate = _dot(km_hi, q_hi) + _dot(km_lo, q_hi) + _dot(km_hi, q_lo)
        sel_scr[hh] = _top_blocks_t(gate, i).astype(F32)
        m_scr[hh] = jnp.full((1, blk), NEG, F32)
        l_scr[hh] = jnp.zeros((1, blk), F32)
        acc_scr[hh] = jnp.zeros((HEAD_DIM, blk), F32)
    scores_into(s_own, i)
    scores_into(s_even, 0)
    consume(s_own, i, own=True)

    def two_blocks(c, carry):
        j = 2 * c
        scores_into(s_odd, j + 1)
        consume(s_even, j, own=False)
        scores_into(s_even, j + 2)
        consume(s_odd, j + 1, own=False)
        return carry

    lax.fori_loop(0, (i + 1) // 2, two_blocks, 0)
    ot = jnp.concatenate([acc_scr[hh] / l_scr[hh] for hh in heads], axis=0)
    o_ref[...] = ot.T


def _moba_prompt(q_t, k_bf, vt_bf, km):
    b, d, t = q_t.shape
    blk = MOBA_BLOCK
    nb = t // blk
    width = min(d, 2 * LANES)
    n_h = width // HEAD_DIM
    assert t % blk == 0 and nb % 8 == 0 and d % width == 0
    tile_spec = pl.BlockSpec((None, blk, width), lambda bi, hg, i: (bi, i, hg))
    return pl.pallas_call(
        _moba_prompt_kernel,
        grid=(b, d // width, nb),
        in_specs=[pl.BlockSpec((None, width, blk), lambda bi, hg, i: (bi, hg, i)),
                  pl.BlockSpec((None, t, width), lambda bi, hg, i: (bi, 0, hg)),
                  pl.BlockSpec((None, nb, width, blk), lambda bi, hg, i: (bi, 0, hg, 0)),
                  pl.BlockSpec((None, nb, width), lambda bi, hg, i: (bi, 0, hg))],
        out_specs=tile_spec,
        out_shape=jax.ShapeDtypeStruct((b, t, d), F32),
        scratch_shapes=[pltpu.VMEM((n_h, width, blk), BF16),
                        pltpu.VMEM((n_h, nb, blk), F32),
                        pltpu.VMEM((n_h, 1, blk), F32),
                        pltpu.VMEM((n_h, 1, blk), F32),
                        pltpu.VMEM((n_h, HEAD_DIM, blk), F32)]
                       + [pltpu.VMEM((n_h, blk, blk), F32)] * 3,
        compiler_params=_params("arbitrary", "arbitrary", "arbitrary"),
        name="moba_prompt",
    )(q_t, k_bf, vt_bf, km)


def _pad_rows(x, n):
    return jnp.concatenate([x, jnp.zeros((n - x.shape[0], x.shape[1]), x.dtype)], axis=0)


def _moba_sample_kernel(pt_ref, q_ref, kn_ref, vn_ref, *rest, n_pages, page, n_heads):
    k_pages, v_pages = rest[:n_pages], rest[n_pages:2 * n_pages]
    o_ref, s_scr = rest[2 * n_pages], rest[2 * n_pages + 1]
    t_dec, d = q_ref.shape
    r = n_heads * t_dec
    ppb = MOBA_BLOCK // page
    n_past = n_pages // ppb
    own = lax.broadcasted_iota(jnp.int32, (r, d), 0) // t_dec
    head_mask = (lax.broadcasted_iota(jnp.int32, (r, d), 1) // HEAD_DIM) == own
    qrows = jnp.where(head_mask, jnp.concatenate([q_ref[...]] * n_heads, axis=0), 0.0)
    qs = (qrows * HEAD_DIM ** -0.5).astype(BF16)

    gates = []
    for j in range(n_past):
        for p in range(j * ppb, (j + 1) * ppb):
            s_scr[:, p * page:(p + 1) * page] = _dot(qs, k_pages[p][...].astype(BF16))
        gates.append(jnp.sum(s_scr[:, j * MOBA_BLOCK:(j + 1) * MOBA_BLOCK], axis=1, keepdims=True))
    for j in range(n_past):
        rank = jnp.zeros((r, 1), jnp.int32)
        for j2 in range(n_past):
            if j2 != j:
                ahead = (gates[j2] > gates[j]) | ((gates[j2] == gates[j]) & (j2 < j))
                rank = rank + ahead.astype(jnp.int32)
        chosen = rank < MOBA_TOPK
        sl = slice(j * MOBA_BLOCK, (j + 1) * MOBA_BLOCK)
        s_scr[:, sl] = jnp.where(chosen, s_scr[:, sl], NEG)
    t_row = lax.broadcasted_iota(jnp.int32, (r, page), 0) % t_dec
    t_col = lax.broadcasted_iota(jnp.int32, (r, page), 1)
    s_own = _dot_nt(qs, _pad_rows(kn_ref[...], page).astype(BF16))
    s_scr[:, n_pages * page:] = jnp.where(t_col <= t_row, s_own, NEG)

    s = s_scr[...]
    m = jnp.max(s, axis=1, keepdims=True)
    p_all = jnp.exp(s - m)
    l = jnp.sum(p_all, axis=1, keepdims=True)
    acc = _dot(p_all[:, n_pages * page:].astype(BF16), _pad_rows(vn_ref[...], page).astype(BF16))
    for p in range(n_pages):
        acc = acc + _dot_nt(p_all[:, p * page:(p + 1) * page].astype(BF16), v_pages[p][...].astype(BF16))
    o_full = jnp.where(head_mask, acc / l, 0.0)
    out = o_full[0:t_dec]
    for h in range(1, n_heads):
        out = out + o_full[h * t_dec:(h + 1) * t_dec]
    o_ref[...] = out


def _moba_sample(q, k_new, v_new, cache_kt, cache_vt, layer, page_table):
    b, t_dec, d = q.shape
    n_pages = page_table.shape[1]
    page = cache_kt.shape[3]
    n_heads = d // HEAD_DIM
    assert MOBA_BLOCK % page == 0 and (n_pages * page) % MOBA_BLOCK == 0 and t_dec <= page
    tok_spec = pl.BlockSpec((None, t_dec, d), lambda i, pt: (i, 0, 0))
    page_specs = [pl.BlockSpec((None, None, d, page),
                               functools.partial(lambda i, pt, p: (layer, pt[i, p], 0, 0), p=p))
                  for p in range(n_pages)]
    grid_spec = pltpu.PrefetchScalarGridSpec(
        num_scalar_prefetch=1,
        grid=(b,),
        in_specs=[tok_spec, tok_spec, tok_spec] + page_specs + page_specs,
        out_specs=tok_spec,
        scratch_shapes=[pltpu.VMEM((n_heads * t_dec, (n_pages + 1) * page), F32)],
    )
    return pl.pallas_call(
        functools.partial(_moba_sample_kernel, n_pages=n_pages, page=page, n_heads=n_heads),
        grid_spec=grid_spec,
        out_shape=jax.ShapeDtypeStruct((b, t_dec, d), F32),
        compiler_params=_params("arbitrary"),
        name="moba_sample",
    )(page_table, q, k_new, v_new, *([cache_kt] * n_pages), *([cache_vt] * n_pages))


def _dil_prompt_kernel(q_ref, k_ref, v_ref, o_ref, lse_ref, *, dil, tq, width, win):
    length = k_ref.shape[0] // dil
    q0 = pl.program_id(2) * tq
    ks = jnp.clip(q0 - win, 0, length - width)
    lane = lax.broadcasted_iota(jnp.int32, (tq, LANES), 1)
    delta = (q0 + lax.broadcasted_iota(jnp.int32, (tq, width), 0)
             - ks - lax.broadcasted_iota(jnp.int32, (tq, width), 1))
    valid = (delta >= 0) & (delta <= win)

    def rows(start, n):
        return pl.ds(start, n) if dil == 1 else pl.ds(start, n, stride=dil)

    def one_class(r, carry):
        q = q_ref[rows(r, tq), :]
        kw = k_ref[rows(ks * dil + r, width), :].astype(BF16)
        vw = v_ref[rows(ks * dil + r, width), :].astype(BF16)
        outs, lses = [], []
        for hh in range(HEADS_PER_TILE):
            qh = (jnp.where((lane // HEAD_DIM) == hh, q, 0.0) * HEAD_DIM ** -0.5).astype(BF16)
            s = jnp.where(valid, _dot_nt(qh, kw), NEG)
            m = jnp.max(s, axis=1, keepdims=True)
            p = jnp.exp(s - m)
            l = jnp.sum(p, axis=1, keepdims=True)
            outs.append(_dot(p.astype(BF16), vw) / l)
            lses.append(jnp.broadcast_to(m + jnp.log(l), (tq, LANES)))
        o_ref[rows(r, tq), :] = jnp.where(lane < HEAD_DIM, outs[0], outs[1])
        lse_ref[rows(r, tq), :] = jnp.where(lane < HEAD_DIM, lses[0], lses[1])
        return carry

    lax.fori_loop(0, dil, one_class, 0)


def _dil_prompt(q, k, v, window, dil):
    b, t, dk = q.shape
    assert t % dil == 0 and HEADS_PER_TILE == 2
    length = t // dil
    win = window // dil
    tq = min(256, length)
    width = min(tq + win, length)
    assert length % tq == 0
    tile_spec = pl.BlockSpec((None, tq * dil, LANES), lambda bi, h, i: (bi, i, h))
    seq_spec = pl.BlockSpec((None, t, LANES), lambda bi, h, i: (bi, 0, h))
    return pl.pallas_call(
        functools.partial(_dil_prompt_kernel, dil=dil, tq=tq, width=width, win=win),
        grid=(b, dk // LANES, length // tq),
        in_specs=[tile_spec, seq_spec, seq_spec],
        out_specs=[tile_spec, tile_spec],
        out_shape=[jax.ShapeDtypeStruct((b, t, dk), F32)] * 2,
        compiler_params=_params("arbitrary", "arbitrary", "arbitrary"),
        name="dilated_prompt",
    )(q, k, v)


def _dil_sample_kernel(q0_ref, q1_ref, q2_ref, kn_ref, vn_ref, kc_ref, vc_ref, o_ref, s_scr, p_scr, *, n_heads):
    q_refs = (q0_ref, q1_ref, q2_ref)
    t_dec, dk = kn_ref.shape
    past = kc_ref.shape[1]
    pad = LANES
    rg = n_heads * t_dec
    head_mask = _head_lane_mask((rg, dk), t_dec)
    qrows = jnp.concatenate(
        [jnp.where(head_mask, jnp.concatenate([qr[...]] * n_heads, axis=0), 0.0) for qr in q_refs], axis=0)
    qs = (qrows * HEAD_DIM ** -0.5).astype(BF16)
    s_scr[:, :past] = _dot(qs, kc_ref[...].astype(BF16))
    s_scr[:, past:] = _dot_nt(qs, _pad_rows(kn_ref[...], pad).astype(BF16))

    t_row = lax.broadcasted_iota(jnp.int32, (rg, past + pad), 0) % t_dec
    col = lax.broadcasted_iota(jnp.int32, (rg, past + pad), 1)
    delta = past + t_row - col
    in_range = col < past + t_dec
    l_g, lse_g = [], []
    for g, (window, dil) in enumerate(B_CONFIGS):
        rs = slice(g * rg, (g + 1) * rg)
        assert dil & (dil - 1) == 0
        valid = in_range & (delta >= 0) & (delta <= window) & ((delta & (dil - 1)) == 0)
        s = jnp.where(valid, s_scr[rs, :], NEG)
        m = jnp.max(s, axis=1, keepdims=True)
        p = jnp.exp(s - m)
        l = jnp.sum(p, axis=1, keepdims=True)
        p_scr[rs, :] = p.astype(BF16)
        l_g.append(l)
        lse_g.append(m + jnp.log(l))
    acc = (_dot_nt(p_scr[:, :past], vc_ref[...].astype(BF16))
           + _dot(p_scr[:, past:], _pad_rows(vn_ref[...], pad).astype(BF16)))
    mx = functools.reduce(jnp.maximum, lse_g)
    e = [jnp.exp(x - mx) for x in lse_g]
    den = functools.reduce(lambda a, b: a + b, e)
    mixed = jnp.zeros((rg, dk), F32)
    for g in range(len(B_CONFIGS)):
        mixed = mixed + (e[g] / den) * (acc[g * rg:(g + 1) * rg] / l_g[g])
    mixed = jnp.where(head_mask, mixed, 0.0)
    out = mixed[0:t_dec]
    for h in range(1, n_heads):
        out = out + mixed[h * t_dec:(h + 1) * t_dec]
    o_ref[...] = out


def _dil_sample(q_groups, k_new, v_new, cache_kt, cache_vt):
    b, t_dec, dk = k_new.shape
    past = cache_kt.shape[2]
    n_heads = dk // HEAD_DIM
    n_rows = len(B_CONFIGS) * n_heads * t_dec
    tok_spec = pl.BlockSpec((None, t_dec, dk), lambda i: (i, 0, 0))
    cache_spec = pl.BlockSpec((None, dk, past), lambda i: (i, 0, 0))
    return pl.pallas_call(
        functools.partial(_dil_sample_kernel, n_heads=n_heads),
        grid=(b,),
        in_specs=[tok_spec] * 5 + [cache_spec] * 2,
        out_specs=tok_spec,
        out_shape=jax.ShapeDtypeStruct((b, t_dec, dk), F32),
        scratch_shapes=[pltpu.VMEM((n_rows, past + LANES), F32),
                        pltpu.VMEM((n_rows, past + LANES), BF16)],
        compiler_params=_params("arbitrary"),
        name="dilated_sample",
    )(*q_groups, k_new, v_new, cache_kt, cache_vt)


def _trunk(x, mods, kv_mod, pos, past, weights):
    (norm_g, w_ffn_gate, w_ffn_up, w_ffn_down, w_qkv_a, w_o_a,
     kv_norm_g, w_kv_b, w_q_b, w_o_b, final_norm_g) = weights
    n_seq, t, d = x.shape
    m = n_seq * t
    depth = norm_g.shape[0]
    n_a = w_qkv_a.shape[0]
    dk = w_kv_b.shape[1] // 2
    is_prompt = past is None
    short = t < 256
    rows = _Rows(m, min(1024 if is_prompt else 512, m if short else t), t)
    rows_p = _Rows(m, min(512, m if short else t), t)
    tables = _rope_tables(pos)
    if rows.per_row:
        tables = (tuple(jnp.tile(tb, (n_seq, 1)) for tb in tables[0]),
                  tuple(jnp.tile(tb, (1, n_seq)) for tb in tables[1]))

    def heads_last(x_t):
        return jnp.transpose(x_t.reshape(n_seq, -1, HEAD_DIM, x_t.shape[-1]), (0, 3, 1, 2))

    h = x.reshape(m, d)
    a_k, a_v = [], []
    b_k = b_v = b_k_out = b_v_out = None
    for l in range(depth):
        if l == n_a:
            sh, sc = (rows.prep(kv_mod[:, i * d:(i + 1) * d]) for i in range(2))
            if is_prompt:
                b_k, b_kt, b_v, b_vt = _proj(rows_p, h, sh, sc, kv_norm_g, w_kv_b, tables,
                                             ((True, ("f32", "t_f32")), (False, ("f32", "t_f32"))))
                b_k_out, b_v_out = heads_last(b_kt), heads_last(b_vt)
            else:
                b_k, b_v = _proj(rows_p, h, sh, sc, kv_norm_g, w_kv_b, tables,
                                 ((True, ("f32",)), (False, ("f32",))))
                b_k_out, b_v_out = (r.reshape(n_seq, t, -1, HEAD_DIM) for r in (b_k, b_v))
        md = [[rows.prep(mods[l][:, (3 * s + c) * d:(3 * s + c + 1) * d]) for c in range(3)] for s in range(3)]
        h = _ffn(rows, h, md[0][0], md[0][1], md[0][2], norm_g[l, 0], w_ffn_gate, w_ffn_up, w_ffn_down, l, 0)
        if l < n_a:
            shp = (n_seq, t, d)
            if is_prompt:
                q_t, k_t, k_bf, km, v_t, vt_bf = _proj(
                    rows_p, h, md[1][0], md[1][1], norm_g[l, 1], w_qkv_a[l], tables,
                    ((True, ("t_f32",)), (True, ("t_f32", "bf16", "block_mean")), (False, ("t_f32", "t_bf16"))))
                o = _moba_prompt(q_t, k_bf.reshape(shp), vt_bf, km)
                a_k.append(heads_last(k_t))
                a_v.append(heads_last(v_t))
            else:
                q, k, v = _proj(rows_p, h, md[1][0], md[1][1], norm_g[l, 1], w_qkv_a[l], tables,
                                ((True, ("f32",)), (True, ("f32",)), (False, ("f32",))))
                cache_kt, cache_vt = (
                    jnp.transpose(c, (0, 1, 3, 4, 2)).reshape(c.shape[0], c.shape[1], d, c.shape[2])
                    for c in past[:2])
                o = _moba_sample(q.reshape(shp), k.reshape(shp), v.reshape(shp), cache_kt, cache_vt, l, past[2])
                a_k.append(k.reshape(n_seq, t, -1, HEAD_DIM))
                a_v.append(v.reshape(n_seq, t, -1, HEAD_DIM))
            h = _oproj(rows, h, md[1][2], w_o_a[l], [o.reshape(m, d)])
        else:
            lb = l - n_a
            qg = _proj(rows_p, h, md[1][0], md[1][1], norm_g[l, 1], w_q_b[lb], tables,
                       ((True, ("f32",)),) * len(B_CONFIGS))
            shp = (n_seq, t, dk)
            if is_prompt:
                res = [_dil_prompt(qi.reshape(shp), b_k.reshape(shp), b_v.reshape(shp), w, dl)
                       for qi, (w, dl) in zip(qg, B_CONFIGS)]
                h = _oproj(rows, h, md[1][2], w_o_b[lb], [r[0].reshape(m, dk) for r in res],
                           [r[1].reshape(m, dk) for r in res])
            else:
                cb_kt, cb_vt = (jnp.transpose(c, (0, 2, 3, 1)).reshape(n_seq, dk, c.shape[1]) for c in past[3:5])
                o = _dil_sample([qi.reshape(shp) for qi in qg], b_k.reshape(shp), b_v.reshape(shp), cb_kt, cb_vt)
                h = _oproj(rows, h, md[1][2], w_o_b[lb], [o.reshape(m, dk)])
        h = _ffn(rows, h, md[2][0], md[2][1], md[2][2], norm_g[l, 2], w_ffn_gate, w_ffn_up, w_ffn_down, l, 1,
                 final_g=final_norm_g if l == depth - 1 else None)
    return h.reshape(n_seq, t, d), jnp.stack(a_k), jnp.stack(a_v), b_k_out, b_v_out


def kernel(x_prompt, x_sample, cache_a_k, cache_a_v, cache_b_k, cache_b_v, page_table, c_prompt, c_sample, norm_g, w_mod, b_mod, w_ffn_gate, w_ffn_up, w_ffn_down, w_qkv_a, w_o_a, kv_norm_g, w_kv_mod, b_kv_mod, w_kv_b, w_q_b, w_o_b, final_norm_g):
    bp, t, d = x_prompt.shape
    bs, t_dec, _ = x_sample.shape
    past_len = page_table.shape[1] * cache_a_k.shape[2]
    assert past_len % MOBA_BLOCK == 0 and t_dec <= MOBA_BLOCK

    n_c = bp + bs
    c_all = _pad_rows(jnp.concatenate([c_prompt, c_sample], axis=0), -(-n_c // 8) * 8)
    mods = _modulation(c_all, w_mod, b_mod)
    kv_mod = _modulation(c_all, w_kv_mod[None], b_kv_mod[None])[0]

    bf = lambda w: w.astype(BF16)
    weights = (norm_g, *_ffn_weights(w_ffn_gate, w_ffn_up, w_ffn_down), bf(w_qkv_a), bf(w_o_a),
               kv_norm_g, bf(w_kv_b), bf(w_q_b), bf(w_o_b), final_norm_g)
    y_p, ak_p, av_p, bk_p, bv_p = _trunk(
        x_prompt, mods[:, :bp], kv_mod[:bp], jnp.arange(t, dtype=jnp.int32), None, weights)
    y_s, ak_s, av_s, bk_s, bv_s = _trunk(
        x_sample, mods[:, bp:n_c], kv_mod[bp:n_c], past_len + jnp.arange(t_dec, dtype=jnp.int32),
        (cache_a_k, cache_a_v, page_table, cache_b_k, cache_b_v), weights)

    keep = min(max(w for w, _ in B_CONFIGS), t)
    return (y_p, y_s, ak_p, av_p, ak_s, av_s, bk_p[:, -keep:], bv_p[:, -keep:], bk_s, bv_s)
```

```python
import functools

import jax
import jax.numpy as jnp
from jax import lax
from jax.experimental import pallas as pl
from jax.experimental.pallas import tpu as pltpu

HEAD_DIM = 64
ROT_DIM = HEAD_DIM // 4
ROPE_THETA = 500000.0
MOBA_BLOCK = 256
MOBA_TOPK = 3
B_CONFIGS = ((128, 1), (512, 4), (2048, 16))
EPS = 1e-6
NEG = -1e30
LOG2_E = 1.4426950408889634

LANES = 128
HEADS_PER_TILE = LANES // HEAD_DIM
VMEM_LIMIT = 56 * 1024 * 1024

F32 = jnp.float32
BF16 = jnp.bfloat16


def _params(*sem):
    return pltpu.CompilerParams(dimension_semantics=sem, vmem_limit_bytes=VMEM_LIMIT)


def _dot(a, b):
    return jnp.dot(a, b, preferred_element_type=F32)


def _dot_nt(a, b):
    return lax.dot_general(a, b, (((1,), (1,)), ((), ())), preferred_element_type=F32)


def _silu(x):
    return x * jax.nn.sigmoid(x)


def _rmsnorm(x, g):
    return x * lax.rsqrt(jnp.mean(x * x, axis=-1, keepdims=True) + EPS) * g


def _rms_mod(x, g, shift, scale):
    return _rmsnorm(x, g) * (1.0 + scale) + shift


def _head_lane_mask(shape, rows_per_head):
    row = lax.broadcasted_iota(jnp.int32, shape, 0)
    lane = lax.broadcasted_iota(jnp.int32, shape, 1)
    return (lane // HEAD_DIM) == (row // rows_per_head)


class _Rows:
    def __init__(self, m, tm, rows_per_seq):
        self.m, self.tm = m, tm
        self.per_row = rows_per_seq < tm
        self.rows_per_seq = rows_per_seq
        self.tiles_per_seq = max(rows_per_seq // tm, 1)

    def prep(self, vecs):
        if self.per_row:
            return jnp.repeat(vecs, self.rows_per_seq, axis=0)
        return vecs[:, None, :]

    def mod_spec(self, d, col):
        if self.per_row:
            return pl.BlockSpec((self.tm, d), lambda i, j: (i, col))
        tps = self.tiles_per_seq
        return pl.BlockSpec((None, 1, d), lambda i, j: (i // tps, 0, col))

    def pos_spec(self, table_rows):
        n_pos_tiles = table_rows // self.tm
        return pl.BlockSpec((self.tm, LANES), lambda i, j: (i % n_pos_tiles, 0))


def _mod_kernel(c_ref, w_ref, b_ref, o_ref):
    cs = _silu(c_ref[...]).astype(BF16)
    o_ref[...] = _dot(cs, w_ref[...].astype(BF16)) + b_ref[...]


def _modulation(c, w, b):
    mc, d = c.shape
    nl, _, n = w.shape
    tn = d
    return pl.pallas_call(
        _mod_kernel,
        grid=(nl, n // tn),
        in_specs=[pl.BlockSpec((mc, d), lambda l, j: (0, 0)),
                  pl.BlockSpec((None, d, tn), lambda l, j: (l, 0, j)),
                  pl.BlockSpec((None, 1, tn), lambda l, j: (l, 0, j))],
        out_specs=pl.BlockSpec((None, mc, tn), lambda l, j: (l, 0, j)),
        out_shape=jax.ShapeDtypeStruct((nl, mc, n), F32),
        compiler_params=_params("arbitrary", "arbitrary"),
        name="modulation",
    )(c, w, b.reshape(nl, 1, n))


def _ffn_kernel(h_ref, sh_ref, sc_ref, gt_ref, g_ref, wg_ref, wu_ref, wd_ref, *rest, final):
    if final:
        fg_ref, o_ref, u_scr, act_scr = rest
    else:
        o_ref, u_scr, act_scr = rest
    n_f, _, tf = wg_ref.shape
    n_o, _, to = wd_ref.shape
    u_scr[...] = _rms_mod(h_ref[...], g_ref[...], sh_ref[...], sc_ref[...]).astype(BF16)
    for c in range(n_f):
        u = u_scr[...]
        a = _dot(u, wg_ref[c])
        b = _dot(u, wu_ref[c])
        act_scr[:, c * tf:(c + 1) * tf] = (_silu(a) * b).astype(BF16)
    for c in range(n_o):
        sl = slice(c * to, (c + 1) * to)
        down = _dot(act_scr[...], wd_ref[c])
        o_ref[:, sl] = h_ref[:, sl] + 0.5 * gt_ref[:, sl] * down
    if final:
        o_ref[...] = _rmsnorm(o_ref[...], fg_ref[...])


def _ffn_weights(wg, wu, wd):
    d, ff = wg.shape[-2:]
    tf = 256 if ff % 256 == 0 else LANES
    to = min(512, d)
    lead = wg.shape[:-2]
    n_lead = len(lead)
    perm = tuple(range(n_lead)) + (n_lead + 1, n_lead, n_lead + 2)
    chunk = lambda w, rows_, width: jnp.transpose(
        w.astype(BF16).reshape(lead + (rows_, w.shape[-1] // width, width)), perm)
    return chunk(wg, d, tf), chunk(wu, d, tf), chunk(wd, ff, to)


def _ffn(rows, h, mod, cols, g, wg, wu, wd, l, s, final_g=None):
    m, d = h.shape
    n_f, _, tf = wg.shape[-3:]
    n_o, ff, to = wd.shape[-3:]
    tm = rows.tm
    final = final_g is not None
    resident = lambda shape: pl.BlockSpec((None, None) + shape, lambda i, f: (l, s, 0, 0, 0),
                                          pipeline_mode=pl.Buffered(1))
    in_specs = [pl.BlockSpec((tm, d), lambda i, f: (i, 0)),
                rows.mod_spec(d, cols[0]), rows.mod_spec(d, cols[1]), rows.mod_spec(d, cols[2]),
                pl.BlockSpec((1, d), lambda i, f: (0, 0)),
                resident((n_f, d, tf)), resident((n_f, d, tf)), resident((n_o, ff, to))]
    args = [h, mod, mod, mod, g.reshape(1, d), wg, wu, wd]
    if final:
        in_specs.append(pl.BlockSpec((1, d), lambda i, f: (0, 0)))
        args.append(final_g.reshape(1, d))
    return pl.pallas_call(
        functools.partial(_ffn_kernel, final=final),
        grid=(m // tm, 1),
        in_specs=in_specs,
        out_specs=pl.BlockSpec((tm, d), lambda i, f: (i, 0)),
        out_shape=jax.ShapeDtypeStruct((m, d), F32),
        scratch_shapes=[pltpu.VMEM((tm, d), BF16), pltpu.VMEM((tm, ff), BF16)],
        compiler_params=_params("arbitrary", "arbitrary"),
        name="ffn",
    )(*args)


def _rope_tables(pos):
    half = ROT_DIM // 2
    n = pos.shape[0]
    freq = ROPE_THETA ** (-jnp.arange(half, dtype=F32) / half)
    ang = pos.astype(F32)[:, None] * freq[None, :]
    cos, sin = jnp.cos(ang), jnp.sin(ang)
    rest = HEAD_DIM - ROT_DIM
    ct = jnp.concatenate([cos, cos, jnp.ones((n, rest), F32)], axis=1)
    s_lo = jnp.concatenate([-sin, jnp.zeros((n, half + rest), F32)], axis=1)
    s_hi = jnp.concatenate([jnp.zeros((n, half), F32), sin, jnp.zeros((n, rest), F32)], axis=1)
    tile = lambda t: jnp.tile(t, (1, HEADS_PER_TILE))
    return (tile(ct), tile(s_hi), tile(s_lo)), (cos.T, sin.T)


def _rope_tile(x, ct, s_hi, s_lo):
    half = ROT_DIM // 2
    return x * ct + pltpu.roll(x, half, 1) * s_hi + pltpu.roll(x, LANES - half, 1) * s_lo


def _rope_t(x_t, cos_t, sin_t):
    half = ROT_DIM // 2
    pieces = []
    for base in range(0, x_t.shape[0], HEAD_DIM):
        x1, x2 = x_t[base:base + half], x_t[base + half:base + ROT_DIM]
        pieces += [x1 * cos_t - x2 * sin_t, x2 * cos_t + x1 * sin_t, x_t[base + ROT_DIM:base + HEAD_DIM]]
    return jnp.concatenate(pieces, axis=0)


_MEAN_GROUP = 8


def _proj_kernel(h_ref, sh_ref, sc_ref, g_ref, w_ref, ct_ref, shi_ref, slo_ref, cos_t_ref, sin_t_ref,
                 *outs, parts):
    i = pl.program_id(0)
    u = _rms_mod(h_ref[...], g_ref[...], sh_ref[...], sc_ref[...]).astype(BF16)
    tm = u.shape[0]
    n_p = w_ref.shape[1] // len(parts)
    blocks = tm // MOBA_BLOCK
    n_out = 0
    for p, (rope, forms) in enumerate(parts):
        part_outs = outs[n_out:n_out + len(forms)]
        n_out += len(forms)
        y = _dot(u, w_ref[:, p * n_p:(p + 1) * n_p])
        val = val_t = None
        if any(f.startswith("t_") for f in forms):
            val_t = y.T
            if rope:
                val_t = _rope_t(val_t, cos_t_ref[...], sin_t_ref[...])
            if any(not f.startswith("t_") for f in forms):
                val = val_t.T
        elif rope:
            ct, s_hi, s_lo = ct_ref[...], shi_ref[...], slo_ref[...]
            val = jnp.concatenate([_rope_tile(y[:, c * LANES:(c + 1) * LANES], ct, s_hi, s_lo)
                                   for c in range(n_p // LANES)], axis=1)
        else:
            val = y
        for form, out in zip(forms, part_outs):
            if form == "f32":
                out[...] = val
            elif form == "bf16":
                out[...] = val.astype(BF16)
            elif form == "t_f32":
                out[...] = val_t
            elif form == "t_bf16":
                for c in range(blocks):
                    out[c] = val_t[:, c * MOBA_BLOCK:(c + 1) * MOBA_BLOCK].astype(BF16)
            elif form == "block_mean":
                base = (i % (_MEAN_GROUP // blocks)) * blocks
                for c in range(blocks):
                    out[pl.ds(base + c, 1), :] = jnp.mean(
                        val[c * MOBA_BLOCK:(c + 1) * MOBA_BLOCK], axis=0, keepdims=True)


def _proj(rows, h, mod, cols, g, w, tables, parts):
    m, d = h.shape
    n_p = w.shape[1] // len(parts)
    tm, tps, t = rows.tm, rows.tiles_per_seq, rows.rows_per_seq
    n_seq = m // t
    out_specs, out_shape = [], []
    for _, forms in parts:
        for form in forms:
            if form in ("f32", "bf16"):
                out_specs.append(pl.BlockSpec((tm, n_p), lambda i, j: (i, 0)))
                out_shape.append(jax.ShapeDtypeStruct((m, n_p), F32 if form == "f32" else BF16))
                continue
            assert not rows.per_row and tm % MOBA_BLOCK == 0
            if form == "t_f32":
                out_specs.append(pl.BlockSpec((None, n_p, tm), lambda i, j: (i // tps, 0, i % tps)))
                out_shape.append(jax.ShapeDtypeStruct((n_seq, n_p, t), F32))
            elif form == "t_bf16":
                out_specs.append(pl.BlockSpec((None, tm // MOBA_BLOCK, n_p, MOBA_BLOCK),
                                              lambda i, j: (i // tps, i % tps, 0, 0)))
                out_shape.append(jax.ShapeDtypeStruct((n_seq, t // MOBA_BLOCK, n_p, MOBA_BLOCK), BF16))
            else:
                steps = _MEAN_GROUP * MOBA_BLOCK // tm
                assert form == "block_mean" and tps % steps == 0
                out_specs.append(pl.BlockSpec((None, _MEAN_GROUP, n_p),
                                              lambda i, j: (i // tps, (i % tps) // steps, 0)))
                out_shape.append(jax.ShapeDtypeStruct((n_seq, t // MOBA_BLOCK, n_p), F32))
    lane_tables, t_tables = tables
    pos_spec = rows.pos_spec(lane_tables[0].shape[0])
    n_pos_tiles = lane_tables[0].shape[0] // tm
    pos_t_spec = pl.BlockSpec((ROT_DIM // 2, tm), lambda i, j: (0, i % n_pos_tiles))
    return pl.pallas_call(
        functools.partial(_proj_kernel, parts=parts),
        grid=(m // tm, 1),
        in_specs=[pl.BlockSpec((tm, d), lambda i, j: (i, 0)),
                  rows.mod_spec(d, cols[0]), rows.mod_spec(d, cols[1]),
                  pl.BlockSpec((1, d), lambda i, j: (0, 0)),
                  pl.BlockSpec(w.shape, lambda i, j: (0, 0)),
                  pos_spec, pos_spec, pos_spec, pos_t_spec, pos_t_spec],
        out_specs=out_specs,
        out_shape=out_shape,
        compiler_params=_params("arbitrary", "arbitrary"),
        name="proj",
    )(h, mod, mod, g.reshape(1, d), w, *lane_tables, *t_tables)


def _oproj_kernel(h_ref, gt_ref, w_ref, *rest, n_groups):
    o_ref = rest[-1]
    if n_groups == 0:
        o = rest[0][...]
    else:
        os_, lses = rest[:n_groups], rest[n_groups:2 * n_groups]
        lse = [r[...] for r in lses]
        mx = functools.reduce(jnp.maximum, lse)
        e = [jnp.exp(x - mx) for x in lse]
        den = functools.reduce(lambda a, b: a + b, e)
        o = functools.reduce(lambda a, b: a + b, [(ei / den) * r[...] for ei, r in zip(e, os_)])
    o_ref[...] = h_ref[...] + gt_ref[...] * _dot(o.astype(BF16), w_ref[...])


def _oproj(rows, h, mod, col, w, o_list, lse_list=()):
    m, d = h.shape
    k = w.shape[0]
    tm = rows.tm
    xs = list(o_list) + list(lse_list)
    return pl.pallas_call(
        functools.partial(_oproj_kernel, n_groups=len(lse_list)),
        grid=(m // tm, 1),
        in_specs=[pl.BlockSpec((tm, d), lambda i, j: (i, 0)),
                  rows.mod_spec(d, col),
                  pl.BlockSpec((k, d), lambda i, j: (0, 0))]
                 + [pl.BlockSpec((tm, k), lambda i, j: (i, 0))] * len(xs),
        out_specs=pl.BlockSpec((tm, d), lambda i, j: (i, 0)),
        out_shape=jax.ShapeDtypeStruct((m, d), F32),
        compiler_params=_params("arbitrary", "arbitrary"),
        name="oproj",
    )(h, mod, w, *xs)


def _split_bf16(x):
    hi = x.astype(BF16)
    lo = (x - hi.astype(F32)).astype(BF16)
    return hi, lo


def _top_blocks_t(gate, n_past):
    blk_id = lax.broadcasted_iota(jnp.int32, gate.shape, 0).astype(F32)
    n_past = n_past.astype(F32)
    g = jnp.where(blk_id < n_past, gate, NEG)
    sel = jnp.zeros(gate.shape, jnp.bool_)
    for _ in range(MOBA_TOPK):
        mx = jnp.max(g, axis=0, keepdims=True)
        idx = jnp.min(jnp.where(g == mx, blk_id, float(gate.shape[0])), axis=0, keepdims=True)
        pick = blk_id == idx
        sel = sel | pick
        g = jnp.where(pick, -jnp.inf, g)
    return sel & (blk_id < n_past)


def _moba_prompt_kernel(q_ref, k_ref, vt_ref, km_ref, o_ref,
                        qt_scr, sel_scr, m_scr, l_scr, acc_scr, s_own, s_even, s_odd):
    blk = MOBA_BLOCK
    n_h, width = qt_scr.shape[0], qt_scr.shape[1]
    nb = vt_ref.shape[0]
    i = pl.program_id(2)
    heads = range(n_h)
    hs = [slice(hh * HEAD_DIM, (hh + 1) * HEAD_DIM) for hh in heads]

    def scores_into(dst, j):
        jj = jnp.minimum(j, nb - 1)
        kb = k_ref[pl.ds(pl.multiple_of(jj * blk, blk), blk), :]
        for hh in heads:
            dst[hh] = _dot(kb, qt_scr[hh])

    def consume(src, j, own):
        jj = jnp.minimum(j, nb - 1)
        vtb = vt_ref[jj]
        ps, alphas, chosen = [], [], []
        for hh in heads:
            s = src[hh]
            if own:
                key_id = lax.broadcasted_iota(jnp.int32, (blk, blk), 0)
                qry_id = lax.broadcasted_iota(jnp.int32, (blk, blk), 1)
                s = jnp.where(key_id <= qry_id, s, NEG)
                ch = jnp.full((1, blk), True)
            else:
                ch = (sel_scr[hh, pl.ds(jj, 1), :] > 0.5) & (j < i)
            m_old = m_scr[hh]
            m_all = jnp.maximum(m_old, jnp.max(s, axis=0, keepdims=True))
            p = jnp.exp2(s - m_all)
            m_new = jnp.where(ch, m_all, m_old)
            alpha = jnp.exp2(m_old - m_new)
            l_scr[hh] = alpha * l_scr[hh] + jnp.where(ch, jnp.sum(p, axis=0, keepdims=True), 0.0)
            m_scr[hh] = m_new
            ps.append(p.astype(BF16))
            alphas.append(alpha)
            chosen.append(ch)
        pvs = [_dot(vtb[hs[hh], :], ps[hh]) for hh in heads]
        for hh in heads:
            acc_scr[hh] = alphas[hh] * acc_scr[hh] + jnp.where(chosen[hh], pvs[hh], 0.0)

    qt = q_ref[...]
    feat = lax.broadcasted_iota(jnp.int32, (width, blk), 0)
    km_hi, km_lo = _split_bf16(km_ref[...])
    for hh in heads:
        qth = jnp.where((feat // HEAD_DIM) == hh, qt, 0.0)
        qt_scr[hh] = (qth * (HEAD_DIM ** -0.5 * LOG2_E)).astype(BF16)
        q_hi, q_lo = _split_bf16(qth)
        gate = _dot(km_hi, q_hi) + _dot(km_lo, q_hi) + _dot(km_hi, q_lo)
        sel_scr[hh] = _top_blocks_t(gate, i).astype(F32)
        m_scr[hh] = jnp.full((1, blk), NEG, F32)
        l_scr[hh] = jnp.zeros((1, blk), F32)
        acc_scr[hh] = jnp.zeros((HEAD_DIM, blk), F32)
    scores_into(s_own, i)
    scores_into(s_even, 0)
    consume(s_own, i, own=True)

    def two_blocks(c, carry):
        j = 2 * c
        scores_into(s_odd, j + 1)
        consume(s_even, j, own=False)
        scores_into(s_even, j + 2)
        consume(s_odd, j + 1, own=False)
        return carry

    lax.fori_loop(0, (i + 1) // 2, two_blocks, 0)
    ot = jnp.concatenate([acc_scr[hh] / l_scr[hh] for hh in heads], axis=0)
    o_ref[...] = ot.T


def _moba_prompt(q_t, k_bf, vt_bf, km):
    b, d, t = q_t.shape
    blk = MOBA_BLOCK
    nb = t // blk
    width = min(d, 2 * LANES)
    n_h = width // HEAD_DIM
    assert t % blk == 0 and nb % 8 == 0 and d % width == 0
    tile_spec = pl.BlockSpec((None, blk, width), lambda bi, hg, i: (bi, i, hg))
    return pl.pallas_call(
        _moba_prompt_kernel,
        grid=(b, d // width, nb),
        in_specs=[pl.BlockSpec((None, width, blk), lambda bi, hg, i: (bi, hg, i)),
                  pl.BlockSpec((None, t, width), lambda bi, hg, i: (bi, 0, hg)),
                  pl.BlockSpec((None, nb, width, blk), lambda bi, hg, i: (bi, 0, hg, 0)),
                  pl.BlockSpec((None, nb, width), lambda bi, hg, i: (bi, 0, hg))],
        out_specs=tile_spec,
        out_shape=jax.ShapeDtypeStruct((b, t, d), F32),
        scratch_shapes=[pltpu.VMEM((n_h, width, blk), BF16),
                        pltpu.VMEM((n_h, nb, blk), F32),
                        pltpu.VMEM((n_h, 1, blk), F32),
                        pltpu.VMEM((n_h, 1, blk), F32),
                        pltpu.VMEM((n_h, HEAD_DIM, blk), F32)]
                       + [pltpu.VMEM((n_h, blk, blk), F32)] * 3,
        compiler_params=_params("arbitrary", "arbitrary", "arbitrary"),
        name="moba_prompt",
    )(q_t, k_bf, vt_bf, km)


def _pad_rows(x, n):
    return jnp.concatenate([x, jnp.zeros((n - x.shape[0], x.shape[1]), x.dtype)], axis=0)


def _moba_sample_kernel(pt_ref, q_ref, kn_ref, vn_ref, *rest, n_pages, page, n_heads):
    k_pages, v_pages = rest[:n_pages], rest[n_pages:2 * n_pages]
    o_ref, s_scr = rest[2 * n_pages], rest[2 * n_pages + 1]
    t_dec, d = q_ref.shape
    r = n_heads * t_dec
    ppb = MOBA_BLOCK // page
    n_past = n_pages // ppb
    own = lax.broadcasted_iota(jnp.int32, (r, d), 0) // t_dec
    head_mask = (lax.broadcasted_iota(jnp.int32, (r, d), 1) // HEAD_DIM) == own
    qrows = jnp.where(head_mask, jnp.concatenate([q_ref[...]] * n_heads, axis=0), 0.0)
    qs = (qrows * HEAD_DIM ** -0.5).astype(BF16)

    gates = []
    for j in range(n_past):
        for p in range(j * ppb, (j + 1) * ppb):
            s_scr[:, p * page:(p + 1) * page] = _dot(qs, k_pages[p][...].astype(BF16))
        gates.append(jnp.sum(s_scr[:, j * MOBA_BLOCK:(j + 1) * MOBA_BLOCK], axis=1, keepdims=True))
    for j in range(n_past):
        rank = jnp.zeros((r, 1), jnp.int32)
        for j2 in range(n_past):
            if j2 != j:
                ahead = (gates[j2] > gates[j]) | ((gates[j2] == gates[j]) & (j2 < j))
                rank = rank + ahead.astype(jnp.int32)
        chosen = rank < MOBA_TOPK
        sl = slice(j * MOBA_BLOCK, (j + 1) * MOBA_BLOCK)
        s_scr[:, sl] = jnp.where(chosen, s_scr[:, sl], NEG)
    t_row = lax.broadcasted_iota(jnp.int32, (r, page), 0) % t_dec
    t_col = lax.broadcasted_iota(jnp.int32, (r, page), 1)
    s_own = _dot_nt(qs, _pad_rows(kn_ref[...], page).astype(BF16))
    s_scr[:, n_pages * page:] = jnp.where(t_col <= t_row, s_own, NEG)

    s = s_scr[...]
    m = jnp.max(s, axis=1, keepdims=True)
    p_all = jnp.exp(s - m)
    l = jnp.sum(p_all, axis=1, keepdims=True)
    acc = _dot(p_all[:, n_pages * page:].astype(BF16), _pad_rows(vn_ref[...], page).astype(BF16))
    for p in range(n_pages):
        acc = acc + _dot_nt(p_all[:, p * page:(p + 1) * page].astype(BF16), v_pages[p][...].astype(BF16))
    o_full = jnp.where(head_mask, acc / l, 0.0)
    out = o_full[0:t_dec]
    for h in range(1, n_heads):
        out = out + o_full[h * t_dec:(h + 1) * t_dec]
    o_ref[...] = out


def _moba_sample(q, k_new, v_new, cache_kt, cache_vt, layer, page_table):
    b, t_dec, d = q.shape
    n_pages = page_table.shape[1]
    page = cache_kt.shape[3]
    n_heads = d // HEAD_DIM
    assert MOBA_BLOCK % page == 0 and (n_pages * page) % MOBA_BLOCK == 0 and t_dec <= page
    tok_spec = pl.BlockSpec((None, t_dec, d), lambda i, pt: (i, 0, 0))
    page_specs = [pl.BlockSpec((None, None, d, page),
                               functools.partial(lambda i, pt, p: (layer, pt[i, p], 0, 0), p=p))
                  for p in range(n_pages)]
    grid_spec = pltpu.PrefetchScalarGridSpec(
        num_scalar_prefetch=1,
        grid=(b,),
        in_specs=[tok_spec, tok_spec, tok_spec] + page_specs + page_specs,
        out_specs=tok_spec,
        scratch_shapes=[pltpu.VMEM((n_heads * t_dec, (n_pages + 1) * page), F32)],
    )
    return pl.pallas_call(
        functools.partial(_moba_sample_kernel, n_pages=n_pages, page=page, n_heads=n_heads),
        grid_spec=grid_spec,
        out_shape=jax.ShapeDtypeStruct((b, t_dec, d), F32),
        compiler_params=_params("arbitrary"),
        name="moba_sample",
    )(page_table, q, k_new, v_new, *([cache_kt] * n_pages), *([cache_vt] * n_pages))


def _dil_prompt_kernel(q_ref, k_ref, v_ref, o_ref, lse_ref, *, dil, tq, width, win):
    length = k_ref.shape[0] // dil
    q0 = pl.program_id(2) * tq
    ks = jnp.clip(q0 - win, 0, length - width)
    lane = lax.broadcasted_iota(jnp.int32, (tq, LANES), 1)
    delta = (q0 + lax.broadcasted_iota(jnp.int32, (tq, width), 0)
             - ks - lax.broadcasted_iota(jnp.int32, (tq, width), 1))
    valid = (delta >= 0) & (delta <= win)

    def rows(start, n):
        return pl.ds(start, n) if dil == 1 else pl.ds(start, n, stride=dil)

    def one_class(r, carry):
        q = q_ref[rows(r, tq), :]
        kw = k_ref[rows(ks * dil + r, width), :].astype(BF16)
        vw = v_ref[rows(ks * dil + r, width), :].astype(BF16)
        outs, lses = [], []
        for hh in range(HEADS_PER_TILE):
            qh = (jnp.where((lane // HEAD_DIM) == hh, q, 0.0) * HEAD_DIM ** -0.5).astype(BF16)
            s = jnp.where(valid, _dot_nt(qh, kw), NEG)
            m = jnp.max(s, axis=1, keepdims=True)
            p = jnp.exp(s - m)
            l = jnp.sum(p, axis=1, keepdims=True)
            outs.append(_dot(p.astype(BF16), vw) / l)
            lses.append(jnp.broadcast_to(m + jnp.log(l), (tq, LANES)))
        o_ref[rows(r, tq), :] = jnp.where(lane < HEAD_DIM, outs[0], outs[1])
        lse_ref[rows(r, tq), :] = jnp.where(lane < HEAD_DIM, lses[0], lses[1])
        return carry

    lax.fori_loop(0, dil, one_class, 0)


def _dil_prompt(q, k, v, window, dil):
    b, t, dk = q.shape
    assert t % dil == 0 and HEADS_PER_TILE == 2
    length = t // dil
    win = window // dil
    tq = min(256, length)
    width = min(tq + win, length)
    assert length % tq == 0
    tile_spec = pl.BlockSpec((None, tq * dil, LANES), lambda bi, h, i: (bi, i, h))
    seq_spec = pl.BlockSpec((None, t, LANES), lambda bi, h, i: (bi, 0, h))
    return pl.pallas_call(
        functools.partial(_dil_prompt_kernel, dil=dil, tq=tq, width=width, win=win),
        grid=(b, dk // LANES, length // tq),
        in_specs=[tile_spec, seq_spec, seq_spec],
        out_specs=[tile_spec, tile_spec],
        out_shape=[jax.ShapeDtypeStruct((b, t, dk), F32)] * 2,
        compiler_params=_params("arbitrary", "arbitrary", "arbitrary"),
        name="dilated_prompt",
    )(q, k, v)


def _dil_sample_kernel(q0_ref, q1_ref, q2_ref, kn_ref, vn_ref, kc_ref, vc_ref, o_ref, s_scr, p_scr, *, n_heads):
    q_refs = (q0_ref, q1_ref, q2_ref)
    t_dec, dk = kn_ref.shape
    past = kc_ref.shape[1]
    pad = LANES
    rg = n_heads * t_dec
    head_mask = _head_lane_mask((rg, dk), t_dec)
    qrows = jnp.concatenate(
        [jnp.where(head_mask, jnp.concatenate([qr[...]] * n_heads, axis=0), 0.0) for qr in q_refs], axis=0)
    qs = (qrows * HEAD_DIM ** -0.5).astype(BF16)
    s_scr[:, :past] = _dot(qs, kc_ref[...].astype(BF16))
    s_scr[:, past:] = _dot_nt(qs, _pad_rows(kn_ref[...], pad).astype(BF16))

    t_row = lax.broadcasted_iota(jnp.int32, (rg, past + pad), 0) % t_dec
    col = lax.broadcasted_iota(jnp.int32, (rg, past + pad), 1)
    delta = past + t_row - col
    in_range = col < past + t_dec
    l_g, lse_g = [], []
    for g, (window, dil) in enumerate(B_CONFIGS):
        rs = slice(g * rg, (g + 1) * rg)
        assert dil & (dil - 1) == 0
        valid = in_range & (delta >= 0) & (delta <= window) & ((delta & (dil - 1)) == 0)
        s = jnp.where(valid, s_scr[rs, :], NEG)
        m = jnp.max(s, axis=1, keepdims=True)
        p = jnp.exp(s - m)
        l = jnp.sum(p, axis=1, keepdims=True)
        p_scr[rs, :] = p.astype(BF16)
        l_g.append(l)
        lse_g.append(m + jnp.log(l))
    acc = (_dot_nt(p_scr[:, :past], vc_ref[...].astype(BF16))
           + _dot(p_scr[:, past:], _pad_rows(vn_ref[...], pad).astype(BF16)))
    mx = functools.reduce(jnp.maximum, lse_g)
    e = [jnp.exp(x - mx) for x in lse_g]
    den = functools.reduce(lambda a, b: a + b, e)
    mixed = jnp.zeros((rg, dk), F32)
    for g in range(len(B_CONFIGS)):
        mixed = mixed + (e[g] / den) * (acc[g * rg:(g + 1) * rg] / l_g[g])
    mixed = jnp.where(head_mask, mixed, 0.0)
    out = mixed[0:t_dec]
    for h in range(1, n_heads):
        out = out + mixed[h * t_dec:(h + 1) * t_dec]
    o_ref[...] = out


def _dil_sample(q_groups, k_new, v_new, cache_kt, cache_vt):
    b, t_dec, dk = k_new.shape
    past = cache_kt.shape[2]
    n_heads = dk // HEAD_DIM
    n_rows = len(B_CONFIGS) * n_heads * t_dec
    tok_spec = pl.BlockSpec((None, t_dec, dk), lambda i: (i, 0, 0))
    cache_spec = pl.BlockSpec((None, dk, past), lambda i: (i, 0, 0))
    return pl.pallas_call(
        functools.partial(_dil_sample_kernel, n_heads=n_heads),
        grid=(b,),
        in_specs=[tok_spec] * 5 + [cache_spec] * 2,
        out_specs=tok_spec,
        out_shape=jax.ShapeDtypeStruct((b, t_dec, dk), F32),
        scratch_shapes=[pltpu.VMEM((n_rows, past + LANES), F32),
                        pltpu.VMEM((n_rows, past + LANES), BF16)],
        compiler_params=_params("arbitrary"),
        name="dilated_sample",
    )(*q_groups, k_new, v_new, cache_kt, cache_vt)


def _trunk(x, mods, kv_mod, pos, past, weights):
    (norm_g, w_ffn_gate, w_ffn_up, w_ffn_down, w_qkv_a, w_o_a,
     kv_norm_g, w_kv_b, w_q_b, w_o_b, final_norm_g) = weights
    n_seq, t, d = x.shape
    m = n_seq * t
    depth = norm_g.shape[0]
    n_a = w_qkv_a.shape[0]
    dk = w_kv_b.shape[1] // 2
    is_prompt = past is None
    short = t < 256
    rows = _Rows(m, min(1024 if is_prompt else 512, m if short else t), t)
    rows_p = _Rows(m, min(512, m if short else t), t)
    tables = _rope_tables(pos)
    if rows.per_row:
        tables = (tuple(jnp.tile(tb, (n_seq, 1)) for tb in tables[0]),
                  tuple(jnp.tile(tb, (1, n_seq)) for tb in tables[1]))

    def heads_last(x_t):
        return jnp.transpose(x_t.reshape(n_seq, -1, HEAD_DIM, x_t.shape[-1]), (0, 3, 1, 2))

    h = x.reshape(m, d)
    a_k, a_v = [], []
    b_k = b_v = b_k_out = b_v_out = None
    for l in range(depth):
        if l == n_a:
            kvm = rows.prep(kv_mod)
            if is_prompt:
                b_k, b_kt, b_v, b_vt = _proj(rows_p, h, kvm, (0, 1), kv_norm_g, w_kv_b, tables,
                                             ((True, ("f32", "t_f32")), (False, ("f32", "t_f32"))))
                b_k_out, b_v_out = heads_last(b_kt), heads_last(b_vt)
            else:
                b_k, b_v = _proj(rows_p, h, kvm, (0, 1), kv_norm_g, w_kv_b, tables,
                                 ((True, ("f32",)), (False, ("f32",))))
                b_k_out, b_v_out = (r.reshape(n_seq, t, -1, HEAD_DIM) for r in (b_k, b_v))
        md = rows.prep(mods[l])
        h = _ffn(rows, h, md, (0, 1, 2), norm_g[l, 0], w_ffn_gate, w_ffn_up, w_ffn_down, l, 0)
        if l < n_a:
            shp = (n_seq, t, d)
            if is_prompt:
                q_t, k_t, k_bf, km, v_t, vt_bf = _proj(
                    rows_p, h, md, (3, 4), norm_g[l, 1], w_qkv_a[l], tables,
                    ((True, ("t_f32",)), (True, ("t_f32", "bf16", "block_mean")), (False, ("t_f32", "t_bf16"))))
                o = _moba_prompt(q_t, k_bf.reshape(shp), vt_bf, km)
                a_k.append(heads_last(k_t))
                a_v.append(heads_last(v_t))
            else:
                q, k, v = _proj(rows_p, h, md, (3, 4), norm_g[l, 1], w_qkv_a[l], tables,
                                ((True, ("f32",)), (True, ("f32",)), (False, ("f32",))))
                cache_kt, cache_vt = (
                    jnp.transpose(c, (0, 1, 3, 4, 2)).reshape(c.shape[0], c.shape[1], d, c.shape[2])
                    for c in past[:2])
                o = _moba_sample(q.reshape(shp), k.reshape(shp), v.reshape(shp), cache_kt, cache_vt, l, past[2])
                a_k.append(k.reshape(n_seq, t, -1, HEAD_DIM))
                a_v.append(v.reshape(n_seq, t, -1, HEAD_DIM))
            h = _oproj(rows, h, md, 5,w_o_a[l], [o.reshape(m, d)])
        else:
            lb = l - n_a
            qg = _proj(rows_p, h, md, (3, 4), norm_g[l, 1], w_q_b[lb], tables,
                       ((True, ("f32",)),) * len(B_CONFIGS))
            shp = (n_seq, t, dk)
            if is_prompt:
                res = [_dil_prompt(qi.reshape(shp), b_k.reshape(shp), b_v.reshape(shp), w, dl)
                       for qi, (w, dl) in zip(qg, B_CONFIGS)]
                h = _oproj(rows, h, md, 5,w_o_b[lb], [r[0].reshape(m, dk) for r in res],
                           [r[1].reshape(m, dk) for r in res])
            else:
                cb_kt, cb_vt = (jnp.transpose(c, (0, 2, 3, 1)).reshape(n_seq, dk, c.shape[1]) for c in past[3:5])
                o = _dil_sample([qi.reshape(shp) for qi in qg], b_k.reshape(shp), b_v.reshape(shp), cb_kt, cb_vt)
                h = _oproj(rows, h, md, 5,w_o_b[lb], [o.reshape(m, dk)])
        h = _ffn(rows, h, md, (6, 7, 8), norm_g[l, 2], w_ffn_gate, w_ffn_up, w_ffn_down, l, 1,
                 final_g=final_norm_g if l == depth - 1 else None)
    return h.reshape(n_seq, t, d), jnp.stack(a_k), jnp.stack(a_v), b_k_out, b_v_out


def kernel(x_prompt, x_sample, cache_a_k, cache_a_v, cache_b_k, cache_b_v, page_table, c_prompt, c_sample, norm_g, w_mod, b_mod, w_ffn_gate, w_ffn_up, w_ffn_down, w_qkv_a, w_o_a, kv_norm_g, w_kv_mod, b_kv_mod, w_kv_b, w_q_b, w_o_b, final_norm_g):
    bp, t, d = x_prompt.shape
    bs, t_dec, _ = x_sample.shape
    past_len = page_table.shape[1] * cache_a_k.shape[2]
    assert past_len % MOBA_BLOCK == 0 and t_dec <= MOBA_BLOCK

    n_c = bp + bs
    c_all = _pad_rows(jnp.concatenate([c_prompt, c_sample], axis=0), -(-n_c // 8) * 8)
    mods = _modulation(c_all, w_mod, b_mod)
    kv_mod = _modulation(c_all, w_kv_mod[None], b_kv_mod[None])[0]

    bf = lambda w: w.astype(BF16)
    weights = (norm_g, *_ffn_weights(w_ffn_gate, w_ffn_up, w_ffn_down), bf(w_qkv_a), bf(w_o_a),
               kv_norm_g, bf(w_kv_b), bf(w_q_b), bf(w_o_b), final_norm_g)
    y_p, ak_p, av_p, bk_p, bv_p = _trunk(
        x_prompt, mods[:, :bp], kv_mod[:bp], jnp.arange(t, dtype=jnp.int32), None, weights)
    y_s, ak_s, av_s, bk_s, bv_s = _trunk(
        x_sample, mods[:, bp:n_c], kv_mod[bp:n_c], past_len + jnp.arange(t_dec, dtype=jnp.int32),
        (cache_a_k, cache_a_v, page_table, cache_b_k, cache_b_v), weights)

    keep = min(max(w for w, _ in B_CONFIGS), t)
    return (y_p, y_s, ak_p, av_p, ak_s, av_s, bk_p[:, -keep:], bv_p[:, -keep:], bk_s, bv_s)
```

```python
import functools

import jax
import jax.numpy as jnp
from jax import lax
from jax.experimental import pallas as pl
from jax.experimental.pallas import tpu as pltpu

HEAD_DIM = 64
ROT_DIM = HEAD_DIM // 4
ROPE_THETA = 500000.0
MOBA_BLOCK = 256
MOBA_TOPK = 3
B_CONFIGS = ((128, 1), (512, 4), (2048, 16))
EPS = 1e-6
NEG = -1e30
LOG2_E = 1.4426950408889634

LANES = 128
HEADS_PER_TILE = LANES // HEAD_DIM
VMEM_LIMIT = 56 * 1024 * 1024

F32 = jnp.float32
BF16 = jnp.bfloat16


def _params(*sem):
    return pltpu.CompilerParams(dimension_semantics=sem, vmem_limit_bytes=VMEM_LIMIT)


def _dot(a, b):
    return jnp.dot(a, b, preferred_element_type=F32)


def _dot_nt(a, b):
    return lax.dot_general(a, b, (((1,), (1,)), ((), ())), preferred_element_type=F32)


def _silu(x):
    return x * jax.nn.sigmoid(x)


def _rmsnorm(x, g):
    return x * lax.rsqrt(jnp.mean(x * x, axis=-1, keepdims=True) + EPS) * g


def _rms_mod(x, g, shift, scale):
    return _rmsnorm(x, g) * (1.0 + scale) + shift


def _head_lane_mask(shape, rows_per_head):
    row = lax.broadcasted_iota(jnp.int32, shape, 0)
    lane = lax.broadcasted_iota(jnp.int32, shape, 1)
    return (lane // HEAD_DIM) == (row // rows_per_head)


class _Rows:
    def __init__(self, m, tm, rows_per_seq):
        self.m, self.tm = m, tm
        self.per_row = rows_per_seq < tm
        self.rows_per_seq = rows_per_seq
        self.tiles_per_seq = max(rows_per_seq // tm, 1)

    def prep(self, vecs):
        if self.per_row:
            return jnp.repeat(vecs, self.rows_per_seq, axis=0)
        return vecs[:, None, :]

    def mod_spec(self, d, col):
        if self.per_row:
            return pl.BlockSpec((self.tm, d), lambda i, j: (i, col))
        tps = self.tiles_per_seq
        return pl.BlockSpec((None, 1, d), lambda i, j: (i // tps, 0, col))

    def pos_spec(self, table_rows):
        n_pos_tiles = table_rows // self.tm
        return pl.BlockSpec((self.tm, LANES), lambda i, j: (i % n_pos_tiles, 0))


def _mod_kernel(c_ref, w_ref, b_ref, o_ref):
    cs = _silu(c_ref[...]).astype(BF16)
    o_ref[...] = _dot(cs, w_ref[...].astype(BF16)) + b_ref[...]


def _modulation(c, w, b):
    mc, d = c.shape
    nl, _, n = w.shape
    tn = d
    return pl.pallas_call(
        _mod_kernel,
        grid=(nl, n // tn),
        in_specs=[pl.BlockSpec((mc, d), lambda l, j: (0, 0)),
                  pl.BlockSpec((None, d, tn), lambda l, j: (l, 0, j)),
                  pl.BlockSpec((None, 1, tn), lambda l, j: (l, 0, j))],
        out_specs=pl.BlockSpec((None, mc, tn), lambda l, j: (l, 0, j)),
        out_shape=jax.ShapeDtypeStruct((nl, mc, n), F32),
        compiler_params=_params("arbitrary", "arbitrary"),
        name="modulation",
    )(c, w, b.reshape(nl, 1, n))


def _ffn_kernel(h_ref, sh_ref, sc_ref, gt_ref, g_ref, wg_ref, wu_ref, wd_ref, *rest, final):
    if final:
        fg_ref, o_ref, u_scr, act_scr = rest
    else:
        o_ref, u_scr, act_scr = rest
    d, ff = wg_ref.shape
    tf = 256 if ff % 256 == 0 else LANES
    to = min(512, d)
    u_scr[...] = _rms_mod(h_ref[...], g_ref[...], sh_ref[...], sc_ref[...]).astype(BF16)
    for c in range(ff // tf):
        sl = slice(c * tf, (c + 1) * tf)
        u = u_scr[...]
        a = _dot(u, wg_ref[:, sl])
        b = _dot(u, wu_ref[:, sl])
        act_scr[:, sl] = (_silu(a) * b).astype(BF16)
    for c in range(d // to):
        sl = slice(c * to, (c + 1) * to)
        down = _dot(act_scr[...], wd_ref[:, sl])
        o_ref[:, sl] = h_ref[:, sl] + 0.5 * gt_ref[:, sl] * down
    if final:
        o_ref[...] = _rmsnorm(o_ref[...], fg_ref[...])


def _ffn(rows, h, mod, cols, g, wg, wu, wd, l, s, final_g=None):
    m, d = h.shape
    ff = wg.shape[-1]
    tm = rows.tm
    final = final_g is not None
    resident = lambda shape: pl.BlockSpec((None, None) + shape, lambda i, f: (l, s, 0, 0),
                                          pipeline_mode=pl.Buffered(1))
    in_specs = [pl.BlockSpec((tm, d), lambda i, f: (i, 0)),
                rows.mod_spec(d, cols[0]), rows.mod_spec(d, cols[1]), rows.mod_spec(d, cols[2]),
                pl.BlockSpec((1, d), lambda i, f: (0, 0)),
                resident((d, ff)), resident((d, ff)), resident((ff, d))]
    args = [h, mod, mod, mod, g.reshape(1, d), wg, wu, wd]
    if final:
        in_specs.append(pl.BlockSpec((1, d), lambda i, f: (0, 0)))
        args.append(final_g.reshape(1, d))
    return pl.pallas_call(
        functools.partial(_ffn_kernel, final=final),
        grid=(m // tm, 1),
        in_specs=in_specs,
        out_specs=pl.BlockSpec((tm, d), lambda i, f: (i, 0)),
        out_shape=jax.ShapeDtypeStruct((m, d), F32),
        scratch_shapes=[pltpu.VMEM((tm, d), BF16), pltpu.VMEM((tm, ff), BF16)],
        compiler_params=_params("arbitrary", "arbitrary"),
        name="ffn",
    )(*args)


def _rope_tables(pos):
    half = ROT_DIM // 2
    n = pos.shape[0]
    freq = ROPE_THETA ** (-jnp.arange(half, dtype=F32) / half)
    ang = pos.astype(F32)[:, None] * freq[None, :]
    cos, sin = jnp.cos(ang), jnp.sin(ang)
    rest = HEAD_DIM - ROT_DIM
    ct = jnp.concatenate([cos, cos, jnp.ones((n, rest), F32)], axis=1)
    s_lo = jnp.concatenate([-sin, jnp.zeros((n, half + rest), F32)], axis=1)
    s_hi = jnp.concatenate([jnp.zeros((n, half), F32), sin, jnp.zeros((n, rest), F32)], axis=1)
    tile = lambda t: jnp.tile(t, (1, HEADS_PER_TILE))
    return (tile(ct), tile(s_hi), tile(s_lo)), (cos.T, sin.T)


def _rope_tile(x, ct, s_hi, s_lo):
    half = ROT_DIM // 2
    return x * ct + pltpu.roll(x, half, 1) * s_hi + pltpu.roll(x, LANES - half, 1) * s_lo


def _rope_t(x_t, cos_t, sin_t):
    half = ROT_DIM // 2
    pieces = []
    for base in range(0, x_t.shape[0], HEAD_DIM):
        x1, x2 = x_t[base:base + half], x_t[base + half:base + ROT_DIM]
        pieces += [x1 * cos_t - x2 * sin_t, x2 * cos_t + x1 * sin_t, x_t[base + ROT_DIM:base + HEAD_DIM]]
    return jnp.concatenate(pieces, axis=0)


_MEAN_GROUP = 8


def _proj_kernel(h_ref, sh_ref, sc_ref, g_ref, w_ref, ct_ref, shi_ref, slo_ref, cos_t_ref, sin_t_ref,
                 *outs, parts):
    i = pl.program_id(0)
    u = _rms_mod(h_ref[...], g_ref[...], sh_ref[...], sc_ref[...]).astype(BF16)
    tm = u.shape[0]
    n_p = w_ref.shape[1] // len(parts)
    blocks = tm // MOBA_BLOCK
    n_out = 0
    for p, (rope, forms) in enumerate(parts):
        part_outs = outs[n_out:n_out + len(forms)]
        n_out += len(forms)
        y = _dot(u, w_ref[:, p * n_p:(p + 1) * n_p])
        val = val_t = None
        if any(f.startswith("t_") for f in forms):
            val_t = y.T
            if rope:
                val_t = _rope_t(val_t, cos_t_ref[...], sin_t_ref[...])
            if any(not f.startswith("t_") for f in forms):
                val = val_t.T
        elif rope:
            ct, s_hi, s_lo = ct_ref[...], shi_ref[...], slo_ref[...]
            val = jnp.concatenate([_rope_tile(y[:, c * LANES:(c + 1) * LANES], ct, s_hi, s_lo)
                                   for c in range(n_p // LANES)], axis=1)
        else:
            val = y
        for form, out in zip(forms, part_outs):
            if form == "f32":
                out[...] = val
            elif form == "bf16":
                out[...] = val.astype(BF16)
            elif form == "t_f32":
                out[...] = val_t
            elif form == "t_bf16":
                for c in range(blocks):
                    out[c] = val_t[:, c * MOBA_BLOCK:(c + 1) * MOBA_BLOCK].astype(BF16)
            elif form == "block_mean":
                base = (i % (_MEAN_GROUP // blocks)) * blocks
                for c in range(blocks):
                    out[pl.ds(base + c, 1), :] = jnp.mean(
                        val[c * MOBA_BLOCK:(c + 1) * MOBA_BLOCK], axis=0, keepdims=True)


def _proj(rows, h, mod, cols, g, w, tables, parts):
    m, d = h.shape
    n_p = w.shape[1] // len(parts)
    tm, tps, t = rows.tm, rows.tiles_per_seq, rows.rows_per_seq
    n_seq = m // t
    out_specs, out_shape = [], []
    for _, forms in parts:
        for form in forms:
            if form in ("f32", "bf16"):
                out_specs.append(pl.BlockSpec((tm, n_p), lambda i, j: (i, 0)))
                out_shape.append(jax.ShapeDtypeStruct((m, n_p), F32 if form == "f32" else BF16))
                continue
            assert not rows.per_row and tm % MOBA_BLOCK == 0
            if form == "t_f32":
                out_specs.append(pl.BlockSpec((None, n_p, tm), lambda i, j: (i // tps, 0, i % tps)))
                out_shape.append(jax.ShapeDtypeStruct((n_seq, n_p, t), F32))
            elif form == "t_bf16":
                out_specs.append(pl.BlockSpec((None, tm // MOBA_BLOCK, n_p, MOBA_BLOCK),
                                              lambda i, j: (i // tps, i % tps, 0, 0)))
                out_shape.append(jax.ShapeDtypeStruct((n_seq, t // MOBA_BLOCK, n_p, MOBA_BLOCK), BF16))
            else:
                steps = _MEAN_GROUP * MOBA_BLOCK // tm
                assert form == "block_mean" and tps % steps == 0
                out_specs.append(pl.BlockSpec((None, _MEAN_GROUP, n_p),
                                              lambda i, j: (i // tps, (i % tps) // steps, 0)))
                out_shape.append(jax.ShapeDtypeStruct((n_seq, t // MOBA_BLOCK, n_p), F32))
    lane_tables, t_tables = tables
    pos_spec = rows.pos_spec(lane_tables[0].shape[0])
    n_pos_tiles = lane_tables[0].shape[0] // tm
    pos_t_spec = pl.BlockSpec((ROT_DIM // 2, tm), lambda i, j: (0, i % n_pos_tiles))
    return pl.pallas_call(
        functools.partial(_proj_kernel, parts=parts),
        grid=(m // tm, 1),
        in_specs=[pl.BlockSpec((tm, d), lambda i, j: (i, 0)),
                  rows.mod_spec(d, cols[0]), rows.mod_spec(d, cols[1]),
                  pl.BlockSpec((1, d), lambda i, j: (0, 0)),
                  pl.BlockSpec(w.shape, lambda i, j: (0, 0)),
                  pos_spec, pos_spec, pos_spec, pos_t_spec, pos_t_spec],
        out_specs=out_specs,
        out_shape=out_shape,
        compiler_params=_params("arbitrary", "arbitrary"),
        name="proj",
    )(h, mod, mod, g.reshape(1, d), w, *lane_tables, *t_tables)


def _oproj_kernel(h_ref, gt_ref, w_ref, *rest, n_groups):
    o_ref = rest[-1]
    if n_groups == 0:
        o = rest[0][...]
    else:
        os_, lses = rest[:n_groups], rest[n_groups:2 * n_groups]
        lse = [r[...] for r in lses]
        mx = functools.reduce(jnp.maximum, lse)
        e = [jnp.exp(x - mx) for x in lse]
        den = functools.reduce(lambda a, b: a + b, e)
        o = functools.reduce(lambda a, b: a + b, [(ei / den) * r[...] for ei, r in zip(e, os_)])
    o_ref[...] = h_ref[...] + gt_ref[...] * _dot(o.astype(BF16), w_ref[...])


def _oproj(rows, h, mod, col, w, o_list, lse_list=()):
    m, d = h.shape
    k = w.shape[0]
    tm = rows.tm
    xs = list(o_list) + list(lse_list)
    return pl.pallas_call(
        functools.partial(_oproj_kernel, n_groups=len(lse_list)),
        grid=(m // tm, 1),
        in_specs=[pl.BlockSpec((tm, d), lambda i, j: (i, 0)),
                  rows.mod_spec(d, col),
                  pl.BlockSpec((k, d), lambda i, j: (0, 0))]
                 + [pl.BlockSpec((tm, k), lambda i, j: (i, 0))] * len(xs),
        out_specs=pl.BlockSpec((tm, d), lambda i, j: (i, 0)),
        out_shape=jax.ShapeDtypeStruct((m, d), F32),
        compiler_params=_params("arbitrary", "arbitrary"),
        name="oproj",
    )(h, mod, w, *xs)


def _split_bf16(x):
    hi = x.astype(BF16)
    lo = (x - hi.astype(F32)).astype(BF16)
    return hi, lo


def _top_blocks_t(gate, n_past):
    blk_id = lax.broadcasted_iota(jnp.int32, gate.shape, 0).astype(F32)
    n_past = n_past.astype(F32)
    g = jnp.where(blk_id < n_past, gate, NEG)
    sel = jnp.zeros(gate.shape, jnp.bool_)
    for _ in range(MOBA_TOPK):
        mx = jnp.max(g, axis=0, keepdims=True)
        idx = jnp.min(jnp.where(g == mx, blk_id, float(gate.shape[0])), axis=0, keepdims=True)
        pick = blk_id == idx
        sel = sel | pick
        g = jnp.where(pick, -jnp.inf, g)
    return sel & (blk_id < n_past)


def _moba_prompt_kernel(q_ref, k_ref, vt_ref, km_ref, o_ref,
                        qt_scr, sel_scr, m_scr, l_scr, acc_scr, s_own, s_even, s_odd):
    blk = MOBA_BLOCK
    n_h, width = qt_scr.shape[0], qt_scr.shape[1]
    nb = vt_ref.shape[0]
    i = pl.program_id(2)
    heads = range(n_h)
    hs = [slice(hh * HEAD_DIM, (hh + 1) * HEAD_DIM) for hh in heads]

    def scores_into(dst, j):
        jj = jnp.minimum(j, nb - 1)
        kb = k_ref[pl.ds(pl.multiple_of(jj * blk, blk), blk), :]
        for hh in heads:
            dst[hh] = _dot(kb, qt_scr[hh])

    def consume(src, j, own):
        jj = jnp.minimum(j, nb - 1)
        vtb = vt_ref[jj]
        ps, alphas, chosen = [], [], []
        for hh in heads:
            s = src[hh]
            if own:
                key_id = lax.broadcasted_iota(jnp.int32, (blk, blk), 0)
                qry_id = lax.broadcasted_iota(jnp.int32, (blk, blk), 1)
                s = jnp.where(key_id <= qry_id, s, NEG)
                ch = jnp.full((1, blk), True)
            else:
                ch = (sel_scr[hh, pl.ds(jj, 1), :] > 0.5) & (j < i)
            m_old = m_scr[hh]
            m_all = jnp.maximum(m_old, jnp.max(s, axis=0, keepdims=True))
            p = jnp.exp2(s - m_all)
            m_new = jnp.where(ch, m_all, m_old)
            alpha = jnp.exp2(m_old - m_new)
            l_scr[hh] = alpha * l_scr[hh] + jnp.where(ch, jnp.sum(p, axis=0, keepdims=True), 0.0)
            m_scr[hh] = m_new
            ps.append(p.astype(BF16))
            alphas.append(alpha)
            chosen.append(ch)
        pvs = [_dot(vtb[hs[hh], :], ps[hh]) for hh in heads]
        for hh in heads:
            acc_scr[hh] = alphas[hh] * acc_scr[hh] + jnp.where(chosen[hh], pvs[hh], 0.0)

    qt = q_ref[...]
    feat = lax.broadcasted_iota(jnp.int32, (width, blk), 0)
    km_hi, km_lo = _split_bf16(km_ref[...])
    for hh in heads:
        qth = jnp.where((feat // HEAD_DIM) == hh, qt, 0.0)
        qt_scr[hh] = (qth * (HEAD_DIM ** -0.5 * LOG2_E)).astype(BF16)
        q_hi, q_lo = _split_bf16(qth)
        gate = _dot(km_hi, q_hi) + _dot(km_lo, q_hi) + _dot(km_hi, q_lo)
        sel_scr[hh] = _top_blocks_t(gate, i).astype(F32)
        m_scr[hh] = jnp.full((1, blk), NEG, F32)
        l_scr[hh] = jnp.zeros((1, blk), F32)
        acc_scr[hh] = jnp.zeros((HEAD_DIM, blk), F32)
    scores_into(s_own, i)
    scores_into(s_even, 0)
    consume(s_own, i, own=True)

    def two_blocks(c, carry):
        j = 2 * c
        scores_into(s_odd, j + 1)
        consume(s_even, j, own=False)
        scores_into(s_even, j + 2)
        consume(s_odd, j + 1, own=False)
        return carry

    lax.fori_loop(0, (i + 1) // 2, two_blocks, 0)
    ot = jnp.concatenate([acc_scr[hh] / l_scr[hh] for hh in heads], axis=0)
    o_ref[...] = ot.T


def _moba_prompt(q_t, k_bf, vt_bf, km):
    b, d, t = q_t.shape
    blk = MOBA_BLOCK
    nb = t // blk
    width = min(d, 2 * LANES)
    n_h = width // HEAD_DIM
    assert t % blk == 0 and nb % 8 == 0 and d % width == 0
    tile_spec = pl.BlockSpec((None, blk, width), lambda bi, hg, i: (bi, i, hg))
    return pl.pallas_call(
        _moba_prompt_kernel,
        grid=(b, d // width, nb),
        in_specs=[pl.BlockSpec((None, width, blk), lambda bi, hg, i: (bi, hg, i)),
                  pl.BlockSpec((None, t, width), lambda bi, hg, i: (bi, 0, hg)),
                  pl.BlockSpec((None, nb, width, blk), lambda bi, hg, i: (bi, 0, hg, 0)),
                  pl.BlockSpec((None, nb, width), lambda bi, hg, i: (bi, 0, hg))],
        out_specs=tile_spec,
        out_shape=jax.ShapeDtypeStruct((b, t, d), F32),
        scratch_shapes=[pltpu.VMEM((n_h, width, blk), BF16),
                        pltpu.VMEM((n_h, nb, blk), F32),
                        pltpu.VMEM((n_h, 1, blk), F32),
                        pltpu.VMEM((n_h, 1, blk), F32),
                        pltpu.VMEM((n_h, HEAD_DIM, blk), F32)]
                       + [pltpu.VMEM((n_h, blk, blk), F32)] * 3,
        compiler_params=_params("arbitrary", "arbitrary", "arbitrary"),
        name="moba_prompt",
    )(q_t, k_bf, vt_bf, km)


def _pad_rows(x, n):
    return jnp.concatenate([x, jnp.zeros((n - x.shape[0], x.shape[1]), x.dtype)], axis=0)


def _moba_sample_kernel(pt_ref, q_ref, kn_ref, vn_ref, *rest, n_pages, page, n_heads):
    k_pages, v_pages = rest[:n_pages], rest[n_pages:2 * n_pages]
    o_ref, s_scr = rest[2 * n_pages], rest[2 * n_pages + 1]
    t_dec, d = q_ref.shape
    r = n_heads * t_dec
    ppb = MOBA_BLOCK // page
    n_past = n_pages // ppb
    own = lax.broadcasted_iota(jnp.int32, (r, d), 0) // t_dec
    head_mask = (lax.broadcasted_iota(jnp.int32, (r, d), 1) // HEAD_DIM) == own
    qrows = jnp.where(head_mask, jnp.concatenate([q_ref[...]] * n_heads, axis=0), 0.0)
    qs = (qrows * HEAD_DIM ** -0.5).astype(BF16)

    gates = []
    for j in range(n_past):
        for p in range(j * ppb, (j + 1) * ppb):
            s_scr[:, p * page:(p + 1) * page] = _dot(qs, k_pages[p][...].astype(BF16))
        gates.append(jnp.sum(s_scr[:, j * MOBA_BLOCK:(j + 1) * MOBA_BLOCK], axis=1, keepdims=True))
    for j in range(n_past):
        rank = jnp.zeros((r, 1), jnp.int32)
        for j2 in range(n_past):
            if j2 != j:
                ahead = (gates[j2] > gates[j]) | ((gates[j2] == gates[j]) & (j2 < j))
                rank = rank + ahead.astype(jnp.int32)
        chosen = rank < MOBA_TOPK
        sl = slice(j * MOBA_BLOCK, (j + 1) * MOBA_BLOCK)
        s_scr[:, sl] = jnp.where(chosen, s_scr[:, sl], NEG)
    t_row = lax.broadcasted_iota(jnp.int32, (r, page), 0) % t_dec
    t_col = lax.broadcasted_iota(jnp.int32, (r, page), 1)
    s_own = _dot_nt(qs, _pad_rows(kn_ref[...], page).astype(BF16))
    s_scr[:, n_pages * page:] = jnp.where(t_col <= t_row, s_own, NEG)

    s = s_scr[...]
    m = jnp.max(s, axis=1, keepdims=True)
    p_all = jnp.exp(s - m)
    l = jnp.sum(p_all, axis=1, keepdims=True)
    acc = _dot(p_all[:, n_pages * page:].astype(BF16), _pad_rows(vn_ref[...], page).astype(BF16))
    for p in range(n_pages):
        acc = acc + _dot_nt(p_all[:, p * page:(p + 1) * page].astype(BF16), v_pages[p][...].astype(BF16))
    o_full = jnp.where(head_mask, acc / l, 0.0)
    out = o_full[0:t_dec]
    for h in range(1, n_heads):
        out = out + o_full[h * t_dec:(h + 1) * t_dec]
    o_ref[...] = out


def _moba_sample(q, k_new, v_new, cache_kt, cache_vt, layer, page_table):
    b, t_dec, d = q.shape
    n_pages = page_table.shape[1]
    page = cache_kt.shape[3]
    n_heads = d // HEAD_DIM
    assert MOBA_BLOCK % page == 0 and (n_pages * page) % MOBA_BLOCK == 0 and t_dec <= page
    tok_spec = pl.BlockSpec((None, t_dec, d), lambda i, pt: (i, 0, 0))
    page_specs = [pl.BlockSpec((None, None, d, page),
                               functools.partial(lambda i, pt, p: (layer, pt[i, p], 0, 0), p=p))
                  for p in range(n_pages)]
    grid_spec = pltpu.PrefetchScalarGridSpec(
        num_scalar_prefetch=1,
        grid=(b,),
        in_specs=[tok_spec, tok_spec, tok_spec] + page_specs + page_specs,
        out_specs=tok_spec,
        scratch_shapes=[pltpu.VMEM((n_heads * t_dec, (n_pages + 1) * page), F32)],
    )
    return pl.pallas_call(
        functools.partial(_moba_sample_kernel, n_pages=n_pages, page=page, n_heads=n_heads),
        grid_spec=grid_spec,
        out_shape=jax.ShapeDtypeStruct((b, t_dec, d), F32),
        compiler_params=_params("arbitrary"),
        name="moba_sample",
    )(page_table, q, k_new, v_new, *([cache_kt] * n_pages), *([cache_vt] * n_pages))


def _dil_prompt_kernel(q_ref, k_ref, v_ref, o_ref, lse_ref, *, dil, tq, sub, width, win):
    length = k_ref.shape[0] // dil
    base = pl.program_id(2) * (sub * tq)
    lane = lax.broadcasted_iota(jnp.int32, (tq, LANES), 1)
    row_col = (lax.broadcasted_iota(jnp.int32, (tq, width), 0)
               - lax.broadcasted_iota(jnp.int32, (tq, width), 1))
    heads = range(HEADS_PER_TILE)

    def rows(start, n):
        return pl.ds(start, n) if dil == 1 else pl.ds(start, n, stride=dil)

    def attend(items):
        qs, kws, vws, valids = [], [], [], []
        for r, u in items:
            q0 = base + u * tq
            ks = jnp.clip(q0 - win, 0, length - width)
            q = q_ref[rows(u * tq * dil + r, tq), :]
            qs.append([(jnp.where((lane // HEAD_DIM) == hh, q, 0.0) * HEAD_DIM ** -0.5).astype(BF16)
                       for hh in heads])
            kws.append(k_ref[rows(ks * dil + r, width), :].astype(BF16))
            vws.append(v_ref[rows(ks * dil + r, width), :].astype(BF16))
            delta = q0 - ks + row_col
            valids.append((delta >= 0) & (delta <= win))
        n = range(len(items))
        scores = [[_dot_nt(qs[it][hh], kws[it]) for hh in heads] for it in n]
        ps, ls, ms = [], [], []
        for it in n:
            for hh in heads:
                s = jnp.where(valids[it], scores[it][hh], NEG)
                m = jnp.max(s, axis=1, keepdims=True)
                p = jnp.exp(s - m)
                ls.append(jnp.sum(p, axis=1, keepdims=True))
                ms.append(m)
                ps.append(p.astype(BF16))
        pvs = [_dot(ps[it * len(heads) + hh], vws[it]) for it in n for hh in heads]
        for it, (r, u) in enumerate(items):
            c0, c1 = it * len(heads), it * len(heads) + 1
            o_ref[rows(u * tq * dil + r, tq), :] = jnp.where(lane < HEAD_DIM, pvs[c0] / ls[c0], pvs[c1] / ls[c1])
            lse_ref[rows(u * tq * dil + r, tq), :] = jnp.where(
                lane < HEAD_DIM, jnp.broadcast_to(ms[c0] + jnp.log(ls[c0]), (tq, LANES)),
                jnp.broadcast_to(ms[c1] + jnp.log(ls[c1]), (tq, LANES)))

    if dil == 1:
        attend([(0, u) for u in range(sub)])
    else:
        def two_classes(c, carry):
            attend([(2 * c, u) for u in range(sub)] + [(2 * c + 1, u) for u in range(sub)])
            return carry
        lax.fori_loop(0, dil // 2, two_classes, 0)


def _dil_prompt(q, k, v, window, dil):
    b, t, dk = q.shape
    assert t % dil == 0 and HEADS_PER_TILE == 2 and (dil == 1 or dil % 2 == 0)
    length = t // dil
    win = window // dil
    tq = min(256, length)
    sub = 2 if dil == 1 and length % (2 * tq) == 0 else 1
    width = min(tq + win, length)
    assert length % (sub * tq) == 0
    tile_spec = pl.BlockSpec((None, sub * tq * dil, LANES), lambda bi, h, i: (bi, i, h))
    seq_spec = pl.BlockSpec((None, t, LANES), lambda bi, h, i: (bi, 0, h))
    return pl.pallas_call(
        functools.partial(_dil_prompt_kernel, dil=dil, tq=tq, sub=sub, width=width, win=win),
        grid=(b, dk // LANES, length // (sub * tq)),
        in_specs=[tile_spec, seq_spec, seq_spec],
        out_specs=[tile_spec, tile_spec],
        out_shape=[jax.ShapeDtypeStruct((b, t, dk), F32)] * 2,
        compiler_params=_params("arbitrary", "arbitrary", "arbitrary"),
        name="dilated_prompt",
    )(q, k, v)


def _dil_sample_kernel(q0_ref, q1_ref, q2_ref, kn_ref, vn_ref, kc_ref, vc_ref, o_ref, s_scr, p_scr, *, n_heads):
    q_refs = (q0_ref, q1_ref, q2_ref)
    t_dec, dk = kn_ref.shape
    past = kc_ref.shape[1]
    pad = LANES
    rg = n_heads * t_dec
    head_mask = _head_lane_mask((rg, dk), t_dec)
    qrows = jnp.concatenate(
        [jnp.where(head_mask, jnp.concatenate([qr[...]] * n_heads, axis=0), 0.0) for qr in q_refs], axis=0)
    qs = (qrows * HEAD_DIM ** -0.5).astype(BF16)
    s_scr[:, :past] = _dot(qs, kc_ref[...].astype(BF16))
    s_scr[:, past:] = _dot_nt(qs, _pad_rows(kn_ref[...], pad).astype(BF16))

    t_row = lax.broadcasted_iota(jnp.int32, (rg, past + pad), 0) % t_dec
    col = lax.broadcasted_iota(jnp.int32, (rg, past + pad), 1)
    delta = past + t_row - col
    in_range = col < past + t_dec
    l_g, lse_g = [], []
    for g, (window, dil) in enumerate(B_CONFIGS):
        rs = slice(g * rg, (g + 1) * rg)
        assert dil & (dil - 1) == 0
        valid = in_range & (delta >= 0) & (delta <= window) & ((delta & (dil - 1)) == 0)
        s = jnp.where(valid, s_scr[rs, :], NEG)
        m = jnp.max(s, axis=1, keepdims=True)
        p = jnp.exp(s - m)
        l = jnp.sum(p, axis=1, keepdims=True)
        p_scr[rs, :] = p.astype(BF16)
        l_g.append(l)
        lse_g.append(m + jnp.log(l))
    acc = (_dot_nt(p_scr[:, :past], vc_ref[...].astype(BF16))
           + _dot(p_scr[:, past:], _pad_rows(vn_ref[...], pad).astype(BF16)))
    mx = functools.reduce(jnp.maximum, lse_g)
    e = [jnp.exp(x - mx) for x in lse_g]
    den = functools.reduce(lambda a, b: a + b, e)
    mixed = jnp.zeros((rg, dk), F32)
    for g in range(len(B_CONFIGS)):
        mixed = mixed + (e[g] / den) * (acc[g * rg:(g + 1) * rg] / l_g[g])
    mixed = jnp.where(head_mask, mixed, 0.0)
    out = mixed[0:t_dec]
    for h in range(1, n_heads):
        out = out + mixed[h * t_dec:(h + 1) * t_dec]
    o_ref[...] = out


def _dil_sample(q_groups, k_new, v_new, cache_kt, cache_vt):
    b, t_dec, dk = k_new.shape
    past = cache_kt.shape[2]
    n_heads = dk // HEAD_DIM
    n_rows = len(B_CONFIGS) * n_heads * t_dec
    tok_spec = pl.BlockSpec((None, t_dec, dk), lambda i: (i, 0, 0))
    cache_spec = pl.BlockSpec((None, dk, past), lambda i: (i, 0, 0))
    return pl.pallas_call(
        functools.partial(_dil_sample_kernel, n_heads=n_heads),
        grid=(b,),
        in_specs=[tok_spec] * 5 + [cache_spec] * 2,
        out_specs=tok_spec,
        out_shape=jax.ShapeDtypeStruct((b, t_dec, dk), F32),
        scratch_shapes=[pltpu.VMEM((n_rows, past + LANES), F32),
                        pltpu.VMEM((n_rows, past + LANES), BF16)],
        compiler_params=_params("arbitrary"),
        name="dilated_sample",
    )(*q_groups, k_new, v_new, cache_kt, cache_vt)


def _trunk(x, mods, kv_mod, pos, past, weights):
    (norm_g, w_ffn_gate, w_ffn_up, w_ffn_down, w_qkv_a, w_o_a,
     kv_norm_g, w_kv_b, w_q_b, w_o_b, final_norm_g) = weights
    n_seq, t, d = x.shape
    m = n_seq * t
    depth = norm_g.shape[0]
    n_a = w_qkv_a.shape[0]
    dk = w_kv_b.shape[1] // 2
    is_prompt = past is None
    short = t < 256
    rows = _Rows(m, min(1024 if is_prompt else 512, m if short else t), t)
    rows_p = _Rows(m, min(512, m if short else t), t)
    tables = _rope_tables(pos)
    if rows.per_row:
        tables = (tuple(jnp.tile(tb, (n_seq, 1)) for tb in tables[0]),
                  tuple(jnp.tile(tb, (1, n_seq)) for tb in tables[1]))

    def heads_last(x_t):
        return jnp.transpose(x_t.reshape(n_seq, -1, HEAD_DIM, x_t.shape[-1]), (0, 3, 1, 2))

    h = x.reshape(m, d)
    a_k, a_v = [], []
    b_k = b_v = b_k_out = b_v_out = None
    for l in range(depth):
        if l == n_a:
            kvm = rows.prep(kv_mod)
            if is_prompt:
                b_k, b_kt, b_v, b_vt = _proj(rows_p, h, kvm, (0, 1), kv_norm_g, w_kv_b, tables,
                                             ((True, ("f32", "t_f32")), (False, ("f32", "t_f32"))))
                b_k_out, b_v_out = heads_last(b_kt), heads_last(b_vt)
            else:
                b_k, b_v = _proj(rows_p, h, kvm, (0, 1), kv_norm_g, w_kv_b, tables,
                                 ((True, ("f32",)), (False, ("f32",))))
                b_k_out, b_v_out = (r.reshape(n_seq, t, -1, HEAD_DIM) for r in (b_k, b_v))
        md = rows.prep(mods[l])
        h = _ffn(rows, h, md, (0, 1, 2), norm_g[l, 0], w_ffn_gate, w_ffn_up, w_ffn_down, l, 0)
        if l < n_a:
            shp = (n_seq, t, d)
            if is_prompt:
                q_t, k_t, k_bf, km, v_t, vt_bf = _proj(
                    rows_p, h, md, (3, 4), norm_g[l, 1], w_qkv_a[l], tables,
                    ((True, ("t_f32",)), (True, ("t_f32", "bf16", "block_mean")), (False, ("t_f32", "t_bf16"))))
                o = _moba_prompt(q_t, k_bf.reshape(shp), vt_bf, km)
                a_k.append(heads_last(k_t))
                a_v.append(heads_last(v_t))
            else:
                q, k, v = _proj(rows_p, h, md, (3, 4), norm_g[l, 1], w_qkv_a[l], tables,
                                ((True, ("f32",)), (True, ("f32",)), (False, ("f32",))))
                cache_kt, cache_vt = (
                    jnp.transpose(c, (0, 1, 3, 4, 2)).reshape(c.shape[0], c.shape[1], d, c.shape[2])
                    for c in past[:2])
                o = _moba_sample(q.reshape(shp), k.reshape(shp), v.reshape(shp), cache_kt, cache_vt, l, past[2])
                a_k.append(k.reshape(n_seq, t, -1, HEAD_DIM))
                a_v.append(v.reshape(n_seq, t, -1, HEAD_DIM))
            h = _oproj(rows, h, md, 5,w_o_a[l], [o.reshape(m, d)])
        else:
            lb = l - n_a
            qg = _proj(rows_p, h, md, (3, 4), norm_g[l, 1], w_q_b[lb], tables,
                       ((True, ("f32",)),) * len(B_CONFIGS))
            shp = (n_seq, t, dk)
            if is_prompt:
                res = [_dil_prompt(qi.reshape(shp), b_k.reshape(shp), b_v.reshape(shp), w, dl)
                       for qi, (w, dl) in zip(qg, B_CONFIGS)]
                h = _oproj(rows, h, md, 5,w_o_b[lb], [r[0].reshape(m, dk) for r in res],
                           [r[1].reshape(m, dk) for r in res])
            else:
                cb_kt, cb_vt = (jnp.transpose(c, (0, 2, 3, 1)).reshape(n_seq, dk, c.shape[1]) for c in past[3:5])
                o = _dil_sample([qi.reshape(shp) for qi in qg], b_k.reshape(shp), b_v.reshape(shp), cb_kt, cb_vt)
                h = _oproj(rows, h, md, 5,w_o_b[lb], [o.reshape(m, dk)])
        h = _ffn(rows, h, md, (6, 7, 8), norm_g[l, 2], w_ffn_gate, w_ffn_up, w_ffn_down, l, 1,
                 final_g=final_norm_g if l == depth - 1 else None)
    return h.reshape(n_seq, t, d), jnp.stack(a_k), jnp.stack(a_v), b_k_out, b_v_out


def kernel(x_prompt, x_sample, cache_a_k, cache_a_v, cache_b_k, cache_b_v, page_table, c_prompt, c_sample, norm_g, w_mod, b_mod, w_ffn_gate, w_ffn_up, w_ffn_down, w_qkv_a, w_o_a, kv_norm_g, w_kv_mod, b_kv_mod, w_kv_b, w_q_b, w_o_b, final_norm_g):
    bp, t, d = x_prompt.shape
    bs, t_dec, _ = x_sample.shape
    past_len = page_table.shape[1] * cache_a_k.shape[2]
    assert past_len % MOBA_BLOCK == 0 and t_dec <= MOBA_BLOCK

    n_c = bp + bs
    c_all = _pad_rows(jnp.concatenate([c_prompt, c_sample], axis=0), -(-n_c // 8) * 8)
    mods = _modulation(c_all, w_mod, b_mod)
    kv_mod = _modulation(c_all, w_kv_mod[None], b_kv_mod[None])[0]

    bf = lambda w: w.astype(BF16)
    weights = (norm_g, bf(w_ffn_gate), bf(w_ffn_up), bf(w_ffn_down), bf(w_qkv_a), bf(w_o_a),
               kv_norm_g, bf(w_kv_b), bf(w_q_b), bf(w_o_b), final_norm_g)
    y_p, ak_p, av_p, bk_p, bv_p = _trunk(
        x_prompt, mods[:, :bp], kv_mod[:bp], jnp.arange(t, dtype=jnp.int32), None, weights)
    y_s, ak_s, av_s, bk_s, bv_s = _trunk(
        x_sample, mods[:, bp:n_c], kv_mod[bp:n_c], past_len + jnp.arange(t_dec, dtype=jnp.int32),
        (cache_a_k, cache_a_v, page_table, cache_b_k, cache_b_v), weights)

    keep = min(max(w for w, _ in B_CONFIGS), t)
    return (y_p, y_s, ak_p, av_p, ak_s, av_s, bk_p[:, -keep:], bv_p[:, -keep:], bk_s, bv_s)
```

```python
import functools

import jax
import jax.numpy as jnp
from jax import lax
from jax.experimental import pallas as pl
from jax.experimental.pallas import tpu as pltpu

HEAD_DIM = 64
ROT_DIM = HEAD_DIM // 4
ROPE_THETA = 500000.0
MOBA_BLOCK = 256
MOBA_TOPK = 3
B_CONFIGS = ((128, 1), (512, 4), (2048, 16))
EPS = 1e-6
NEG = -1e30
LOG2_E = 1.4426950408889634

LANES = 128
HEADS_PER_TILE = LANES // HEAD_DIM
VMEM_LIMIT = 56 * 1024 * 1024

F32 = jnp.float32
BF16 = jnp.bfloat16


def _params(*sem):
    return pltpu.CompilerParams(dimension_semantics=sem, vmem_limit_bytes=VMEM_LIMIT)


def _dot(a, b):
    return jnp.dot(a, b, preferred_element_type=F32)


def _dot_nt(a, b):
    return lax.dot_general(a, b, (((1,), (1,)), ((), ())), preferred_element_type=F32)


def _silu(x):
    return x * jax.nn.sigmoid(x)


def _rmsnorm(x, g):
    return x * lax.rsqrt(jnp.mean(x * x, axis=-1, keepdims=True) + EPS) * g


def _rms_mod(x, g, shift, scale):
    return _rmsnorm(x, g) * (1.0 + scale) + shift


def _head_lane_mask(shape, rows_per_head):
    row = lax.broadcasted_iota(jnp.int32, shape, 0)
    lane = lax.broadcasted_iota(jnp.int32, shape, 1)
    return (lane // HEAD_DIM) == (row // rows_per_head)


class _Rows:
    def __init__(self, m, tm, rows_per_seq):
        self.m, self.tm = m, tm
        self.per_row = rows_per_seq < tm
        self.rows_per_seq = rows_per_seq
        self.tiles_per_seq = max(rows_per_seq // tm, 1)

    def prep(self, vecs):
        if self.per_row:
            return jnp.repeat(vecs, self.rows_per_seq, axis=0)
        return vecs[:, None, :]

    def mod_spec(self, d, col):
        if self.per_row:
            return pl.BlockSpec((self.tm, d), lambda i, j: (i, col))
        tps = self.tiles_per_seq
        return pl.BlockSpec((None, 1, d), lambda i, j: (i // tps, 0, col))

    def pos_spec(self, table_rows):
        n_pos_tiles = table_rows // self.tm
        return pl.BlockSpec((self.tm, LANES), lambda i, j: (i % n_pos_tiles, 0))


def _mod_kernel(c_ref, w_ref, b_ref, o_ref):
    cs = _silu(c_ref[...]).astype(BF16)
    o_ref[...] = _dot(cs, w_ref[...].astype(BF16)) + b_ref[...]


def _modulation(c, w, b):
    mc, d = c.shape
    nl, _, n = w.shape
    tn = d
    return pl.pallas_call(
        _mod_kernel,
        grid=(nl, n // tn),
        in_specs=[pl.BlockSpec((mc, d), lambda l, j: (0, 0)),
                  pl.BlockSpec((None, d, tn), lambda l, j: (l, 0, j)),
                  pl.BlockSpec((None, 1, tn), lambda l, j: (l, 0, j))],
        out_specs=pl.BlockSpec((None, mc, tn), lambda l, j: (l, 0, j)),
        out_shape=jax.ShapeDtypeStruct((nl, mc, n), F32),
        compiler_params=_params("arbitrary", "arbitrary"),
        name="modulation",
    )(c, w, b.reshape(nl, 1, n))


def _ffn_kernel(h_ref, sh_ref, sc_ref, gt_ref, g_ref, wg_ref, wu_ref, wd_ref, *rest, final):
    if final:
        fg_ref, o_ref, u_scr, act_scr = rest
    else:
        o_ref, u_scr, act_scr = rest
    d, ff = wg_ref.shape
    tf = 256 if ff % 256 == 0 else LANES
    to = min(512, d)
    u_scr[...] = _rms_mod(h_ref[...], g_ref[...], sh_ref[...], sc_ref[...]).astype(BF16)
    for c in range(ff // tf):
        sl = slice(c * tf, (c + 1) * tf)
        u = u_scr[...]
        a = _dot(u, wg_ref[:, sl])
        b = _dot(u, wu_ref[:, sl])
        act_scr[:, sl] = (_silu(a) * b).astype(BF16)
    for c in range(d // to):
        sl = slice(c * to, (c + 1) * to)
        down = _dot(act_scr[...], wd_ref[:, sl])
        o_ref[:, sl] = h_ref[:, sl] + 0.5 * gt_ref[:, sl] * down
    if final:
        o_ref[...] = _rmsnorm(o_ref[...], fg_ref[...])


def _ffn(rows, h, mod, cols, g, wg, wu, wd, l, s, final_g=None):
    m, d = h.shape
    ff = wg.shape[-1]
    tm = rows.tm
    final = final_g is not None
    resident = lambda shape: pl.BlockSpec((None, None) + shape, lambda i, f: (l, s, 0, 0),
                                          pipeline_mode=pl.Buffered(1))
    in_specs = [pl.BlockSpec((tm, d), lambda i, f: (i, 0)),
                rows.mod_spec(d, cols[0]), rows.mod_spec(d, cols[1]), rows.mod_spec(d, cols[2]),
                pl.BlockSpec((1, d), lambda i, f: (0, 0)),
                resident((d, ff)), resident((d, ff)), resident((ff, d))]
    args = [h, mod, mod, mod, g.reshape(1, d), wg, wu, wd]
    if final:
        in_specs.append(pl.BlockSpec((1, d), lambda i, f: (0, 0)))
        args.append(final_g.reshape(1, d))
    return pl.pallas_call(
        functools.partial(_ffn_kernel, final=final),
        grid=(m // tm, 1),
        in_specs=in_specs,
        out_specs=pl.BlockSpec((tm, d), lambda i, f: (i, 0)),
        out_shape=jax.ShapeDtypeStruct((m, d), F32),
        scratch_shapes=[pltpu.VMEM((tm, d), BF16), pltpu.VMEM((tm, ff), BF16)],
        compiler_params=_params("arbitrary", "arbitrary"),
        name="ffn",
    )(*args)


def _rope_tables(pos):
    half = ROT_DIM // 2
    n = pos.shape[0]
    freq = ROPE_THETA ** (-jnp.arange(half, dtype=F32) / half)
    ang = pos.astype(F32)[:, None] * freq[None, :]
    cos, sin = jnp.cos(ang), jnp.sin(ang)
    rest = HEAD_DIM - ROT_DIM
    ct = jnp.concatenate([cos, cos, jnp.ones((n, rest), F32)], axis=1)
    s_lo = jnp.concatenate([-sin, jnp.zeros((n, half + rest), F32)], axis=1)
    s_hi = jnp.concatenate([jnp.zeros((n, half), F32), sin, jnp.zeros((n, rest), F32)], axis=1)
    tile = lambda t: jnp.tile(t, (1, HEADS_PER_TILE))
    return (tile(ct), tile(s_hi), tile(s_lo)), (cos.T, sin.T)


def _rope_tile(x, ct, s_hi, s_lo):
    half = ROT_DIM // 2
    return x * ct + pltpu.roll(x, half, 1) * s_hi + pltpu.roll(x, LANES - half, 1) * s_lo


def _rope_t(x_t, cos_t, sin_t):
    half = ROT_DIM // 2
    pieces = []
    for base in range(0, x_t.shape[0], HEAD_DIM):
        x1, x2 = x_t[base:base + half], x_t[base + half:base + ROT_DIM]
        pieces += [x1 * cos_t - x2 * sin_t, x2 * cos_t + x1 * sin_t, x_t[base + ROT_DIM:base + HEAD_DIM]]
    return jnp.concatenate(pieces, axis=0)


_MEAN_GROUP = 8


def _proj_kernel(h_ref, sh_ref, sc_ref, g_ref, w_ref, ct_ref, shi_ref, slo_ref, cos_t_ref, sin_t_ref,
                 *outs, parts):
    i = pl.program_id(0)
    u = _rms_mod(h_ref[...], g_ref[...], sh_ref[...], sc_ref[...]).astype(BF16)
    tm = u.shape[0]
    n_p = w_ref.shape[1] // len(parts)
    blocks = tm // MOBA_BLOCK
    n_out = 0
    for p, (rope, forms) in enumerate(parts):
        part_outs = outs[n_out:n_out + len(forms)]
        n_out += len(forms)
        y = _dot(u, w_ref[:, p * n_p:(p + 1) * n_p])
        val = val_t = None
        if any(f.startswith("t_") for f in forms):
            val_t = y.T
            if rope:
                val_t = _rope_t(val_t, cos_t_ref[...], sin_t_ref[...])
            if any(not f.startswith("t_") for f in forms):
                val = val_t.T
        elif rope:
            ct, s_hi, s_lo = ct_ref[...], shi_ref[...], slo_ref[...]
            val = jnp.concatenate([_rope_tile(y[:, c * LANES:(c + 1) * LANES], ct, s_hi, s_lo)
                                   for c in range(n_p // LANES)], axis=1)
        else:
            val = y
        for form, out in zip(forms, part_outs):
            if form == "f32":
                out[...] = val
            elif form == "bf16":
                out[...] = val.astype(BF16)
            elif form == "t_f32":
                out[...] = val_t
            elif form == "t_bf16":
                for c in range(blocks):
                    out[c] = val_t[:, c * MOBA_BLOCK:(c + 1) * MOBA_BLOCK].astype(BF16)
            elif form == "block_mean":
                base = (i % (_MEAN_GROUP // blocks)) * blocks
                for c in range(blocks):
                    out[pl.ds(base + c, 1), :] = jnp.mean(
                        val[c * MOBA_BLOCK:(c + 1) * MOBA_BLOCK], axis=0, keepdims=True)


def _proj(rows, h, mod, cols, g, w, tables, parts):
    m, d = h.shape
    n_p = w.shape[1] // len(parts)
    tm, tps, t = rows.tm, rows.tiles_per_seq, rows.rows_per_seq
    n_seq = m // t
    out_specs, out_shape = [], []
    for _, forms in parts:
        for form in forms:
            if form in ("f32", "bf16"):
                out_specs.append(pl.BlockSpec((tm, n_p), lambda i, j: (i, 0)))
                out_shape.append(jax.ShapeDtypeStruct((m, n_p), F32 if form == "f32" else BF16))
                continue
            assert not rows.per_row and tm % MOBA_BLOCK == 0
            if form == "t_f32":
                out_specs.append(pl.BlockSpec((None, n_p, tm), lambda i, j: (i // tps, 0, i % tps)))
                out_shape.append(jax.ShapeDtypeStruct((n_seq, n_p, t), F32))
            elif form == "t_bf16":
                out_specs.append(pl.BlockSpec((None, tm // MOBA_BLOCK, n_p, MOBA_BLOCK),
                                              lambda i, j: (i // tps, i % tps, 0, 0)))
                out_shape.append(jax.ShapeDtypeStruct((n_seq, t // MOBA_BLOCK, n_p, MOBA_BLOCK), BF16))
            else:
                steps = _MEAN_GROUP * MOBA_BLOCK // tm
                assert form == "block_mean" and tps % steps == 0
                out_specs.append(pl.BlockSpec((None, _MEAN_GROUP, n_p),
                                              lambda i, j: (i // tps, (i % tps) // steps, 0)))
                out_shape.append(jax.ShapeDtypeStruct((n_seq, t // MOBA_BLOCK, n_p), F32))
    lane_tables, t_tables = tables
    pos_spec = rows.pos_spec(lane_tables[0].shape[0])
    n_pos_tiles = lane_tables[0].shape[0] // tm
    pos_t_spec = pl.BlockSpec((ROT_DIM // 2, tm), lambda i, j: (0, i % n_pos_tiles))
    return pl.pallas_call(
        functools.partial(_proj_kernel, parts=parts),
        grid=(m // tm, 1),
        in_specs=[pl.BlockSpec((tm, d), lambda i, j: (i, 0)),
                  rows.mod_spec(d, cols[0]), rows.mod_spec(d, cols[1]),
                  pl.BlockSpec((1, d), lambda i, j: (0, 0)),
                  pl.BlockSpec(w.shape, lambda i, j: (0, 0)),
                  pos_spec, pos_spec, pos_spec, pos_t_spec, pos_t_spec],
        out_specs=out_specs,
        out_shape=out_shape,
        compiler_params=_params("arbitrary", "arbitrary"),
        name="proj",
    )(h, mod, mod, g.reshape(1, d), w, *lane_tables, *t_tables)


def _oproj_kernel(h_ref, gt_ref, w_ref, *rest, n_groups):
    o_ref = rest[-1]
    if n_groups == 0:
        o = rest[0][...]
    else:
        os_, lses = rest[:n_groups], rest[n_groups:2 * n_groups]
        lse = [r[...] for r in lses]
        mx = functools.reduce(jnp.maximum, lse)
        e = [jnp.exp(x - mx) for x in lse]
        den = functools.reduce(lambda a, b: a + b, e)
        o = functools.reduce(lambda a, b: a + b, [(ei / den) * r[...] for ei, r in zip(e, os_)])
    o_ref[...] = h_ref[...] + gt_ref[...] * _dot(o.astype(BF16), w_ref[...])


def _oproj(rows, h, mod, col, w, o_list, lse_list=()):
    m, d = h.shape
    k = w.shape[0]
    tm = rows.tm
    xs = list(o_list) + list(lse_list)
    return pl.pallas_call(
        functools.partial(_oproj_kernel, n_groups=len(lse_list)),
        grid=(m // tm, 1),
        in_specs=[pl.BlockSpec((tm, d), lambda i, j: (i, 0)),
                  rows.mod_spec(d, col),
                  pl.BlockSpec((k, d), lambda i, j: (0, 0))]
                 + [pl.BlockSpec((tm, k), lambda i, j: (i, 0))] * len(xs),
        out_specs=pl.BlockSpec((tm, d), lambda i, j: (i, 0)),
        out_shape=jax.ShapeDtypeStruct((m, d), F32),
        compiler_params=_params("arbitrary", "arbitrary"),
        name="oproj",
    )(h, mod, w, *xs)


def _split_bf16(x):
    hi = x.astype(BF16)
    lo = (x - hi.astype(F32)).astype(BF16)
    return hi, lo


def _top_blocks_t(gate, n_past):
    blk_id = lax.broadcasted_iota(jnp.int32, gate.shape, 0).astype(F32)
    n_past = n_past.astype(F32)
    g = jnp.where(blk_id < n_past, gate, NEG)
    sel = jnp.zeros(gate.shape, jnp.bool_)
    for _ in range(MOBA_TOPK):
        mx = jnp.max(g, axis=0, keepdims=True)
        idx = jnp.min(jnp.where(g == mx, blk_id, float(gate.shape[0])), axis=0, keepdims=True)
        pick = blk_id == idx
        sel = sel | pick
        g = jnp.where(pick, -jnp.inf, g)
    return sel & (blk_id < n_past)


def _moba_prompt_kernel(q_ref, k_ref, vt_ref, km_ref, o_ref,
                        qt_scr, sel_scr, m_scr, l_scr, acc_scr, s_own, s_even, s_odd):
    blk = MOBA_BLOCK
    n_h = qt_scr.shape[0]
    tiles = [slice(hh // HEADS_PER_TILE * LANES, (hh // HEADS_PER_TILE + 1) * LANES) for hh in range(n_h)]
    nb = vt_ref.shape[0]
    i = pl.program_id(2)
    heads = range(n_h)
    hs = [slice(hh * HEAD_DIM, (hh + 1) * HEAD_DIM) for hh in heads]

    def scores_into(dst, j):
        jj = jnp.minimum(j, nb - 1)
        kb = k_ref[pl.ds(pl.multiple_of(jj * blk, blk), blk), :]
        for hh in heads:
            dst[hh] = _dot(kb[:, tiles[hh]], qt_scr[hh])

    def consume(src, j, own):
        jj = jnp.minimum(j, nb - 1)
        vtb = vt_ref[jj]
        ps, alphas, chosen = [], [], []
        for hh in heads:
            s = src[hh]
            if own:
                key_id = lax.broadcasted_iota(jnp.int32, (blk, blk), 0)
                qry_id = lax.broadcasted_iota(jnp.int32, (blk, blk), 1)
                s = jnp.where(key_id <= qry_id, s, NEG)
                ch = jnp.full((1, blk), True)
            else:
                ch = (sel_scr[hh, pl.ds(jj, 1), :] > 0.5) & (j < i)
            m_old = m_scr[hh]
            m_all = jnp.maximum(m_old, jnp.max(s, axis=0, keepdims=True))
            p = jnp.exp2(s - m_all)
            m_new = jnp.where(ch, m_all, m_old)
            alpha = jnp.exp2(m_old - m_new)
            l_scr[hh] = alpha * l_scr[hh] + jnp.where(ch, jnp.sum(p, axis=0, keepdims=True), 0.0)
            m_scr[hh] = m_new
            ps.append(p.astype(BF16))
            alphas.append(alpha)
            chosen.append(ch)
        pvs = [_dot(vtb[hs[hh], :], ps[hh]) for hh in heads]
        for hh in heads:
            acc_scr[hh] = alphas[hh] * acc_scr[hh] + jnp.where(chosen[hh], pvs[hh], 0.0)

    feat = lax.broadcasted_iota(jnp.int32, (LANES, blk), 0)
    for hh in heads:
        qth = jnp.where((feat // HEAD_DIM) == hh % HEADS_PER_TILE, q_ref[tiles[hh], :], 0.0)
        qt_scr[hh] = (qth * (HEAD_DIM ** -0.5 * LOG2_E)).astype(BF16)
        q_hi, q_lo = _split_bf16(qth)
        km_hi, km_lo = _split_bf16(km_ref[:, tiles[hh]])
        gate = _dot(km_hi, q_hi) + _dot(km_lo, q_hi) + _dot(km_hi, q_lo)
        sel_scr[hh] = _top_blocks_t(gate, i).astype(F32)
        m_scr[hh] = jnp.full((1, blk), NEG, F32)
        l_scr[hh] = jnp.zeros((1, blk), F32)
        acc_scr[hh] = jnp.zeros((HEAD_DIM, blk), F32)
    scores_into(s_own, i)
    scores_into(s_even, 0)
    consume(s_own, i, own=True)

    def two_blocks(c, carry):
        j = 2 * c
        scores_into(s_odd, j + 1)
        consume(s_even, j, own=False)
        scores_into(s_even, j + 2)
        consume(s_odd, j + 1, own=False)
        return carry

    lax.fori_loop(0, (i + 1) // 2, two_blocks, 0)
    ot = jnp.concatenate([acc_scr[hh] / l_scr[hh] for hh in heads], axis=0)
    o_ref[...] = ot.T


def _moba_prompt(q_t, k_bf, vt_bf, km):
    b, d, t = q_t.shape
    blk = MOBA_BLOCK
    nb = t // blk
    width = min(d, 2 * LANES)
    n_h = width // HEAD_DIM
    assert t % blk == 0 and nb % 8 == 0 and d % width == 0
    tile_spec = pl.BlockSpec((None, blk, width), lambda bi, hg, i: (bi, i, hg))
    return pl.pallas_call(
        _moba_prompt_kernel,
        grid=(b, d // width, nb),
        in_specs=[pl.BlockSpec((None, width, blk), lambda bi, hg, i: (bi, hg, i)),
                  pl.BlockSpec((None, t, width), lambda bi, hg, i: (bi, 0, hg)),
                  pl.BlockSpec((None, nb, width, blk), lambda bi, hg, i: (bi, 0, hg, 0)),
                  pl.BlockSpec((None, nb, width), lambda bi, hg, i: (bi, 0, hg))],
        out_specs=tile_spec,
        out_shape=jax.ShapeDtypeStruct((b, t, d), F32),
        scratch_shapes=[pltpu.VMEM((n_h, LANES, blk), BF16),
                        pltpu.VMEM((n_h, nb, blk), F32),
                        pltpu.VMEM((n_h, 1, blk), F32),
                        pltpu.VMEM((n_h, 1, blk), F32),
                        pltpu.VMEM((n_h, HEAD_DIM, blk), F32)]
                       + [pltpu.VMEM((n_h, blk, blk), F32)] * 3,
        compiler_params=_params("arbitrary", "arbitrary", "arbitrary"),
        name="moba_prompt",
    )(q_t, k_bf, vt_bf, km)


def _pad_rows(x, n):
    return jnp.concatenate([x, jnp.zeros((n - x.shape[0], x.shape[1]), x.dtype)], axis=0)


def _moba_sample_kernel(pt_ref, q_ref, kn_ref, vn_ref, *rest, n_pages, page, n_heads):
    k_pages, v_pages = rest[:n_pages], rest[n_pages:2 * n_pages]
    o_ref, s_scr = rest[2 * n_pages], rest[2 * n_pages + 1]
    t_dec, d = q_ref.shape
    r = n_heads * t_dec
    ppb = MOBA_BLOCK // page
    n_past = n_pages // ppb
    own = lax.broadcasted_iota(jnp.int32, (r, d), 0) // t_dec
    head_mask = (lax.broadcasted_iota(jnp.int32, (r, d), 1) // HEAD_DIM) == own
    qrows = jnp.where(head_mask, jnp.concatenate([q_ref[...]] * n_heads, axis=0), 0.0)
    qs = (qrows * HEAD_DIM ** -0.5).astype(BF16)

    def block_of(pages, j):
        return jnp.concatenate([pages[p][...].astype(BF16) for p in range(j * ppb, (j + 1) * ppb)], axis=1)

    gates = []
    for j in range(n_past):
        sl = slice(j * MOBA_BLOCK, (j + 1) * MOBA_BLOCK)
        s_scr[:, sl] = _dot(qs, block_of(k_pages, j))
        gates.append(jnp.sum(s_scr[:, sl], axis=1, keepdims=True))
    for j in range(n_past):
        rank = jnp.zeros((r, 1), jnp.int32)
        for j2 in range(n_past):
            if j2 != j:
                ahead = (gates[j2] > gates[j]) | ((gates[j2] == gates[j]) & (j2 < j))
                rank = rank + ahead.astype(jnp.int32)
        chosen = rank < MOBA_TOPK
        sl = slice(j * MOBA_BLOCK, (j + 1) * MOBA_BLOCK)
        s_scr[:, sl] = jnp.where(chosen, s_scr[:, sl], NEG)
    t_row = lax.broadcasted_iota(jnp.int32, (r, page), 0) % t_dec
    t_col = lax.broadcasted_iota(jnp.int32, (r, page), 1)
    s_own = _dot_nt(qs, _pad_rows(kn_ref[...], page).astype(BF16))
    s_scr[:, n_pages * page:] = jnp.where(t_col <= t_row, s_own, NEG)

    s = s_scr[...]
    m = jnp.max(s, axis=1, keepdims=True)
    p_all = jnp.exp(s - m)
    l = jnp.sum(p_all, axis=1, keepdims=True)
    acc = _dot(p_all[:, n_pages * page:].astype(BF16), _pad_rows(vn_ref[...], page).astype(BF16))
    for j in range(n_past):
        acc = acc + _dot_nt(p_all[:, j * MOBA_BLOCK:(j + 1) * MOBA_BLOCK].astype(BF16), block_of(v_pages, j))
    o_full = jnp.where(head_mask, acc / l, 0.0)
    out = o_full[0:t_dec]
    for h in range(1, n_heads):
        out = out + o_full[h * t_dec:(h + 1) * t_dec]
    o_ref[...] = out


def _moba_sample(q, k_new, v_new, cache_kt, cache_vt, layer, page_table):
    b, t_dec, d = q.shape
    n_pages = page_table.shape[1]
    page = cache_kt.shape[3]
    n_heads = d // HEAD_DIM
    assert MOBA_BLOCK % page == 0 and (n_pages * page) % MOBA_BLOCK == 0 and t_dec <= page
    tok_spec = pl.BlockSpec((None, t_dec, d), lambda i, pt: (i, 0, 0))
    page_specs = [pl.BlockSpec((None, None, d, page),
                               functools.partial(lambda i, pt, p: (layer, pt[i, p], 0, 0), p=p))
                  for p in range(n_pages)]
    grid_spec = pltpu.PrefetchScalarGridSpec(
        num_scalar_prefetch=1,
        grid=(b,),
        in_specs=[tok_spec, tok_spec, tok_spec] + page_specs + page_specs,
        out_specs=tok_spec,
        scratch_shapes=[pltpu.VMEM((n_heads * t_dec, (n_pages + 1) * page), F32)],
    )
    return pl.pallas_call(
        functools.partial(_moba_sample_kernel, n_pages=n_pages, page=page, n_heads=n_heads),
        grid_spec=grid_spec,
        out_shape=jax.ShapeDtypeStruct((b, t_dec, d), F32),
        compiler_params=_params("arbitrary"),
        name="moba_sample",
    )(page_table, q, k_new, v_new, *([cache_kt] * n_pages), *([cache_vt] * n_pages))


def _dil_prompt_kernel(q_ref, k_ref, v_ref, o_ref, lse_ref, *, dil, tq, sub, width, win):
    length = k_ref.shape[0] // dil
    base = pl.program_id(2) * (sub * tq)
    lane = lax.broadcasted_iota(jnp.int32, (tq, LANES), 1)
    row_col = (lax.broadcasted_iota(jnp.int32, (tq, width), 0)
               - lax.broadcasted_iota(jnp.int32, (tq, width), 1))
    heads = range(HEADS_PER_TILE)

    def rows(start, n):
        return pl.ds(start, n) if dil == 1 else pl.ds(start, n, stride=dil)

    def attend(items):
        qs, kws, vws, valids = [], [], [], []
        for r, u in items:
            q0 = base + u * tq
            ks = jnp.clip(q0 - win, 0, length - width)
            q = q_ref[rows(u * tq * dil + r, tq), :]
            qs.append([(jnp.where((lane // HEAD_DIM) == hh, q, 0.0) * HEAD_DIM ** -0.5).astype(BF16)
                       for hh in heads])
            kws.append(k_ref[rows(ks * dil + r, width), :].astype(BF16))
            vws.append(v_ref[rows(ks * dil + r, width), :].astype(BF16))
            delta = q0 - ks + row_col
            valids.append((delta >= 0) & (delta <= win))
        n = range(len(items))
        scores = [[_dot_nt(qs[it][hh], kws[it]) for hh in heads] for it in n]
        ps, ls, ms = [], [], []
        for it in n:
            for hh in heads:
                s = jnp.where(valids[it], scores[it][hh], NEG)
                m = jnp.max(s, axis=1, keepdims=True)
                p = jnp.exp(s - m)
                ls.append(jnp.sum(p, axis=1, keepdims=True))
                ms.append(m)
                ps.append(p.astype(BF16))
        pvs = [_dot(ps[it * len(heads) + hh], vws[it]) for it in n for hh in heads]
        for it, (r, u) in enumerate(items):
            c0, c1 = it * len(heads), it * len(heads) + 1
            o_ref[rows(u * tq * dil + r, tq), :] = jnp.where(lane < HEAD_DIM, pvs[c0] / ls[c0], pvs[c1] / ls[c1])
            lse_ref[rows(u * tq * dil + r, tq), :] = jnp.where(
                lane < HEAD_DIM, jnp.broadcast_to(ms[c0] + jnp.log(ls[c0]), (tq, LANES)),
                jnp.broadcast_to(ms[c1] + jnp.log(ls[c1]), (tq, LANES)))

    if dil == 1:
        attend([(0, u) for u in range(sub)])
    else:
        def two_classes(c, carry):
            attend([(2 * c, u) for u in range(sub)] + [(2 * c + 1, u) for u in range(sub)])
            return carry
        lax.fori_loop(0, dil // 2, two_classes, 0)


def _dil_prompt(q, k, v, window, dil):
    b, t, dk = q.shape
    assert t % dil == 0 and HEADS_PER_TILE == 2 and (dil == 1 or dil % 2 == 0)
    length = t // dil
    win = window // dil
    tq = min(256, length)
    sub = 2 if dil == 1 and length % (2 * tq) == 0 else 1
    width = min(tq + win, length)
    assert length % (sub * tq) == 0
    tile_spec = pl.BlockSpec((None, sub * tq * dil, LANES), lambda bi, h, i: (bi, i, h))
    seq_spec = pl.BlockSpec((None, t, LANES), lambda bi, h, i: (bi, 0, h))
    return pl.pallas_call(
        functools.partial(_dil_prompt_kernel, dil=dil, tq=tq, sub=sub, width=width, win=win),
        grid=(b, dk // LANES, length // (sub * tq)),
        in_specs=[tile_spec, seq_spec, seq_spec],
        out_specs=[tile_spec, tile_spec],
        out_shape=[jax.ShapeDtypeStruct((b, t, dk), F32)] * 2,
        compiler_params=_params("arbitrary", "arbitrary", "arbitrary"),
        name="dilated_prompt",
    )(q, k, v)


def _dil_sample_kernel(q0_ref, q1_ref, q2_ref, kn_ref, vn_ref, kc_ref, vc_ref, o_ref, s_scr, p_scr, *, n_heads):
    q_refs = (q0_ref, q1_ref, q2_ref)
    t_dec, dk = kn_ref.shape
    past = kc_ref.shape[1]
    pad = LANES
    rg = n_heads * t_dec
    head_mask = _head_lane_mask((rg, dk), t_dec)
    qrows = jnp.concatenate(
        [jnp.where(head_mask, jnp.concatenate([qr[...]] * n_heads, axis=0), 0.0) for qr in q_refs], axis=0)
    qs = (qrows * HEAD_DIM ** -0.5).astype(BF16)
    s_scr[:, :past] = _dot(qs, kc_ref[...].astype(BF16))
    s_scr[:, past:] = _dot_nt(qs, _pad_rows(kn_ref[...], pad).astype(BF16))

    t_row = lax.broadcasted_iota(jnp.int32, (rg, past + pad), 0) % t_dec
    col = lax.broadcasted_iota(jnp.int32, (rg, past + pad), 1)
    delta = past + t_row - col
    in_range = col < past + t_dec
    l_g, lse_g = [], []
    for g, (window, dil) in enumerate(B_CONFIGS):
        rs = slice(g * rg, (g + 1) * rg)
        assert dil & (dil - 1) == 0
        valid = in_range & (delta >= 0) & (delta <= window) & ((delta & (dil - 1)) == 0)
        s = jnp.where(valid, s_scr[rs, :], NEG)
        m = jnp.max(s, axis=1, keepdims=True)
        p = jnp.exp(s - m)
        l = jnp.sum(p, axis=1, keepdims=True)
        p_scr[rs, :] = p.astype(BF16)
        l_g.append(l)
        lse_g.append(m + jnp.log(l))
    acc = (_dot_nt(p_scr[:, :past], vc_ref[...].astype(BF16))
           + _dot(p_scr[:, past:], _pad_rows(vn_ref[...], pad).astype(BF16)))
    mx = functools.reduce(jnp.maximum, lse_g)
    e = [jnp.exp(x - mx) for x in lse_g]
    den = functools.reduce(lambda a, b: a + b, e)
    mixed = jnp.zeros((rg, dk), F32)
    for g in range(len(B_CONFIGS)):
        mixed = mixed + (e[g] / den) * (acc[g * rg:(g + 1) * rg] / l_g[g])
    mixed = jnp.where(head_mask, mixed, 0.0)
    out = mixed[0:t_dec]
    for h in range(1, n_heads):
        out = out + mixed[h * t_dec:(h + 1) * t_dec]
    o_ref[...] = out


def _dil_sample(q_groups, k_new, v_new, cache_kt, cache_vt):
    b, t_dec, dk = k_new.shape
    past = cache_kt.shape[2]
    n_heads = dk // HEAD_DIM
    n_rows = len(B_CONFIGS) * n_heads * t_dec
    tok_spec = pl.BlockSpec((None, t_dec, dk), lambda i: (i, 0, 0))
    cache_spec = pl.BlockSpec((None, dk, past), lambda i: (i, 0, 0))
    return pl.pallas_call(
        functools.partial(_dil_sample_kernel, n_heads=n_heads),
        grid=(b,),
        in_specs=[tok_spec] * 5 + [cache_spec] * 2,
        out_specs=tok_spec,
        out_shape=jax.ShapeDtypeStruct((b, t_dec, dk), F32),
        scratch_shapes=[pltpu.VMEM((n_rows, past + LANES), F32),
                        pltpu.VMEM((n_rows, past + LANES), BF16)],
        compiler_params=_params("arbitrary"),
        name="dilated_sample",
    )(*q_groups, k_new, v_new, cache_kt, cache_vt)


def _trunk(x, mods, kv_mod, pos, past, weights):
    (norm_g, w_ffn_gate, w_ffn_up, w_ffn_down, w_qkv_a, w_o_a,
     kv_norm_g, w_kv_b, w_q_b, w_o_b, final_norm_g) = weights
    n_seq, t, d = x.shape
    m = n_seq * t
    depth = norm_g.shape[0]
    n_a = w_qkv_a.shape[0]
    dk = w_kv_b.shape[1] // 2
    is_prompt = past is None
    short = t < 256
    rows = _Rows(m, min(1024 if is_prompt else 512, m if short else t), t)
    rows_p = _Rows(m, min(512, m if short else t), t)
    tables = _rope_tables(pos)
    if rows.per_row:
        tables = (tuple(jnp.tile(tb, (n_seq, 1)) for tb in tables[0]),
                  tuple(jnp.tile(tb, (1, n_seq)) for tb in tables[1]))

    def heads_last(x_t):
        return jnp.transpose(x_t.reshape(n_seq, -1, HEAD_DIM, x_t.shape[-1]), (0, 3, 1, 2))

    h = x.reshape(m, d)
    a_k, a_v = [], []
    b_k = b_v = b_k_out = b_v_out = None
    for l in range(depth):
        if l == n_a:
            kvm = rows.prep(kv_mod)
            if is_prompt:
                b_k, b_kt, b_v, b_vt = _proj(rows_p, h, kvm, (0, 1), kv_norm_g, w_kv_b, tables,
                                             ((True, ("f32", "t_f32")), (False, ("f32", "t_f32"))))
                b_k_out, b_v_out = heads_last(b_kt), heads_last(b_vt)
            else:
                b_k, b_v = _proj(rows_p, h, kvm, (0, 1), kv_norm_g, w_kv_b, tables,
                                 ((True, ("f32",)), (False, ("f32",))))
                b_k_out, b_v_out = (r.reshape(n_seq, t, -1, HEAD_DIM) for r in (b_k, b_v))
        md = rows.prep(mods[l])
        h = _ffn(rows, h, md, (0, 1, 2), norm_g[l, 0], w_ffn_gate, w_ffn_up, w_ffn_down, l, 0)
        if l < n_a:
            shp = (n_seq, t, d)
            if is_prompt:
                q_t, k_t, k_bf, km, v_t, vt_bf = _proj(
                    rows_p, h, md, (3, 4), norm_g[l, 1], w_qkv_a[l], tables,
                    ((True, ("t_f32",)), (True, ("t_f32", "bf16", "block_mean")), (False, ("t_f32", "t_bf16"))))
                o = _moba_prompt(q_t, k_bf.reshape(shp), vt_bf, km)
                a_k.append(heads_last(k_t))
                a_v.append(heads_last(v_t))
            else:
                q, k, v = _proj(rows_p, h, md, (3, 4), norm_g[l, 1], w_qkv_a[l], tables,
                                ((True, ("f32",)), (True, ("f32",)), (False, ("f32",))))
                cache_kt, cache_vt = (
                    jnp.transpose(c, (0, 1, 3, 4, 2)).reshape(c.shape[0], c.shape[1], d, c.shape[2])
                    for c in past[:2])
                o = _moba_sample(q.reshape(shp), k.reshape(shp), v.reshape(shp), cache_kt, cache_vt, l, past[2])
                a_k.append(k.reshape(n_seq, t, -1, HEAD_DIM))
                a_v.append(v.reshape(n_seq, t, -1, HEAD_DIM))
            h = _oproj(rows, h, md, 5,w_o_a[l], [o.reshape(m, d)])
        else:
            lb = l - n_a
            qg = _proj(rows_p, h, md, (3, 4), norm_g[l, 1], w_q_b[lb], tables,
                       ((True, ("f32",)),) * len(B_CONFIGS))
            shp = (n_seq, t, dk)
            if is_prompt:
                res = [_dil_prompt(qi.reshape(shp), b_k.reshape(shp), b_v.reshape(shp), w, dl)
                       for qi, (w, dl) in zip(qg, B_CONFIGS)]
                h = _oproj(rows, h, md, 5,w_o_b[lb], [r[0].reshape(m, dk) for r in res],
                           [r[1].reshape(m, dk) for r in res])
            else:
                cb_kt, cb_vt = (jnp.transpose(c, (0, 2, 3, 1)).reshape(n_seq, dk, c.shape[1]) for c in past[3:5])
                o = _dil_sample([qi.reshape(shp) for qi in qg], b_k.reshape(shp), b_v.reshape(shp), cb_kt, cb_vt)
                h = _oproj(rows, h, md, 5,w_o_b[lb], [o.reshape(m, dk)])
        h = _ffn(rows, h, md, (6, 7, 8), norm_g[l, 2], w_ffn_gate, w_ffn_up, w_ffn_down, l, 1,
                 final_g=final_norm_g if l == depth - 1 else None)
    return h.reshape(n_seq, t, d), jnp.stack(a_k), jnp.stack(a_v), b_k_out, b_v_out


def kernel(x_prompt, x_sample, cache_a_k, cache_a_v, cache_b_k, cache_b_v, page_table, c_prompt, c_sample, norm_g, w_mod, b_mod, w_ffn_gate, w_ffn_up, w_ffn_down, w_qkv_a, w_o_a, kv_norm_g, w_kv_mod, b_kv_mod, w_kv_b, w_q_b, w_o_b, final_norm_g):
    bp, t, d = x_prompt.shape
    bs, t_dec, _ = x_sample.shape
    past_len = page_table.shape[1] * cache_a_k.shape[2]
    assert past_len % MOBA_BLOCK == 0 and t_dec <= MOBA_BLOCK

    n_c = bp + bs
    c_all = _pad_rows(jnp.concatenate([c_prompt, c_sample], axis=0), -(-n_c // 8) * 8)
    mods = _modulation(c_all, w_mod, b_mod)
    kv_mod = _modulation(c_all, w_kv_mod[None], b_kv_mod[None])[0]

    bf = lambda w: w.astype(BF16)
    weights = (norm_g, bf(w_ffn_gate), bf(w_ffn_up), bf(w_ffn_down), bf(w_qkv_a), bf(w_o_a),
               kv_norm_g, bf(w_kv_b), bf(w_q_b), bf(w_o_b), final_norm_g)
    y_p, ak_p, av_p, bk_p, bv_p = _trunk(
        x_prompt, mods[:, :bp], kv_mod[:bp], jnp.arange(t, dtype=jnp.int32), None, weights)
    y_s, ak_s, av_s, bk_s, bv_s = _trunk(
        x_sample, mods[:, bp:n_c], kv_mod[bp:n_c], past_len + jnp.arange(t_dec, dtype=jnp.int32),
        (cache_a_k, cache_a_v, page_table, cache_b_k, cache_b_v), weights)

    keep = min(max(w for w, _ in B_CONFIGS), t)
    return (y_p, y_s, ak_p, av_p, ak_s, av_s, bk_p[:, -keep:], bv_p[:, -keep:], bk_s, bv_s)
```

```python
import functools

import jax
import jax.numpy as jnp
from jax import lax
from jax.experimental import pallas as pl
from jax.experimental.pallas import tpu as pltpu

HEAD_DIM = 64
ROT_DIM = HEAD_DIM // 4
ROPE_THETA = 500000.0
MOBA_BLOCK = 256
MOBA_TOPK = 3
B_CONFIGS = ((128, 1), (512, 4), (2048, 16))
EPS = 1e-6
NEG = -1e30
LOG2_E = 1.4426950408889634

LANES = 128
HEADS_PER_TILE = LANES // HEAD_DIM
DIL_ITEMS = 4
VMEM_LIMIT = 56 * 1024 * 1024

F32 = jnp.float32
BF16 = jnp.bfloat16


def _params(*sem):
    return pltpu.CompilerParams(dimension_semantics=sem, vmem_limit_bytes=VMEM_LIMIT)


def _dot(a, b):
    return jnp.dot(a, b, preferred_element_type=F32)


def _dot_nt(a, b):
    return lax.dot_general(a, b, (((1,), (1,)), ((), ())), preferred_element_type=F32)


def _silu(x):
    return x * jax.nn.sigmoid(x)


def _rmsnorm(x, g):
    return x * lax.rsqrt(jnp.mean(x * x, axis=-1, keepdims=True) + EPS) * g


def _rms_mod(x, g, shift, scale):
    return _rmsnorm(x, g) * (1.0 + scale) + shift


def _head_lane_mask(shape, rows_per_head):
    row = lax.broadcasted_iota(jnp.int32, shape, 0)
    lane = lax.broadcasted_iota(jnp.int32, shape, 1)
    return (lane // HEAD_DIM) == (row // rows_per_head)


class _Rows:
    def __init__(self, m, tm, rows_per_seq):
        self.m, self.tm = m, tm
        self.per_row = rows_per_seq < tm
        self.rows_per_seq = rows_per_seq
        self.tiles_per_seq = max(rows_per_seq // tm, 1)

    def prep(self, vecs):
        if self.per_row:
            return jnp.repeat(vecs, self.rows_per_seq, axis=0)
        return vecs[:, None, :]

    def mod_spec(self, d, col):
        if self.per_row:
            return pl.BlockSpec((self.tm, d), lambda i, j: (i, col))
        tps = self.tiles_per_seq
        return pl.BlockSpec((None, 1, d), lambda i, j: (i // tps, 0, col))

    def pos_spec(self, table_rows):
        n_pos_tiles = table_rows // self.tm
        return pl.BlockSpec((self.tm, LANES), lambda i, j: (i % n_pos_tiles, 0))


def _mod_kernel(c_ref, w_ref, b_ref, o_ref):
    cs = _silu(c_ref[...]).astype(BF16)
    o_ref[...] = _dot(cs, w_ref[...].astype(BF16)) + b_ref[...]


def _modulation(c, w, b):
    mc, d = c.shape
    nl, _, n = w.shape
    tn = d
    return pl.pallas_call(
        _mod_kernel,
        grid=(nl, n // tn),
        in_specs=[pl.BlockSpec((mc, d), lambda l, j: (0, 0)),
                  pl.BlockSpec((None, d, tn), lambda l, j: (l, 0, j)),
                  pl.BlockSpec((None, 1, tn), lambda l, j: (l, 0, j))],
        out_specs=pl.BlockSpec((None, mc, tn), lambda l, j: (l, 0, j)),
        out_shape=jax.ShapeDtypeStruct((nl, mc, n), F32),
        compiler_params=_params("arbitrary", "arbitrary"),
        name="modulation",
    )(c, w, b.reshape(nl, 1, n))


def _ffn_kernel(h_ref, sh_ref, sc_ref, gt_ref, g_ref, wg_ref, wu_ref, wd_ref, *rest, final):
    if final:
        fg_ref, o_ref, u_scr, act_scr = rest
    else:
        o_ref, u_scr, act_scr = rest
    d, ff = wg_ref.shape
    tf = 256 if ff % 256 == 0 else LANES
    to = min(512, d)
    u_scr[...] = _rms_mod(h_ref[...], g_ref[...], sh_ref[...], sc_ref[...]).astype(BF16)
    for c in range(ff // tf):
        sl = slice(c * tf, (c + 1) * tf)
        u = u_scr[...]
        a = _dot(u, wg_ref[:, sl])
        b = _dot(u, wu_ref[:, sl])
        act_scr[:, sl] = (_silu(a) * b).astype(BF16)
    for c in range(d // to):
        sl = slice(c * to, (c + 1) * to)
        down = _dot(act_scr[...], wd_ref[:, sl])
        o_ref[:, sl] = h_ref[:, sl] + 0.5 * gt_ref[:, sl] * down
    if final:
        o_ref[...] = _rmsnorm(o_ref[...], fg_ref[...])


def _ffn(rows, h, mod, cols, g, wg, wu, wd, l, s, final_g=None):
    m, d = h.shape
    ff = wg.shape[-1]
    tm = rows.tm
    final = final_g is not None
    resident = lambda shape: pl.BlockSpec((None, None) + shape, lambda i, f: (l, s, 0, 0),
                                          pipeline_mode=pl.Buffered(1))
    in_specs = [pl.BlockSpec((tm, d), lambda i, f: (i, 0)),
                rows.mod_spec(d, cols[0]), rows.mod_spec(d, cols[1]), rows.mod_spec(d, cols[2]),
                pl.BlockSpec((1, d), lambda i, f: (0, 0)),
                resident((d, ff)), resident((d, ff)), resident((ff, d))]
    args = [h, mod, mod, mod, g.reshape(1, d), wg, wu, wd]
    if final:
        in_specs.append(pl.BlockSpec((1, d), lambda i, f: (0, 0)))
        args.append(final_g.reshape(1, d))
    return pl.pallas_call(
        functools.partial(_ffn_kernel, final=final),
        grid=(m // tm, 1),
        in_specs=in_specs,
        out_specs=pl.BlockSpec((tm, d), lambda i, f: (i, 0)),
        out_shape=jax.ShapeDtypeStruct((m, d), F32),
        scratch_shapes=[pltpu.VMEM((tm, d), BF16), pltpu.VMEM((tm, ff), BF16)],
        compiler_params=_params("arbitrary", "arbitrary"),
        name="ffn",
    )(*args)


def _rope_tables(pos):
    half = ROT_DIM // 2
    n = pos.shape[0]
    freq = ROPE_THETA ** (-jnp.arange(half, dtype=F32) / half)
    ang = pos.astype(F32)[:, None] * freq[None, :]
    cos, sin = jnp.cos(ang), jnp.sin(ang)
    rest = HEAD_DIM - ROT_DIM
    ct = jnp.concatenate([cos, cos, jnp.ones((n, rest), F32)], axis=1)
    s_lo = jnp.concatenate([-sin, jnp.zeros((n, half + rest), F32)], axis=1)
    s_hi = jnp.concatenate([jnp.zeros((n, half), F32), sin, jnp.zeros((n, rest), F32)], axis=1)
    tile = lambda t: jnp.tile(t, (1, HEADS_PER_TILE))
    return (tile(ct), tile(s_hi), tile(s_lo)), (cos.T, sin.T)


def _rope_tile(x, ct, s_hi, s_lo):
    half = ROT_DIM // 2
    return x * ct + pltpu.roll(x, half, 1) * s_hi + pltpu.roll(x, LANES - half, 1) * s_lo


def _rope_t(x_t, cos_t, sin_t):
    half = ROT_DIM // 2
    pieces = []
    for base in range(0, x_t.shape[0], HEAD_DIM):
        x1, x2 = x_t[base:base + half], x_t[base + half:base + ROT_DIM]
        pieces += [x1 * cos_t - x2 * sin_t, x2 * cos_t + x1 * sin_t, x_t[base + ROT_DIM:base + HEAD_DIM]]
    return jnp.concatenate(pieces, axis=0)


_MEAN_GROUP = 8


def _proj_kernel(h_ref, sh_ref, sc_ref, g_ref, w_ref, ct_ref, shi_ref, slo_ref, cos_t_ref, sin_t_ref,
                 *outs, parts):
    i = pl.program_id(0)
    u = _rms_mod(h_ref[...], g_ref[...], sh_ref[...], sc_ref[...]).astype(BF16)
    tm = u.shape[0]
    n_p = w_ref.shape[1] // len(parts)
    blocks = tm // MOBA_BLOCK
    n_out = 0
    for p, (rope, forms) in enumerate(parts):
        part_outs = outs[n_out:n_out + len(forms)]
        n_out += len(forms)
        y = _dot(u, w_ref[:, p * n_p:(p + 1) * n_p])
        val = val_t = None
        if any(f.startswith("t_") for f in forms):
            val_t = y.T
            if rope:
                val_t = _rope_t(val_t, cos_t_ref[...], sin_t_ref[...])
            if any(not f.startswith("t_") for f in forms):
                val = val_t.T
        elif rope:
            ct, s_hi, s_lo = ct_ref[...], shi_ref[...], slo_ref[...]
            val = jnp.concatenate([_rope_tile(y[:, c * LANES:(c + 1) * LANES], ct, s_hi, s_lo)
                                   for c in range(n_p // LANES)], axis=1)
        else:
            val = y
        for form, out in zip(forms, part_outs):
            if form == "f32":
                out[...] = val
            elif form == "bf16":
                out[...] = val.astype(BF16)
            elif form == "t_f32":
                out[...] = val_t
            elif form == "t_bf16":
                for c in range(blocks):
                    out[c] = val_t[:, c * MOBA_BLOCK:(c + 1) * MOBA_BLOCK].astype(BF16)
            elif form == "block_mean":
                base = (i % (_MEAN_GROUP // blocks)) * blocks
                for c in range(blocks):
                    out[pl.ds(base + c, 1), :] = jnp.mean(
                        val[c * MOBA_BLOCK:(c + 1) * MOBA_BLOCK], axis=0, keepdims=True)


def _proj(rows, h, mod, cols, g, w, tables, parts):
    m, d = h.shape
    n_p = w.shape[1] // len(parts)
    tm, tps, t = rows.tm, rows.tiles_per_seq, rows.rows_per_seq
    n_seq = m // t
    out_specs, out_shape = [], []
    for _, forms in parts:
        for form in forms:
            if form in ("f32", "bf16"):
                out_specs.append(pl.BlockSpec((tm, n_p), lambda i, j: (i, 0)))
                out_shape.append(jax.ShapeDtypeStruct((m, n_p), F32 if form == "f32" else BF16))
                continue
            assert not rows.per_row and tm % MOBA_BLOCK == 0
            if form == "t_f32":
                out_specs.append(pl.BlockSpec((None, n_p, tm), lambda i, j: (i // tps, 0, i % tps)))
                out_shape.append(jax.ShapeDtypeStruct((n_seq, n_p, t), F32))
            elif form == "t_bf16":
                out_specs.append(pl.BlockSpec((None, tm // MOBA_BLOCK, n_p, MOBA_BLOCK),
                                              lambda i, j: (i // tps, i % tps, 0, 0)))
                out_shape.append(jax.ShapeDtypeStruct((n_seq, t // MOBA_BLOCK, n_p, MOBA_BLOCK), BF16))
            else:
                steps = _MEAN_GROUP * MOBA_BLOCK // tm
                assert form == "block_mean" and tps % steps == 0
                out_specs.append(pl.BlockSpec((None, _MEAN_GROUP, n_p),
                                              lambda i, j: (i // tps, (i % tps) // steps, 0)))
                out_shape.append(jax.ShapeDtypeStruct((n_seq, t // MOBA_BLOCK, n_p), F32))
    lane_tables, t_tables = tables
    pos_spec = rows.pos_spec(lane_tables[0].shape[0])
    n_pos_tiles = lane_tables[0].shape[0] // tm
    pos_t_spec = pl.BlockSpec((ROT_DIM // 2, tm), lambda i, j: (0, i % n_pos_tiles))
    return pl.pallas_call(
        functools.partial(_proj_kernel, parts=parts),
        grid=(m // tm, 1),
        in_specs=[pl.BlockSpec((tm, d), lambda i, j: (i, 0)),
                  rows.mod_spec(d, cols[0]), rows.mod_spec(d, cols[1]),
                  pl.BlockSpec((1, d), lambda i, j: (0, 0)),
                  pl.BlockSpec(w.shape, lambda i, j: (0, 0)),
                  pos_spec, pos_spec, pos_spec, pos_t_spec, pos_t_spec],
        out_specs=out_specs,
        out_shape=out_shape,
        compiler_params=_params("arbitrary", "arbitrary"),
        name="proj",
    )(h, mod, mod, g.reshape(1, d), w, *lane_tables, *t_tables)


def _oproj_kernel(h_ref, gt_ref, w_ref, *rest, n_groups):
    o_ref = rest[-1]
    if n_groups == 0:
        o = rest[0][...]
    else:
        os_, lses = rest[:n_groups], rest[n_groups:2 * n_groups]
        lse = [r[...] for r in lses]
        mx = functools.reduce(jnp.maximum, lse)
        e = [jnp.exp(x - mx) for x in lse]
        den = functools.reduce(lambda a, b: a + b, e)
        o = functools.reduce(lambda a, b: a + b, [(ei / den) * r[...] for ei, r in zip(e, os_)])
    o_ref[...] = h_ref[...] + gt_ref[...] * _dot(o.astype(BF16), w_ref[...])


def _oproj(rows, h, mod, col, w, o_list, lse_list=()):
    m, d = h.shape
    k = w.shape[0]
    tm = rows.tm
    xs = list(o_list) + list(lse_list)
    return pl.pallas_call(
        functools.partial(_oproj_kernel, n_groups=len(lse_list)),
        grid=(m // tm, 1),
        in_specs=[pl.BlockSpec((tm, d), lambda i, j: (i, 0)),
                  rows.mod_spec(d, col),
                  pl.BlockSpec((k, d), lambda i, j: (0, 0))]
                 + [pl.BlockSpec((tm, k), lambda i, j: (i, 0))] * len(xs),
        out_specs=pl.BlockSpec((tm, d), lambda i, j: (i, 0)),
        out_shape=jax.ShapeDtypeStruct((m, d), F32),
        compiler_params=_params("arbitrary", "arbitrary"),
        name="oproj",
    )(h, mod, w, *xs)


def _split_bf16(x):
    hi = x.astype(BF16)
    lo = (x - hi.astype(F32)).astype(BF16)
    return hi, lo


def _top_blocks_t(gate, n_past):
    blk_id = lax.broadcasted_iota(jnp.int32, gate.shape, 0).astype(F32)
    n_past = n_past.astype(F32)
    g = jnp.where(blk_id < n_past, gate, NEG)
    sel = jnp.zeros(gate.shape, jnp.bool_)
    for _ in range(MOBA_TOPK):
        mx = jnp.max(g, axis=0, keepdims=True)
        idx = jnp.min(jnp.where(g == mx, blk_id, float(gate.shape[0])), axis=0, keepdims=True)
        pick = blk_id == idx
        sel = sel | pick
        g = jnp.where(pick, -jnp.inf, g)
    return sel & (blk_id < n_past)


def _moba_prompt_kernel(q_ref, k_ref, vt_ref, km_ref, o_ref,
                        qt_scr, sel_scr, m_scr, l_scr, acc_scr, s_own, s_even, s_odd):
    blk = MOBA_BLOCK
    n_h = qt_scr.shape[0]
    tiles = [slice(hh // HEADS_PER_TILE * LANES, (hh // HEADS_PER_TILE + 1) * LANES) for hh in range(n_h)]
    nb = vt_ref.shape[0]
    i = pl.program_id(2)
    heads = range(n_h)
    hs = [slice(hh * HEAD_DIM, (hh + 1) * HEAD_DIM) for hh in heads]

    def scores_into(dst, first, n_blocks):
        rows = n_blocks * blk
        jj = jnp.minimum(first, nb - n_blocks)
        kb = k_ref[pl.ds(pl.multiple_of(jj * blk, blk), rows), :]
        for hh in heads:
            dst[hh] = _dot(kb[:, tiles[hh]], qt_scr[hh])

    def consume(src, part, j, own):
        jj = jnp.minimum(j, nb - 1)
        vtb = vt_ref[jj]
        ps, alphas, chosen = [], [], []
        for hh in heads:
            s = src[hh, part * blk:(part + 1) * blk, :]
            if own:
                key_id = lax.broadcasted_iota(jnp.int32, (blk, blk), 0)
                qry_id = lax.broadcasted_iota(jnp.int32, (blk, blk), 1)
                s = jnp.where(key_id <= qry_id, s, NEG)
                ch = jnp.full((1, blk), True)
            else:
                ch = (sel_scr[hh, pl.ds(jj, 1), :] > 0.5) & (j < i)
            m_old = m_scr[hh]
            m_all = jnp.maximum(m_old, jnp.max(s, axis=0, keepdims=True))
            p = jnp.exp2(s - m_all)
            m_new = jnp.where(ch, m_all, m_old)
            alpha = jnp.exp2(m_old - m_new)
            l_scr[hh] = alpha * l_scr[hh] + jnp.where(ch, jnp.sum(p, axis=0, keepdims=True), 0.0)
            m_scr[hh] = m_new
            ps.append(p.astype(BF16))
            alphas.append(alpha)
            chosen.append(ch)
        pvs = [_dot(vtb[hs[hh], :], ps[hh]) for hh in heads]
        for hh in heads:
            acc_scr[hh] = alphas[hh] * acc_scr[hh] + jnp.where(chosen[hh], pvs[hh], 0.0)

    feat = lax.broadcasted_iota(jnp.int32, (LANES, blk), 0)
    for hh in heads:
        qth = jnp.where((feat // HEAD_DIM) == hh % HEADS_PER_TILE, q_ref[tiles[hh], :], 0.0)
        qt_scr[hh] = (qth * (HEAD_DIM ** -0.5 * LOG2_E)).astype(BF16)
        q_hi, q_lo = _split_bf16(qth)
        km_hi, km_lo = _split_bf16(km_ref[:, tiles[hh]])
        gate = _dot(km_hi, q_hi) + _dot(km_lo, q_hi) + _dot(km_hi, q_lo)
        sel_scr[hh] = _top_blocks_t(gate, i).astype(F32)
        m_scr[hh] = jnp.full((1, blk), NEG, F32)
        l_scr[hh] = jnp.zeros((1, blk), F32)
        acc_scr[hh] = jnp.zeros((HEAD_DIM, blk), F32)
    scores_into(s_own, i, 1)
    scores_into(s_even, 0, 1)
    consume(s_own, 0, i, own=True)

    def two_blocks(c, carry):
        j = 2 * c
        scores_into(s_odd, j + 1, 1)
        consume(s_even, 0, j, own=False)
        scores_into(s_even, j + 2, 1)
        consume(s_odd, 0, j + 1, own=False)
        return carry

    lax.fori_loop(0, (i + 1) // 2, two_blocks, 0)
    ot = jnp.concatenate([acc_scr[hh] / l_scr[hh] for hh in heads], axis=0)
    o_ref[...] = ot.T


def _moba_prompt(q_t, k_bf, vt_bf, km):
    b, d, t = q_t.shape
    blk = MOBA_BLOCK
    nb = t // blk
    width = min(d, 2 * LANES)
    n_h = width // HEAD_DIM
    assert t % blk == 0 and nb % 8 == 0 and d % width == 0
    tile_spec = pl.BlockSpec((None, blk, width), lambda bi, hg, i: (bi, i, hg))
    return pl.pallas_call(
        _moba_prompt_kernel,
        grid=(b, d // width, nb),
        in_specs=[pl.BlockSpec((None, width, blk), lambda bi, hg, i: (bi, hg, i)),
                  pl.BlockSpec((None, t, width), lambda bi, hg, i: (bi, 0, hg)),
                  pl.BlockSpec((None, nb, width, blk), lambda bi, hg, i: (bi, 0, hg, 0)),
                  pl.BlockSpec((None, nb, width), lambda bi, hg, i: (bi, 0, hg))],
        out_specs=tile_spec,
        out_shape=jax.ShapeDtypeStruct((b, t, d), F32),
        scratch_shapes=[pltpu.VMEM((n_h, LANES, blk), BF16),
                        pltpu.VMEM((n_h, nb, blk), F32),
                        pltpu.VMEM((n_h, 1, blk), F32),
                        pltpu.VMEM((n_h, 1, blk), F32),
                        pltpu.VMEM((n_h, HEAD_DIM, blk), F32)]
                       + [pltpu.VMEM((n_h, blk, blk), F32)] * 3,
        compiler_params=_params("arbitrary", "arbitrary", "arbitrary"),
        name="moba_prompt",
    )(q_t, k_bf, vt_bf, km)


def _pad_rows(x, n):
    return jnp.concatenate([x, jnp.zeros((n - x.shape[0], x.shape[1]), x.dtype)], axis=0)


def _moba_sample_kernel(pt_ref, q_ref, kn_ref, vn_ref, *rest, n_pages, page, n_heads):
    k_pages, v_pages = rest[:n_pages], rest[n_pages:2 * n_pages]
    o_ref, s_scr = rest[2 * n_pages], rest[2 * n_pages + 1]
    t_dec, d = q_ref.shape
    r = n_heads * t_dec
    ppb = MOBA_BLOCK // page
    n_past = n_pages // ppb
    own = lax.broadcasted_iota(jnp.int32, (r, d), 0) // t_dec
    head_mask = (lax.broadcasted_iota(jnp.int32, (r, d), 1) // HEAD_DIM) == own
    qrows = jnp.where(head_mask, jnp.concatenate([q_ref[...]] * n_heads, axis=0), 0.0)
    qs = (qrows * HEAD_DIM ** -0.5).astype(BF16)

    def block_of(pages, j):
        return jnp.concatenate([pages[p][...].astype(BF16) for p in range(j * ppb, (j + 1) * ppb)], axis=1)

    gates = []
    for j in range(n_past):
        sl = slice(j * MOBA_BLOCK, (j + 1) * MOBA_BLOCK)
        s_scr[:, sl] = _dot(qs, block_of(k_pages, j))
        gates.append(jnp.sum(s_scr[:, sl], axis=1, keepdims=True))
    for j in range(n_past):
        rank = jnp.zeros((r, 1), jnp.int32)
        for j2 in range(n_past):
            if j2 != j:
                ahead = (gates[j2] > gates[j]) | ((gates[j2] == gates[j]) & (j2 < j))
                rank = rank + ahead.astype(jnp.int32)
        chosen = rank < MOBA_TOPK
        sl = slice(j * MOBA_BLOCK, (j + 1) * MOBA_BLOCK)
        s_scr[:, sl] = jnp.where(chosen, s_scr[:, sl], NEG)
    t_row = lax.broadcasted_iota(jnp.int32, (r, page), 0) % t_dec
    t_col = lax.broadcasted_iota(jnp.int32, (r, page), 1)
    s_own = _dot_nt(qs, _pad_rows(kn_ref[...], page).astype(BF16))
    s_scr[:, n_pages * page:] = jnp.where(t_col <= t_row, s_own, NEG)

    s = s_scr[...]
    m = jnp.max(s, axis=1, keepdims=True)
    p_all = jnp.exp(s - m)
    l = jnp.sum(p_all, axis=1, keepdims=True)
    acc = _dot(p_all[:, n_pages * page:].astype(BF16), _pad_rows(vn_ref[...], page).astype(BF16))
    for j in range(n_past):
        acc = acc + _dot_nt(p_all[:, j * MOBA_BLOCK:(j + 1) * MOBA_BLOCK].astype(BF16), block_of(v_pages, j))
    o_full = jnp.where(head_mask, acc / l, 0.0)
    out = o_full[0:t_dec]
    for h in range(1, n_heads):
        out = out + o_full[h * t_dec:(h + 1) * t_dec]
    o_ref[...] = out


def _moba_sample(q, k_new, v_new, cache_kt, cache_vt, layer, page_table):
    b, t_dec, d = q.shape
    n_pages = page_table.shape[1]
    page = cache_kt.shape[3]
    n_heads = d // HEAD_DIM
    assert MOBA_BLOCK % page == 0 and (n_pages * page) % MOBA_BLOCK == 0 and t_dec <= page
    tok_spec = pl.BlockSpec((None, t_dec, d), lambda i, pt: (i, 0, 0))
    page_specs = [pl.BlockSpec((None, None, d, page),
                               functools.partial(lambda i, pt, p: (layer, pt[i, p], 0, 0), p=p))
                  for p in range(n_pages)]
    grid_spec = pltpu.PrefetchScalarGridSpec(
        num_scalar_prefetch=1,
        grid=(b,),
        in_specs=[tok_spec, tok_spec, tok_spec] + page_specs + page_specs,
        out_specs=tok_spec,
        scratch_shapes=[pltpu.VMEM((n_heads * t_dec, (n_pages + 1) * page), F32)],
    )
    return pl.pallas_call(
        functools.partial(_moba_sample_kernel, n_pages=n_pages, page=page, n_heads=n_heads),
        grid_spec=grid_spec,
        out_shape=jax.ShapeDtypeStruct((b, t_dec, d), F32),
        compiler_params=_params("arbitrary"),
        name="moba_sample",
    )(page_table, q, k_new, v_new, *([cache_kt] * n_pages), *([cache_vt] * n_pages))


def _dil_prompt_kernel(q_ref, k_ref, v_ref, o_ref, lse_ref, *, dil, tq, sub, width, win):
    length = k_ref.shape[0] // dil
    base = pl.program_id(2) * (sub * tq)
    lane = lax.broadcasted_iota(jnp.int32, (tq, LANES), 1)
    row_col = (lax.broadcasted_iota(jnp.int32, (tq, width), 0)
               - lax.broadcasted_iota(jnp.int32, (tq, width), 1))
    heads = range(HEADS_PER_TILE)

    def rows(start, n):
        return pl.ds(start, n) if dil == 1 else pl.ds(start, n, stride=dil)

    def attend(items):
        qs, kws, vws, valids = [], [], [], []
        for r, u in items:
            q0 = base + u * tq
            ks = jnp.clip(q0 - win, 0, length - width)
            q = q_ref[rows(u * tq * dil + r, tq), :]
            qs.append([(jnp.where((lane // HEAD_DIM) == hh, q, 0.0) * (HEAD_DIM ** -0.5 * LOG2_E)).astype(BF16)
                       for hh in heads])
            kws.append(k_ref[rows(ks * dil + r, width), :].astype(BF16))
            vws.append(v_ref[rows(ks * dil + r, width), :].astype(BF16))
            delta = q0 - ks + row_col
            valids.append((delta >= 0) & (delta <= win))
        n = range(len(items))
        scores = [[_dot_nt(qs[it][hh], kws[it]) for hh in heads] for it in n]
        ps, ls, ms = [], [], []
        for it in n:
            for hh in heads:
                s = jnp.where(valids[it], scores[it][hh], NEG)
                m = jnp.max(s, axis=1, keepdims=True)
                p = jnp.exp2(s - m)
                ls.append(jnp.sum(p, axis=1, keepdims=True))
                ms.append(m * (1.0 / LOG2_E))
                ps.append(p.astype(BF16))
        pvs = [_dot(ps[it * len(heads) + hh], vws[it]) for it in n for hh in heads]
        for it, (r, u) in enumerate(items):
            c0, c1 = it * len(heads), it * len(heads) + 1
            o_ref[rows(u * tq * dil + r, tq), :] = jnp.where(lane < HEAD_DIM, pvs[c0] / ls[c0], pvs[c1] / ls[c1])
            lse_ref[rows(u * tq * dil + r, tq), :] = jnp.where(
                lane < HEAD_DIM, jnp.broadcast_to(ms[c0] + jnp.log(ls[c0]), (tq, LANES)),
                jnp.broadcast_to(ms[c1] + jnp.log(ls[c1]), (tq, LANES)))

    if dil == 1:
        attend([(0, u) for u in range(sub)])
    else:
        per_iter = min(dil, DIL_ITEMS)

        def some_classes(c, carry):
            attend([(per_iter * c + x, u) for x in range(per_iter) for u in range(sub)])
            return carry
        lax.fori_loop(0, dil // per_iter, some_classes, 0)


def _dil_prompt(q, k, v, window, dil):
    b, t, dk = q.shape
    assert t % dil == 0 and HEADS_PER_TILE == 2 and (dil == 1 or dil % 2 == 0)
    length = t // dil
    win = window // dil
    tq = min(LANES, length)
    sub = DIL_ITEMS if dil == 1 and length % (DIL_ITEMS * tq) == 0 else 1
    assert dil == 1 or dil % min(dil, DIL_ITEMS) == 0
    width = min(tq + win, length)
    assert length % (sub * tq) == 0
    tile_spec = pl.BlockSpec((None, sub * tq * dil, LANES), lambda bi, h, i: (bi, i, h))
    seq_spec = pl.BlockSpec((None, t, LANES), lambda bi, h, i: (bi, 0, h))
    return pl.pallas_call(
        functools.partial(_dil_prompt_kernel, dil=dil, tq=tq, sub=sub, width=width, win=win),
        grid=(b, dk // LANES, length // (sub * tq)),
        in_specs=[tile_spec, seq_spec, seq_spec],
        out_specs=[tile_spec, tile_spec],
        out_shape=[jax.ShapeDtypeStruct((b, t, dk), F32)] * 2,
        compiler_params=_params("arbitrary", "arbitrary", "arbitrary"),
        name="dilated_prompt",
    )(q, k, v)


def _dil_sample_kernel(q0_ref, q1_ref, q2_ref, kn_ref, vn_ref, kc_ref, vc_ref, o_ref, s_scr, p_scr, *, n_heads):
    q_refs = (q0_ref, q1_ref, q2_ref)
    t_dec, dk = kn_ref.shape
    past = kc_ref.shape[1]
    pad = LANES
    rg = n_heads * t_dec
    head_mask = _head_lane_mask((rg, dk), t_dec)
    qrows = jnp.concatenate(
        [jnp.where(head_mask, jnp.concatenate([qr[...]] * n_heads, axis=0), 0.0) for qr in q_refs], axis=0)
    qs = (qrows * HEAD_DIM ** -0.5).astype(BF16)
    s_scr[:, :past] = _dot(qs, kc_ref[...].astype(BF16))
    s_scr[:, past:] = _dot_nt(qs, _pad_rows(kn_ref[...], pad).astype(BF16))

    t_row = lax.broadcasted_iota(jnp.int32, (rg, past + pad), 0) % t_dec
    col = lax.broadcasted_iota(jnp.int32, (rg, past + pad), 1)
    delta = past + t_row - col
    in_range = col < past + t_dec
    l_g, lse_g = [], []
    for g, (window, dil) in enumerate(B_CONFIGS):
        rs = slice(g * rg, (g + 1) * rg)
        assert dil & (dil - 1) == 0
        valid = in_range & (delta >= 0) & (delta <= window) & ((delta & (dil - 1)) == 0)
        s = jnp.where(valid, s_scr[rs, :], NEG)
        m = jnp.max(s, axis=1, keepdims=True)
        p = jnp.exp(s - m)
        l = jnp.sum(p, axis=1, keepdims=True)
        p_scr[rs, :] = p.astype(BF16)
        l_g.append(l)
        lse_g.append(m + jnp.log(l))
    acc = (_dot_nt(p_scr[:, :past], vc_ref[...].astype(BF16))
           + _dot(p_scr[:, past:], _pad_rows(vn_ref[...], pad).astype(BF16)))
    mx = functools.reduce(jnp.maximum, lse_g)
    e = [jnp.exp(x - mx) for x in lse_g]
    den = functools.reduce(lambda a, b: a + b, e)
    mixed = jnp.zeros((rg, dk), F32)
    for g in range(len(B_CONFIGS)):
        mixed = mixed + (e[g] / den) * (acc[g * rg:(g + 1) * rg] / l_g[g])
    mixed = jnp.where(head_mask, mixed, 0.0)
    out = mixed[0:t_dec]
    for h in range(1, n_heads):
        out = out + mixed[h * t_dec:(h + 1) * t_dec]
    o_ref[...] = out


def _dil_sample(q_groups, k_new, v_new, cache_kt, cache_vt):
    b, t_dec, dk = k_new.shape
    past = cache_kt.shape[2]
    n_heads = dk // HEAD_DIM
    n_rows = len(B_CONFIGS) * n_heads * t_dec
    tok_spec = pl.BlockSpec((None, t_dec, dk), lambda i: (i, 0, 0))
    cache_spec = pl.BlockSpec((None, dk, past), lambda i: (i, 0, 0))
    return pl.pallas_call(
        functools.partial(_dil_sample_kernel, n_heads=n_heads),
        grid=(b,),
        in_specs=[tok_spec] * 5 + [cache_spec] * 2,
        out_specs=tok_spec,
        out_shape=jax.ShapeDtypeStruct((b, t_dec, dk), F32),
        scratch_shapes=[pltpu.VMEM((n_rows, past + LANES), F32),
                        pltpu.VMEM((n_rows, past + LANES), BF16)],
        compiler_params=_params("arbitrary"),
        name="dilated_sample",
    )(*q_groups, k_new, v_new, cache_kt, cache_vt)


def _trunk(x, mods, kv_mod, pos, past, weights):
    (norm_g, w_ffn_gate, w_ffn_up, w_ffn_down, w_qkv_a, w_o_a,
     kv_norm_g, w_kv_b, w_q_b, w_o_b, final_norm_g) = weights
    n_seq, t, d = x.shape
    m = n_seq * t
    depth = norm_g.shape[0]
    n_a = w_qkv_a.shape[0]
    dk = w_kv_b.shape[1] // 2
    is_prompt = past is None
    short = t < 256
    rows = _Rows(m, min(1024 if is_prompt else 512, m if short else t), t)
    rows_p = _Rows(m, min(512, m if short else t), t)
    tables = _rope_tables(pos)
    if rows.per_row:
        tables = (tuple(jnp.tile(tb, (n_seq, 1)) for tb in tables[0]),
                  tuple(jnp.tile(tb, (1, n_seq)) for tb in tables[1]))

    def heads_last(x_t):
        return jnp.transpose(x_t.reshape(n_seq, -1, HEAD_DIM, x_t.shape[-1]), (0, 3, 1, 2))

    h = x.reshape(m, d)
    a_k, a_v = [], []
    b_k = b_v = b_k_out = b_v_out = None
    for l in range(depth):
        if l == n_a:
            kvm = rows.prep(kv_mod)
            if is_prompt:
                b_k, b_kt, b_v, b_vt = _proj(rows_p, h, kvm, (0, 1), kv_norm_g, w_kv_b, tables,
                                             ((True, ("f32", "t_f32")), (False, ("f32", "t_f32"))))
                b_k_out, b_v_out = heads_last(b_kt), heads_last(b_vt)
            else:
                b_k, b_v = _proj(rows_p, h, kvm, (0, 1), kv_norm_g, w_kv_b, tables,
                                 ((True, ("f32",)), (False, ("f32",))))
                b_k_out, b_v_out = (r.reshape(n_seq, t, -1, HEAD_DIM) for r in (b_k, b_v))
        md = rows.prep(mods[l])
        h = _ffn(rows, h, md, (0, 1, 2), norm_g[l, 0], w_ffn_gate, w_ffn_up, w_ffn_down, l, 0)
        if l < n_a:
            shp = (n_seq, t, d)
            if is_prompt:
                q_t, k_t, k_bf, km, v_t, vt_bf = _proj(
                    rows_p, h, md, (3, 4), norm_g[l, 1], w_qkv_a[l], tables,
                    ((True, ("t_f32",)), (True, ("t_f32", "bf16", "block_mean")), (False, ("t_f32", "t_bf16"))))
                o = _moba_prompt(q_t, k_bf.reshape(shp), vt_bf, km)
                a_k.append(heads_last(k_t))
                a_v.append(heads_last(v_t))
            else:
                q, k, v = _proj(rows_p, h, md, (3, 4), norm_g[l, 1], w_qkv_a[l], tables,
                                ((True, ("f32",)), (True, ("f32",)), (False, ("f32",))))
                cache_kt, cache_vt = (
                    jnp.transpose(c, (0, 1, 3, 4, 2)).reshape(c.shape[0], c.shape[1], d, c.shape[2])
                    for c in past[:2])
                o = _moba_sample(q.reshape(shp), k.reshape(shp), v.reshape(shp), cache_kt, cache_vt, l, past[2])
                a_k.append(k.reshape(n_seq, t, -1, HEAD_DIM))
                a_v.append(v.reshape(n_seq, t, -1, HEAD_DIM))
            h = _oproj(rows, h, md, 5,w_o_a[l], [o.reshape(m, d)])
        else:
            lb = l - n_a
            qg = _proj(rows_p, h, md, (3, 4), norm_g[l, 1], w_q_b[lb], tables,
                       ((True, ("f32",)),) * len(B_CONFIGS))
            shp = (n_seq, t, dk)
            if is_prompt:
                res = [_dil_prompt(qi.reshape(shp), b_k.reshape(shp), b_v.reshape(shp), w, dl)
                       for qi, (w, dl) in zip(qg, B_CONFIGS)]
                h = _oproj(rows, h, md, 5,w_o_b[lb], [r[0].reshape(m, dk) for r in res],
                           [r[1].reshape(m, dk) for r in res])
            else:
                cb_kt, cb_vt = (jnp.transpose(c, (0, 2, 3, 1)).reshape(n_seq, dk, c.shape[1]) for c in past[3:5])
                o = _dil_sample([qi.reshape(shp) for qi in qg], b_k.reshape(shp), b_v.reshape(shp), cb_kt, cb_vt)
                h = _oproj(rows, h, md, 5,w_o_b[lb], [o.reshape(m, dk)])
        h = _ffn(rows, h, md, (6, 7, 8), norm_g[l, 2], w_ffn_gate, w_ffn_up, w_ffn_down, l, 1,
                 final_g=final_norm_g if l == depth - 1 else None)
    return h.reshape(n_seq, t, d), jnp.stack(a_k), jnp.stack(a_v), b_k_out, b_v_out


def kernel(x_prompt, x_sample, cache_a_k, cache_a_v, cache_b_k, cache_b_v, page_table, c_prompt, c_sample, norm_g, w_mod, b_mod, w_ffn_gate, w_ffn_up, w_ffn_down, w_qkv_a, w_o_a, kv_norm_g, w_kv_mod, b_kv_mod, w_kv_b, w_q_b, w_o_b, final_norm_g):
    bp, t, d = x_prompt.shape
    bs, t_dec, _ = x_sample.shape
    past_len = page_table.shape[1] * cache_a_k.shape[2]
    assert past_len % MOBA_BLOCK == 0 and t_dec <= MOBA_BLOCK

    n_c = bp + bs
    c_all = _pad_rows(jnp.concatenate([c_prompt, c_sample], axis=0), -(-n_c // 8) * 8)
    mods = _modulation(c_all, w_mod, b_mod)
    kv_mod = _modulation(c_all, w_kv_mod[None], b_kv_mod[None])[0]

    bf = lambda w: w.astype(BF16)
    weights = (norm_g, bf(w_ffn_gate), bf(w_ffn_up), bf(w_ffn_down), bf(w_qkv_a), bf(w_o_a),
               kv_norm_g, bf(w_kv_b), bf(w_q_b), bf(w_o_b), final_norm_g)
    y_p, ak_p, av_p, bk_p, bv_p = _trunk(
        x_prompt, mods[:, :bp], kv_mod[:bp], jnp.arange(t, dtype=jnp.int32), None, weights)
    y_s, ak_s, av_s, bk_s, bv_s = _trunk(
        x_sample, mods[:, bp:n_c], kv_mod[bp:n_c], past_len + jnp.arange(t_dec, dtype=jnp.int32),
        (cache_a_k, cache_a_v, page_table, cache_b_k, cache_b_v), weights)

    keep = min(max(w for w, _ in B_CONFIGS), t)
    return (y_p, y_s, ak_p, av_p, ak_s, av_s, bk_p[:, -keep:], bv_p[:, -keep:], bk_s, bv_s)
```

```python
import functools

import jax
import jax.numpy as jnp
from jax import lax
from jax.experimental import pallas as pl
from jax.experimental.pallas import tpu as pltpu

HEAD_DIM = 64
ROT_DIM = HEAD_DIM // 4
ROPE_THETA = 500000.0
MOBA_BLOCK = 256
MOBA_TOPK = 3
B_CONFIGS = ((128, 1), (512, 4), (2048, 16))
EPS = 1e-6
NEG = -1e30
LOG2_E = 1.4426950408889634

LANES = 128
HEADS_PER_TILE = LANES // HEAD_DIM
DIL_ITEMS = 4
VMEM_LIMIT = 56 * 1024 * 1024

F32 = jnp.float32
BF16 = jnp.bfloat16


def _params(*sem):
    return pltpu.CompilerParams(dimension_semantics=sem, vmem_limit_bytes=VMEM_LIMIT)


def _dot(a, b):
    return jnp.dot(a, b, preferred_element_type=F32)


def _dot_nt(a, b):
    return lax.dot_general(a, b, (((1,), (1,)), ((), ())), preferred_element_type=F32)


def _silu(x):
    return x * jax.nn.sigmoid(x)


def _rmsnorm(x, g):
    return x * lax.rsqrt(jnp.mean(x * x, axis=-1, keepdims=True) + EPS) * g


def _rms_mod(x, g, shift, scale):
    return _rmsnorm(x, g) * (1.0 + scale) + shift


def _head_lane_mask(shape, rows_per_head):
    row = lax.broadcasted_iota(jnp.int32, shape, 0)
    lane = lax.broadcasted_iota(jnp.int32, shape, 1)
    return (lane // HEAD_DIM) == (row // rows_per_head)


class _Rows:
    def __init__(self, m, tm, rows_per_seq):
        self.m, self.tm = m, tm
        self.per_row = rows_per_seq < tm
        self.rows_per_seq = rows_per_seq
        self.tiles_per_seq = max(rows_per_seq // tm, 1)

    def prep(self, vecs):
        if self.per_row:
            return jnp.repeat(vecs, self.rows_per_seq, axis=0)
        return vecs[:, None, :]

    def mod_spec(self, d, col):
        if self.per_row:
            return pl.BlockSpec((self.tm, d), lambda i, j: (i, col))
        tps = self.tiles_per_seq
        return pl.BlockSpec((None, 1, d), lambda i, j: (i // tps, 0, col))

    def pos_spec(self, table_rows):
        n_pos_tiles = table_rows // self.tm
        return pl.BlockSpec((self.tm, LANES), lambda i, j: (i % n_pos_tiles, 0))


def _mod_kernel(c_ref, w_ref, b_ref, o_ref):
    cs = _silu(c_ref[...]).astype(BF16)
    o_ref[...] = _dot(cs, w_ref[...].astype(BF16)) + b_ref[...]


def _modulation(c, w, b):
    mc, d = c.shape
    nl, _, n = w.shape
    tn = d
    return pl.pallas_call(
        _mod_kernel,
        grid=(nl, n // tn),
        in_specs=[pl.BlockSpec((mc, d), lambda l, j: (0, 0)),
                  pl.BlockSpec((None, d, tn), lambda l, j: (l, 0, j)),
                  pl.BlockSpec((None, 1, tn), lambda l, j: (l, 0, j))],
        out_specs=pl.BlockSpec((None, mc, tn), lambda l, j: (l, 0, j)),
        out_shape=jax.ShapeDtypeStruct((nl, mc, n), F32),
        compiler_params=_params("arbitrary", "arbitrary"),
        name="modulation",
    )(c, w, b.reshape(nl, 1, n))


def _ffn_kernel(h_ref, sh_ref, sc_ref, gt_ref, g_ref, wg_ref, wu_ref, wd_ref, *rest, final):
    if final:
        fg_ref, o_ref, u_scr, act_scr = rest
    else:
        o_ref, u_scr, act_scr = rest
    d, ff = wg_ref.shape
    tf = 256 if ff % 256 == 0 else LANES
    to = min(512, d)
    u_scr[...] = _rms_mod(h_ref[...], g_ref[...], sh_ref[...], sc_ref[...]).astype(BF16)
    for c in range(ff // tf):
        sl = slice(c * tf, (c + 1) * tf)
        u = u_scr[...]
        a = _dot(u, wg_ref[:, sl])
        b = _dot(u, wu_ref[:, sl])
        act_scr[:, sl] = (_silu(a) * b).astype(BF16)
    for c in range(d // to):
        sl = slice(c * to, (c + 1) * to)
        down = _dot(act_scr[...], wd_ref[:, sl])
        o_ref[:, sl] = h_ref[:, sl] + 0.5 * gt_ref[:, sl] * down
    if final:
        o_ref[...] = _rmsnorm(o_ref[...], fg_ref[...])


def _ffn(rows, h, mod, cols, g, wg, wu, wd, l, s, final_g=None):
    m, d = h.shape
    ff = wg.shape[-1]
    tm = rows.tm
    final = final_g is not None
    resident = lambda shape: pl.BlockSpec((None, None) + shape, lambda i, f: (l, s, 0, 0),
                                          pipeline_mode=pl.Buffered(1))
    in_specs = [pl.BlockSpec((tm, d), lambda i, f: (i, 0)),
                rows.mod_spec(d, cols[0]), rows.mod_spec(d, cols[1]), rows.mod_spec(d, cols[2]),
                pl.BlockSpec((1, d), lambda i, f: (0, 0)),
                resident((d, ff)), resident((d, ff)), resident((ff, d))]
    args = [h, mod, mod, mod, g.reshape(1, d), wg, wu, wd]
    if final:
        in_specs.append(pl.BlockSpec((1, d), lambda i, f: (0, 0)))
        args.append(final_g.reshape(1, d))
    return pl.pallas_call(
        functools.partial(_ffn_kernel, final=final),
        grid=(m // tm, 1),
        in_specs=in_specs,
        out_specs=pl.BlockSpec((tm, d), lambda i, f: (i, 0)),
        out_shape=jax.ShapeDtypeStruct((m, d), F32),
        scratch_shapes=[pltpu.VMEM((tm, d), BF16), pltpu.VMEM((tm, ff), BF16)],
        compiler_params=_params("arbitrary", "arbitrary"),
        name="ffn",
    )(*args)


def _rope_tables(pos):
    half = ROT_DIM // 2
    n = pos.shape[0]
    freq = ROPE_THETA ** (-jnp.arange(half, dtype=F32) / half)
    ang = pos.astype(F32)[:, None] * freq[None, :]
    cos, sin = jnp.cos(ang), jnp.sin(ang)
    rest = HEAD_DIM - ROT_DIM
    ct = jnp.concatenate([cos, cos, jnp.ones((n, rest), F32)], axis=1)
    s_lo = jnp.concatenate([-sin, jnp.zeros((n, half + rest), F32)], axis=1)
    s_hi = jnp.concatenate([jnp.zeros((n, half), F32), sin, jnp.zeros((n, rest), F32)], axis=1)
    tile = lambda t: jnp.tile(t, (1, HEADS_PER_TILE))
    return (tile(ct), tile(s_hi), tile(s_lo)), (cos.T, sin.T)


def _rope_tile(x, ct, s_hi, s_lo):
    half = ROT_DIM // 2
    return x * ct + pltpu.roll(x, half, 1) * s_hi + pltpu.roll(x, LANES - half, 1) * s_lo


def _rope_t(x_t, cos_t, sin_t):
    half = ROT_DIM // 2
    pieces = []
    for base in range(0, x_t.shape[0], HEAD_DIM):
        x1, x2 = x_t[base:base + half], x_t[base + half:base + ROT_DIM]
        pieces += [x1 * cos_t - x2 * sin_t, x2 * cos_t + x1 * sin_t, x_t[base + ROT_DIM:base + HEAD_DIM]]
    return jnp.concatenate(pieces, axis=0)


_MEAN_GROUP = 8


def _proj_kernel(h_ref, sh_ref, sc_ref, g_ref, w_ref, ct_ref, shi_ref, slo_ref, cos_t_ref, sin_t_ref,
                 *outs, parts):
    i = pl.program_id(0)
    u = _rms_mod(h_ref[...], g_ref[...], sh_ref[...], sc_ref[...]).astype(BF16)
    tm = u.shape[0]
    n_p = w_ref.shape[1] // len(parts)
    blocks = tm // MOBA_BLOCK
    n_out = 0
    for p, (rope, forms) in enumerate(parts):
        part_outs = outs[n_out:n_out + len(forms)]
        n_out += len(forms)
        y = _dot(u, w_ref[:, p * n_p:(p + 1) * n_p])
        val = val_t = None
        if any(f.startswith("t_") for f in forms):
            val_t = y.T
            if rope:
                val_t = _rope_t(val_t, cos_t_ref[...], sin_t_ref[...])
            if any(not f.startswith("t_") for f in forms):
                val = val_t.T
        elif rope:
            ct, s_hi, s_lo = ct_ref[...], shi_ref[...], slo_ref[...]
            val = jnp.concatenate([_rope_tile(y[:, c * LANES:(c + 1) * LANES], ct, s_hi, s_lo)
                                   for c in range(n_p // LANES)], axis=1)
        else:
            val = y
        for form, out in zip(forms, part_outs):
            if form == "f32":
                out[...] = val
            elif form == "bf16":
                out[...] = val.astype(BF16)
            elif form == "t_f32":
                out[...] = val_t
            elif form == "t_bf16":
                for c in range(blocks):
                    out[c] = val_t[:, c * MOBA_BLOCK:(c + 1) * MOBA_BLOCK].astype(BF16)
            elif form == "block_mean":
                base = (i % (_MEAN_GROUP // blocks)) * blocks
                for c in range(blocks):
                    out[pl.ds(base + c, 1), :] = jnp.mean(
                        val[c * MOBA_BLOCK:(c + 1) * MOBA_BLOCK], axis=0, keepdims=True)


def _proj(rows, h, mod, cols, g, w, tables, parts):
    m, d = h.shape
    n_p = w.shape[1] // len(parts)
    tm, tps, t = rows.tm, rows.tiles_per_seq, rows.rows_per_seq
    n_seq = m // t
    out_specs, out_shape = [], []
    for _, forms in parts:
        for form in forms:
            if form in ("f32", "bf16"):
                out_specs.append(pl.BlockSpec((tm, n_p), lambda i, j: (i, 0)))
                out_shape.append(jax.ShapeDtypeStruct((m, n_p), F32 if form == "f32" else BF16))
                continue
            assert not rows.per_row and tm % MOBA_BLOCK == 0
            if form == "t_f32":
                out_specs.append(pl.BlockSpec((None, n_p, tm), lambda i, j: (i // tps, 0, i % tps)))
                out_shape.append(jax.ShapeDtypeStruct((n_seq, n_p, t), F32))
            elif form == "t_bf16":
                out_specs.append(pl.BlockSpec((None, tm // MOBA_BLOCK, n_p, MOBA_BLOCK),
                                              lambda i, j: (i // tps, i % tps, 0, 0)))
                out_shape.append(jax.ShapeDtypeStruct((n_seq, t // MOBA_BLOCK, n_p, MOBA_BLOCK), BF16))
            else:
                steps = _MEAN_GROUP * MOBA_BLOCK // tm
                assert form == "block_mean" and tps % steps == 0
                out_specs.append(pl.BlockSpec((None, _MEAN_GROUP, n_p),
                                              lambda i, j: (i // tps, (i % tps) // steps, 0)))
                out_shape.append(jax.ShapeDtypeStruct((n_seq, t // MOBA_BLOCK, n_p), F32))
    lane_tables, t_tables = tables
    pos_spec = rows.pos_spec(lane_tables[0].shape[0])
    n_pos_tiles = lane_tables[0].shape[0] // tm
    pos_t_spec = pl.BlockSpec((ROT_DIM // 2, tm), lambda i, j: (0, i % n_pos_tiles))
    return pl.pallas_call(
        functools.partial(_proj_kernel, parts=parts),
        grid=(m // tm, 1),
        in_specs=[pl.BlockSpec((tm, d), lambda i, j: (i, 0)),
                  rows.mod_spec(d, cols[0]), rows.mod_spec(d, cols[1]),
                  pl.BlockSpec((1, d), lambda i, j: (0, 0)),
                  pl.BlockSpec(w.shape, lambda i, j: (0, 0)),
                  pos_spec, pos_spec, pos_spec, pos_t_spec, pos_t_spec],
        out_specs=out_specs,
        out_shape=out_shape,
        compiler_params=_params("arbitrary", "arbitrary"),
        name="proj",
    )(h, mod, mod, g.reshape(1, d), w, *lane_tables, *t_tables)


def _oproj_kernel(h_ref, gt_ref, w_ref, *rest, n_groups):
    o_ref = rest[-1]
    if n_groups == 0:
        o = rest[0][...]
    else:
        os_, lses = rest[:n_groups], rest[n_groups:2 * n_groups]
        lse = [r[...] for r in lses]
        mx = functools.reduce(jnp.maximum, lse)
        e = [jnp.exp(x - mx) for x in lse]
        den = functools.reduce(lambda a, b: a + b, e)
        o = functools.reduce(lambda a, b: a + b, [(ei / den) * r[...] for ei, r in zip(e, os_)])
    o_ref[...] = h_ref[...] + gt_ref[...] * _dot(o.astype(BF16), w_ref[...])


def _oproj(rows, h, mod, col, w, o_list, lse_list=()):
    m, d = h.shape
    k = w.shape[0]
    tm = rows.tm
    xs = list(o_list) + list(lse_list)
    return pl.pallas_call(
        functools.partial(_oproj_kernel, n_groups=len(lse_list)),
        grid=(m // tm, 1),
        in_specs=[pl.BlockSpec((tm, d), lambda i, j: (i, 0)),
                  rows.mod_spec(d, col),
                  pl.BlockSpec((k, d), lambda i, j: (0, 0))]
                 + [pl.BlockSpec((tm, k), lambda i, j: (i, 0))] * len(xs),
        out_specs=pl.BlockSpec((tm, d), lambda i, j: (i, 0)),
        out_shape=jax.ShapeDtypeStruct((m, d), F32),
        compiler_params=_params("arbitrary", "arbitrary"),
        name="oproj",
    )(h, mod, w, *xs)


def _split_bf16(x):
    hi = x.astype(BF16)
    lo = (x - hi.astype(F32)).astype(BF16)
    return hi, lo


def _top_blocks_t(gate, n_past):
    blk_id = lax.broadcasted_iota(jnp.int32, gate.shape, 0).astype(F32)
    n_past = n_past.astype(F32)
    g = jnp.where(blk_id < n_past, gate, NEG)
    sel = jnp.zeros(gate.shape, jnp.bool_)
    for _ in range(MOBA_TOPK):
        mx = jnp.max(g, axis=0, keepdims=True)
        idx = jnp.min(jnp.where(g == mx, blk_id, float(gate.shape[0])), axis=0, keepdims=True)
        pick = blk_id == idx
        sel = sel | pick
        g = jnp.where(pick, -jnp.inf, g)
    return sel & (blk_id < n_past)


def _moba_prompt_kernel(q_ref, k_ref, vt_ref, km_ref, o_ref,
                        qt_scr, sel_scr, m_scr, l_scr, acc_scr, s_own, s_even, s_odd):
    blk = MOBA_BLOCK
    n_h = qt_scr.shape[0]
    tiles = [slice(hh // HEADS_PER_TILE * LANES, (hh // HEADS_PER_TILE + 1) * LANES) for hh in range(n_h)]
    nb = vt_ref.shape[0]
    i = pl.program_id(2)
    heads = range(n_h)
    hs = [slice(hh * HEAD_DIM, (hh + 1) * HEAD_DIM) for hh in heads]

    def scores_into(dst, first, n_blocks):
        rows = n_blocks * blk
        jj = jnp.minimum(first, nb - n_blocks)
        kb = k_ref[pl.ds(pl.multiple_of(jj * blk, blk), rows), :]
        for hh in heads:
            dst[hh] = _dot(kb[:, tiles[hh]], qt_scr[hh])

    def consume(src, part, j, own):
        jj = jnp.minimum(j, nb - 1)
        vtb = vt_ref[jj]
        ps, alphas, chosen = [], [], []
        for hh in heads:
            s = src[hh, part * blk:(part + 1) * blk, :]
            if own:
                key_id = lax.broadcasted_iota(jnp.int32, (blk, blk), 0)
                qry_id = lax.broadcasted_iota(jnp.int32, (blk, blk), 1)
                s = jnp.where(key_id <= qry_id, s, NEG)
                ch = jnp.full((1, blk), True)
            else:
                ch = (sel_scr[hh, pl.ds(jj, 1), :] > 0.5) & (j < i)
            m_old = m_scr[hh]
            m_all = jnp.maximum(m_old, jnp.max(s, axis=0, keepdims=True))
            p = jnp.exp2(s - m_all)
            m_new = jnp.where(ch, m_all, m_old)
            alpha = jnp.exp2(m_old - m_new)
            l_scr[hh] = alpha * l_scr[hh] + jnp.where(ch, jnp.sum(p, axis=0, keepdims=True), 0.0)
            m_scr[hh] = m_new
            ps.append(p.astype(BF16))
            alphas.append(alpha)
            chosen.append(ch)
        pvs = [_dot(vtb[hs[hh], :], ps[hh]) for hh in heads]
        for hh in heads:
            acc_scr[hh] = alphas[hh] * acc_scr[hh] + jnp.where(chosen[hh], pvs[hh], 0.0)

    feat = lax.broadcasted_iota(jnp.int32, (LANES, blk), 0)
    for hh in heads:
        qth = jnp.where((feat // HEAD_DIM) == hh % HEADS_PER_TILE, q_ref[tiles[hh], :], 0.0)
        qt_scr[hh] = (qth * (HEAD_DIM ** -0.5 * LOG2_E)).astype(BF16)
        q_hi, q_lo = _split_bf16(qth)
        km_hi, km_lo = _split_bf16(km_ref[:, tiles[hh]])
        gate = _dot(km_hi, q_hi) + _dot(km_lo, q_hi) + _dot(km_hi, q_lo)
        sel_scr[hh] = _top_blocks_t(gate, i).astype(F32)
        m_scr[hh] = jnp.full((1, blk), NEG, F32)
        l_scr[hh] = jnp.zeros((1, blk), F32)
        acc_scr[hh] = jnp.zeros((HEAD_DIM, blk), F32)
    scores_into(s_own, i, 1)
    scores_into(s_even, 0, 2)
    consume(s_own, 0, i, own=True)

    def pair_from(cur, nxt):
        def run(c):
            scores_into(nxt, 2 * c + 2, 2)
            consume(cur, 0, 2 * c, own=False)
            consume(cur, 1, 2 * c + 1, own=False)
        return run

    def two_blocks(c, carry):
        lax.cond(c % 2 == 0, pair_from(s_even, s_odd), pair_from(s_odd, s_even), c)
        return carry

    lax.fori_loop(0, (i + 1) // 2, two_blocks, 0)
    ot = jnp.concatenate([acc_scr[hh] / l_scr[hh] for hh in heads], axis=0)
    o_ref[...] = ot.T


def _moba_prompt(q_t, k_bf, vt_bf, km):
    b, d, t = q_t.shape
    blk = MOBA_BLOCK
    nb = t // blk
    width = min(d, 2 * LANES)
    n_h = width // HEAD_DIM
    assert t % blk == 0 and nb % 8 == 0 and d % width == 0
    tile_spec = pl.BlockSpec((None, blk, width), lambda bi, hg, i: (bi, i, hg))
    return pl.pallas_call(
        _moba_prompt_kernel,
        grid=(b, d // width, nb),
        in_specs=[pl.BlockSpec((None, width, blk), lambda bi, hg, i: (bi, hg, i)),
                  pl.BlockSpec((None, t, width), lambda bi, hg, i: (bi, 0, hg)),
                  pl.BlockSpec((None, nb, width, blk), lambda bi, hg, i: (bi, 0, hg, 0)),
                  pl.BlockSpec((None, nb, width), lambda bi, hg, i: (bi, 0, hg))],
        out_specs=tile_spec,
        out_shape=jax.ShapeDtypeStruct((b, t, d), F32),
        scratch_shapes=[pltpu.VMEM((n_h, LANES, blk), BF16),
                        pltpu.VMEM((n_h, nb, blk), F32),
                        pltpu.VMEM((n_h, 1, blk), F32),
                        pltpu.VMEM((n_h, 1, blk), F32),
                        pltpu.VMEM((n_h, HEAD_DIM, blk), F32)]
                       + [pltpu.VMEM((n_h, blk, blk), F32)] + [pltpu.VMEM((n_h, 2 * blk, blk), F32)] * 2,
        compiler_params=_params("arbitrary", "arbitrary", "arbitrary"),
        name="moba_prompt",
    )(q_t, k_bf, vt_bf, km)


def _pad_rows(x, n):
    return jnp.concatenate([x, jnp.zeros((n - x.shape[0], x.shape[1]), x.dtype)], axis=0)


def _moba_sample_kernel(pt_ref, q_ref, kn_ref, vn_ref, *rest, n_pages, page, n_heads):
    k_pages, v_pages = rest[:n_pages], rest[n_pages:2 * n_pages]
    o_ref, s_scr = rest[2 * n_pages], rest[2 * n_pages + 1]
    t_dec, d = q_ref.shape
    r = n_heads * t_dec
    ppb = MOBA_BLOCK // page
    n_past = n_pages // ppb
    own = lax.broadcasted_iota(jnp.int32, (r, d), 0) // t_dec
    head_mask = (lax.broadcasted_iota(jnp.int32, (r, d), 1) // HEAD_DIM) == own
    qrows = jnp.where(head_mask, jnp.concatenate([q_ref[...]] * n_heads, axis=0), 0.0)
    qs = (qrows * HEAD_DIM ** -0.5).astype(BF16)

    def block_of(pages, j):
        return jnp.concatenate([pages[p][...].astype(BF16) for p in range(j * ppb, (j + 1) * ppb)], axis=1)

    gates = []
    for j in range(n_past):
        sl = slice(j * MOBA_BLOCK, (j + 1) * MOBA_BLOCK)
        s_scr[:, sl] = _dot(qs, block_of(k_pages, j))
        gates.append(jnp.sum(s_scr[:, sl], axis=1, keepdims=True))
    for j in range(n_past):
        rank = jnp.zeros((r, 1), jnp.int32)
        for j2 in range(n_past):
            if j2 != j:
                ahead = (gates[j2] > gates[j]) | ((gates[j2] == gates[j]) & (j2 < j))
                rank = rank + ahead.astype(jnp.int32)
        chosen = rank < MOBA_TOPK
        sl = slice(j * MOBA_BLOCK, (j + 1) * MOBA_BLOCK)
        s_scr[:, sl] = jnp.where(chosen, s_scr[:, sl], NEG)
    t_row = lax.broadcasted_iota(jnp.int32, (r, page), 0) % t_dec
    t_col = lax.broadcasted_iota(jnp.int32, (r, page), 1)
    s_own = _dot_nt(qs, _pad_rows(kn_ref[...], page).astype(BF16))
    s_scr[:, n_pages * page:] = jnp.where(t_col <= t_row, s_own, NEG)

    s = s_scr[...]
    m = jnp.max(s, axis=1, keepdims=True)
    p_all = jnp.exp(s - m)
    l = jnp.sum(p_all, axis=1, keepdims=True)
    acc = _dot(p_all[:, n_pages * page:].astype(BF16), _pad_rows(vn_ref[...], page).astype(BF16))
    for j in range(n_past):
        acc = acc + _dot_nt(p_all[:, j * MOBA_BLOCK:(j + 1) * MOBA_BLOCK].astype(BF16), block_of(v_pages, j))
    o_full = jnp.where(head_mask, acc / l, 0.0)
    out = o_full[0:t_dec]
    for h in range(1, n_heads):
        out = out + o_full[h * t_dec:(h + 1) * t_dec]
    o_ref[...] = out


def _moba_sample(q, k_new, v_new, cache_kt, cache_vt, layer, page_table):
    b, t_dec, d = q.shape
    n_pages = page_table.shape[1]
    page = cache_kt.shape[3]
    n_heads = d // HEAD_DIM
    assert MOBA_BLOCK % page == 0 and (n_pages * page) % MOBA_BLOCK == 0 and t_dec <= page
    tok_spec = pl.BlockSpec((None, t_dec, d), lambda i, pt: (i, 0, 0))
    page_specs = [pl.BlockSpec((None, None, d, page),
                               functools.partial(lambda i, pt, p: (layer, pt[i, p], 0, 0), p=p))
                  for p in range(n_pages)]
    grid_spec = pltpu.PrefetchScalarGridSpec(
        num_scalar_prefetch=1,
        grid=(b,),
        in_specs=[tok_spec, tok_spec, tok_spec] + page_specs + page_specs,
        out_specs=tok_spec,
        scratch_shapes=[pltpu.VMEM((n_heads * t_dec, (n_pages + 1) * page), F32)],
    )
    return pl.pallas_call(
        functools.partial(_moba_sample_kernel, n_pages=n_pages, page=page, n_heads=n_heads),
        grid_spec=grid_spec,
        out_shape=jax.ShapeDtypeStruct((b, t_dec, d), F32),
        compiler_params=_params("arbitrary"),
        name="moba_sample",
    )(page_table, q, k_new, v_new, *([cache_kt] * n_pages), *([cache_vt] * n_pages))


def _dil_prompt_kernel(q_ref, k_ref, v_ref, o_ref, lse_ref, *, dil, tq, sub, width, win):
    length = k_ref.shape[0] // dil
    base = pl.program_id(2) * (sub * tq)
    lane = lax.broadcasted_iota(jnp.int32, (tq, LANES), 1)
    row_col = (lax.broadcasted_iota(jnp.int32, (tq, width), 0)
               - lax.broadcasted_iota(jnp.int32, (tq, width), 1))
    heads = range(HEADS_PER_TILE)

    def rows(start, n):
        return pl.ds(start, n) if dil == 1 else pl.ds(start, n, stride=dil)

    def attend(items):
        qs, kws, vws, valids = [], [], [], []
        for r, u in items:
            q0 = base + u * tq
            ks = jnp.clip(q0 - win, 0, length - width)
            q = q_ref[rows(u * tq * dil + r, tq), :]
            qs.append([(jnp.where((lane // HEAD_DIM) == hh, q, 0.0) * (HEAD_DIM ** -0.5 * LOG2_E)).astype(BF16)
                       for hh in heads])
            kws.append(k_ref[rows(ks * dil + r, width), :].astype(BF16))
            vws.append(v_ref[rows(ks * dil + r, width), :].astype(BF16))
            delta = q0 - ks + row_col
            valids.append((delta >= 0) & (delta <= win))
        n = range(len(items))
        scores = [[_dot_nt(qs[it][hh], kws[it]) for hh in heads] for it in n]
        ps, ls, ms = [], [], []
        for it in n:
            for hh in heads:
                s = jnp.where(valids[it], scores[it][hh], NEG)
                m = jnp.max(s, axis=1, keepdims=True)
                p = jnp.exp2(s - m)
                ls.append(jnp.sum(p, axis=1, keepdims=True))
                ms.append(m * (1.0 / LOG2_E))
                ps.append(p.astype(BF16))
        pvs = [_dot(ps[it * len(heads) + hh], vws[it]) for it in n for hh in heads]
        for it, (r, u) in enumerate(items):
            c0, c1 = it * len(heads), it * len(heads) + 1
            o_ref[rows(u * tq * dil + r, tq), :] = jnp.where(lane < HEAD_DIM, pvs[c0] / ls[c0], pvs[c1] / ls[c1])
            lse_ref[rows(u * tq * dil + r, tq), :] = jnp.where(
                lane < HEAD_DIM, jnp.broadcast_to(ms[c0] + jnp.log(ls[c0]), (tq, LANES)),
                jnp.broadcast_to(ms[c1] + jnp.log(ls[c1]), (tq, LANES)))

    if dil == 1:
        attend([(0, u) for u in range(sub)])
    else:
        per_iter = min(dil, DIL_ITEMS)

        def some_classes(c, carry):
            attend([(per_iter * c + x, u) for x in range(per_iter) for u in range(sub)])
            return carry
        lax.fori_loop(0, dil // per_iter, some_classes, 0)


def _dil_prompt(q, k, v, window, dil):
    b, t, dk = q.shape
    assert t % dil == 0 and HEADS_PER_TILE == 2 and (dil == 1 or dil % 2 == 0)
    length = t // dil
    win = window // dil
    tq = min(LANES, length)
    sub = DIL_ITEMS if dil == 1 and length % (DIL_ITEMS * tq) == 0 else 1
    assert dil == 1 or dil % min(dil, DIL_ITEMS) == 0
    width = min(tq + win, length)
    assert length % (sub * tq) == 0
    tile_spec = pl.BlockSpec((None, sub * tq * dil, LANES), lambda bi, h, i: (bi, i, h))
    seq_spec = pl.BlockSpec((None, t, LANES), lambda bi, h, i: (bi, 0, h))
    return pl.pallas_call(
        functools.partial(_dil_prompt_kernel, dil=dil, tq=tq, sub=sub, width=width, win=win),
        grid=(b, dk // LANES, length // (sub * tq)),
        in_specs=[tile_spec, seq_spec, seq_spec],
        out_specs=[tile_spec, tile_spec],
        out_shape=[jax.ShapeDtypeStruct((b, t, dk), F32)] * 2,
        compiler_params=_params("arbitrary", "arbitrary", "arbitrary"),
        name="dilated_prompt",
    )(q, k, v)


def _dil_sample_kernel(q0_ref, q1_ref, q2_ref, kn_ref, vn_ref, kc_ref, vc_ref, o_ref, s_scr, p_scr, *, n_heads):
    q_refs = (q0_ref, q1_ref, q2_ref)
    t_dec, dk = kn_ref.shape
    past = kc_ref.shape[1]
    pad = LANES
    rg = n_heads * t_dec
    head_mask = _head_lane_mask((rg, dk), t_dec)
    qrows = jnp.concatenate(
        [jnp.where(head_mask, jnp.concatenate([qr[...]] * n_heads, axis=0), 0.0) for qr in q_refs], axis=0)
    qs = (qrows * HEAD_DIM ** -0.5).astype(BF16)
    s_scr[:, :past] = _dot(qs, kc_ref[...].astype(BF16))
    s_scr[:, past:] = _dot_nt(qs, _pad_rows(kn_ref[...], pad).astype(BF16))

    t_row = lax.broadcasted_iota(jnp.int32, (rg, past + pad), 0) % t_dec
    col = lax.broadcasted_iota(jnp.int32, (rg, past + pad), 1)
    delta = past + t_row - col
    in_range = col < past + t_dec
    l_g, lse_g = [], []
    for g, (window, dil) in enumerate(B_CONFIGS):
        rs = slice(g * rg, (g + 1) * rg)
        assert dil & (dil - 1) == 0
        valid = in_range & (delta >= 0) & (delta <= window) & ((delta & (dil - 1)) == 0)
        s = jnp.where(valid, s_scr[rs, :], NEG)
        m = jnp.max(s, axis=1, keepdims=True)
        p = jnp.exp(s - m)
        l = jnp.sum(p, axis=1, keepdims=True)
        p_scr[rs, :] = p.astype(BF16)
        l_g.append(l)
        lse_g.append(m + jnp.log(l))
    acc = (_dot_nt(p_scr[:, :past], vc_ref[...].astype(BF16))
           + _dot(p_scr[:, past:], _pad_rows(vn_ref[...], pad).astype(BF16)))
    mx = functools.reduce(jnp.maximum, lse_g)
    e = [jnp.exp(x - mx) for x in lse_g]
    den = functools.reduce(lambda a, b: a + b, e)
    mixed = jnp.zeros((rg, dk), F32)
    for g in range(len(B_CONFIGS)):
        mixed = mixed + (e[g] / den) * (acc[g * rg:(g + 1) * rg] / l_g[g])
    mixed = jnp.where(head_mask, mixed, 0.0)
    out = mixed[0:t_dec]
    for h in range(1, n_heads):
        out = out + mixed[h * t_dec:(h + 1) * t_dec]
    o_ref[...] = out


def _dil_sample(q_groups, k_new, v_new, cache_kt, cache_vt):
    b, t_dec, dk = k_new.shape
    past = cache_kt.shape[2]
    n_heads = dk // HEAD_DIM
    n_rows = len(B_CONFIGS) * n_heads * t_dec
    tok_spec = pl.BlockSpec((None, t_dec, dk), lambda i: (i, 0, 0))
    cache_spec = pl.BlockSpec((None, dk, past), lambda i: (i, 0, 0))
    return pl.pallas_call(
        functools.partial(_dil_sample_kernel, n_heads=n_heads),
        grid=(b,),
        in_specs=[tok_spec] * 5 + [cache_spec] * 2,
        out_specs=tok_spec,
        out_shape=jax.ShapeDtypeStruct((b, t_dec, dk), F32),
        scratch_shapes=[pltpu.VMEM((n_rows, past + LANES), F32),
                        pltpu.VMEM((n_rows, past + LANES), BF16)],
        compiler_params=_params("arbitrary"),
        name="dilated_sample",
    )(*q_groups, k_new, v_new, cache_kt, cache_vt)


def _trunk(x, mods, kv_mod, pos, past, weights):
    (norm_g, w_ffn_gate, w_ffn_up, w_ffn_down, w_qkv_a, w_o_a,
     kv_norm_g, w_kv_b, w_q_b, w_o_b, final_norm_g) = weights
    n_seq, t, d = x.shape
    m = n_seq * t
    depth = norm_g.shape[0]
    n_a = w_qkv_a.shape[0]
    dk = w_kv_b.shape[1] // 2
    is_prompt = past is None
    short = t < 256
    rows = _Rows(m, min(1024 if is_prompt else 512, m if short else t), t)
    rows_p = _Rows(m, min(512, m if short else t), t)
    tables = _rope_tables(pos)
    if rows.per_row:
        tables = (tuple(jnp.tile(tb, (n_seq, 1)) for tb in tables[0]),
                  tuple(jnp.tile(tb, (1, n_seq)) for tb in tables[1]))

    def heads_last(x_t):
        return jnp.transpose(x_t.reshape(n_seq, -1, HEAD_DIM, x_t.shape[-1]), (0, 3, 1, 2))

    h = x.reshape(m, d)
    a_k, a_v = [], []
    b_k = b_v = b_k_out = b_v_out = None
    for l in range(depth):
        if l == n_a:
            kvm = rows.prep(kv_mod)
            if is_prompt:
                b_k, b_kt, b_v, b_vt = _proj(rows_p, h, kvm, (0, 1), kv_norm_g, w_kv_b, tables,
                                             ((True, ("f32", "t_f32")), (False, ("f32", "t_f32"))))
                b_k_out, b_v_out = heads_last(b_kt), heads_last(b_vt)
            else:
                b_k, b_v = _proj(rows_p, h, kvm, (0, 1), kv_norm_g, w_kv_b, tables,
                                 ((True, ("f32",)), (False, ("f32",))))
                b_k_out, b_v_out = (r.reshape(n_seq, t, -1, HEAD_DIM) for r in (b_k, b_v))
        md = rows.prep(mods[l])
        h = _ffn(rows, h, md, (0, 1, 2), norm_g[l, 0], w_ffn_gate, w_ffn_up, w_ffn_down, l, 0)
        if l < n_a:
            shp = (n_seq, t, d)
            if is_prompt:
                q_t, k_t, k_bf, km, v_t, vt_bf = _proj(
                    rows_p, h, md, (3, 4), norm_g[l, 1], w_qkv_a[l], tables,
                    ((True, ("t_f32",)), (True, ("t_f32", "bf16", "block_mean")), (False, ("t_f32", "t_bf16"))))
                o = _moba_prompt(q_t, k_bf.reshape(shp), vt_bf, km)
                a_k.append(heads_last(k_t))
                a_v.append(heads_last(v_t))
            else:
                q, k, v = _proj(rows_p, h, md, (3, 4), norm_g[l, 1], w_qkv_a[l], tables,
                                ((True, ("f32",)), (True, ("f32",)), (False, ("f32",))))
                cache_kt, cache_vt = (
                    jnp.transpose(c, (0, 1, 3, 4, 2)).reshape(c.shape[0], c.shape[1], d, c.shape[2])
                    for c in past[:2])
                o = _moba_sample(q.reshape(shp), k.reshape(shp), v.reshape(shp), cache_kt, cache_vt, l, past[2])
                a_k.append(k.reshape(n_seq, t, -1, HEAD_DIM))
                a_v.append(v.reshape(n_seq, t, -1, HEAD_DIM))
            h = _oproj(rows, h, md, 5,w_o_a[l], [o.reshape(m, d)])
        else:
            lb = l - n_a
            qg = _proj(rows_p, h, md, (3, 4), norm_g[l, 1], w_q_b[lb], tables,
                       ((True, ("f32",)),) * len(B_CONFIGS))
            shp = (n_seq, t, dk)
            if is_prompt:
                res = [_dil_prompt(qi.reshape(shp), b_k.reshape(shp), b_v.reshape(shp), w, dl)
                       for qi, (w, dl) in zip(qg, B_CONFIGS)]
                h = _oproj(rows, h, md, 5,w_o_b[lb], [r[0].reshape(m, dk) for r in res],
                           [r[1].reshape(m, dk) for r in res])
            else:
                cb_kt, cb_vt = (jnp.transpose(c, (0, 2, 3, 1)).reshape(n_seq, dk, c.shape[1]) for c in past[3:5])
                o = _dil_sample([qi.reshape(shp) for qi in qg], b_k.reshape(shp), b_v.reshape(shp), cb_kt, cb_vt)
                h = _oproj(rows, h, md, 5,w_o_b[lb], [o.reshape(m, dk)])
        h = _ffn(rows, h, md, (6, 7, 8), norm_g[l, 2], w_ffn_gate, w_ffn_up, w_ffn_down, l, 1,
                 final_g=final_norm_g if l == depth - 1 else None)
    return h.reshape(n_seq, t, d), jnp.stack(a_k), jnp.stack(a_v), b_k_out, b_v_out


def kernel(x_prompt, x_sample, cache_a_k, cache_a_v, cache_b_k, cache_b_v, page_table, c_prompt, c_sample, norm_g, w_mod, b_mod, w_ffn_gate, w_ffn_up, w_ffn_down, w_qkv_a, w_o_a, kv_norm_g, w_kv_mod, b_kv_mod, w_kv_b, w_q_b, w_o_b, final_norm_g):
    bp, t, d = x_prompt.shape
    bs, t_dec, _ = x_sample.shape
    past_len = page_table.shape[1] * cache_a_k.shape[2]
    assert past_len % MOBA_BLOCK == 0 and t_dec <= MOBA_BLOCK

    n_c = bp + bs
    c_all = _pad_rows(jnp.concatenate([c_prompt, c_sample], axis=0), -(-n_c // 8) * 8)
    mods = _modulation(c_all, w_mod, b_mod)
    kv_mod = _modulation(c_all, w_kv_mod[None], b_kv_mod[None])[0]

    bf = lambda w: w.astype(BF16)
    weights = (norm_g, bf(w_ffn_gate), bf(w_ffn_up), bf(w_ffn_down), bf(w_qkv_a), bf(w_o_a),
               kv_norm_g, bf(w_kv_b), bf(w_q_b), bf(w_o_b), final_norm_g)
    y_p, ak_p, av_p, bk_p, bv_p = _trunk(
        x_prompt, mods[:, :bp], kv_mod[:bp], jnp.arange(t, dtype=jnp.int32), None, weights)
    y_s, ak_s, av_s, bk_s, bv_s = _trunk(
        x_sample, mods[:, bp:n_c], kv_mod[bp:n_c], past_len + jnp.arange(t_dec, dtype=jnp.int32),
        (cache_a_k, cache_a_v, page_table, cache_b_k, cache_b_v), weights)

    keep = min(max(w for w, _ in B_CONFIGS), t)
    return (y_p, y_s, ak_p, av_p, ak_s, av_s, bk_p[:, -keep:], bv_p[:, -keep:], bk_s, bv_s)
```

```python
import functools

import jax
import jax.numpy as jnp
from jax import lax
from jax.experimental import pallas as pl
from jax.experimental.pallas import tpu as pltpu

HEAD_DIM = 64
ROT_DIM = HEAD_DIM // 4
ROPE_THETA = 500000.0
MOBA_BLOCK = 256
MOBA_TOPK = 3
B_CONFIGS = ((128, 1), (512, 4), (2048, 16))
EPS = 1e-6
NEG = -1e30
LOG2_E = 1.4426950408889634

LANES = 128
HEADS_PER_TILE = LANES // HEAD_DIM
MXU_COLS = 256
ROWS_LONG, ROWS_SHORT = 1024, 512
DIL_ITEMS = 4
VMEM_LIMIT = 56 * 1024 * 1024

F32 = jnp.float32
BF16 = jnp.bfloat16


def _params(*sem):
    return pltpu.CompilerParams(dimension_semantics=sem, vmem_limit_bytes=VMEM_LIMIT)


def _dot(a, b):
    return jnp.dot(a, b, preferred_element_type=F32)


def _dot_nt(a, b):
    return lax.dot_general(a, b, (((1,), (1,)), ((), ())), preferred_element_type=F32)


def _silu(x):
    return x * jax.nn.sigmoid(x)


def _rmsnorm(x, g):
    return x * lax.rsqrt(jnp.mean(x * x, axis=-1, keepdims=True) + EPS) * g


def _rms_mod(x, g, shift, scale):
    return _rmsnorm(x, g) * (1.0 + scale) + shift


def _head_lane_mask(shape, rows_per_head):
    row = lax.broadcasted_iota(jnp.int32, shape, 0)
    lane = lax.broadcasted_iota(jnp.int32, shape, 1)
    return (lane // HEAD_DIM) == (row // rows_per_head)


class _Rows:
    def __init__(self, m, tm, rows_per_seq):
        self.m, self.tm = m, tm
        self.per_row = rows_per_seq < tm
        self.rows_per_seq = rows_per_seq
        self.tiles_per_seq = max(rows_per_seq // tm, 1)

    def prep(self, vecs):
        if self.per_row:
            return jnp.repeat(vecs, self.rows_per_seq, axis=0)
        return vecs[:, None, :]

    def mod_spec(self, d, col):
        if self.per_row:
            return pl.BlockSpec((self.tm, d), lambda i, j: (i, col))
        tps = self.tiles_per_seq
        return pl.BlockSpec((None, 1, d), lambda i, j: (i // tps, 0, col))

    def pos_spec(self, table_rows):
        n_pos_tiles = table_rows // self.tm
        return pl.BlockSpec((self.tm, LANES), lambda i, j: (i % n_pos_tiles, 0))


def _mod_kernel(c_ref, w_ref, b_ref, o_ref):
    cs = _silu(c_ref[...]).astype(BF16)
    o_ref[...] = _dot(cs, w_ref[...].astype(BF16)) + b_ref[...]


def _modulation(c, w, b):
    mc, d = c.shape
    nl, _, n = w.shape
    tn = d
    return pl.pallas_call(
        _mod_kernel,
        grid=(nl, n // tn),
        in_specs=[pl.BlockSpec((mc, d), lambda l, j: (0, 0)),
                  pl.BlockSpec((None, d, tn), lambda l, j: (l, 0, j)),
                  pl.BlockSpec((None, 1, tn), lambda l, j: (l, 0, j))],
        out_specs=pl.BlockSpec((None, mc, tn), lambda l, j: (l, 0, j)),
        out_shape=jax.ShapeDtypeStruct((nl, mc, n), F32),
        compiler_params=_params("arbitrary", "arbitrary"),
        name="modulation",
    )(c, w, b.reshape(nl, 1, n))


def _ffn_kernel(h_ref, sh_ref, sc_ref, gt_ref, g_ref, wg_ref, wu_ref, wd_ref, *rest, final):
    if final:
        fg_ref, o_ref, u_scr, act_scr = rest
    else:
        o_ref, u_scr, act_scr = rest
    d, ff = wg_ref.shape
    tf = MXU_COLS if ff % MXU_COLS == 0 else LANES
    to = min(2 * MXU_COLS, d)
    u_scr[...] = _rms_mod(h_ref[...], g_ref[...], sh_ref[...], sc_ref[...]).astype(BF16)
    for c in range(ff // tf):
        sl = slice(c * tf, (c + 1) * tf)
        u = u_scr[...]
        a = _dot(u, wg_ref[:, sl])
        b = _dot(u, wu_ref[:, sl])
        act_scr[:, sl] = (_silu(a) * b).astype(BF16)
    for c in range(d // to):
        sl = slice(c * to, (c + 1) * to)
        down = _dot(act_scr[...], wd_ref[:, sl])
        o_ref[:, sl] = h_ref[:, sl] + 0.5 * gt_ref[:, sl] * down
    if final:
        o_ref[...] = _rmsnorm(o_ref[...], fg_ref[...])


def _ffn(rows, h, mod, cols, g, wg, wu, wd, l, s, final_g=None):
    m, d = h.shape
    ff = wg.shape[-1]
    tm = rows.tm
    final = final_g is not None
    resident = lambda shape: pl.BlockSpec((None, None) + shape, lambda i, f: (l, s, 0, 0),
                                          pipeline_mode=pl.Buffered(1))
    in_specs = [pl.BlockSpec((tm, d), lambda i, f: (i, 0)),
                rows.mod_spec(d, cols[0]), rows.mod_spec(d, cols[1]), rows.mod_spec(d, cols[2]),
                pl.BlockSpec((1, d), lambda i, f: (0, 0)),
                resident((d, ff)), resident((d, ff)), resident((ff, d))]
    args = [h, mod, mod, mod, g.reshape(1, d), wg, wu, wd]
    if final:
        in_specs.append(pl.BlockSpec((1, d), lambda i, f: (0, 0)))
        args.append(final_g.reshape(1, d))
    return pl.pallas_call(
        functools.partial(_ffn_kernel, final=final),
        grid=(m // tm, 1),
        in_specs=in_specs,
        out_specs=pl.BlockSpec((tm, d), lambda i, f: (i, 0)),
        out_shape=jax.ShapeDtypeStruct((m, d), F32),
        scratch_shapes=[pltpu.VMEM((tm, d), BF16), pltpu.VMEM((tm, ff), BF16)],
        compiler_params=_params("arbitrary", "arbitrary"),
        name="ffn",
    )(*args)


def _rope_tables(pos):
    half = ROT_DIM // 2
    n = pos.shape[0]
    freq = ROPE_THETA ** (-jnp.arange(half, dtype=F32) / half)
    ang = pos.astype(F32)[:, None] * freq[None, :]
    cos, sin = jnp.cos(ang), jnp.sin(ang)
    rest = HEAD_DIM - ROT_DIM
    ct = jnp.concatenate([cos, cos, jnp.ones((n, rest), F32)], axis=1)
    s_lo = jnp.concatenate([-sin, jnp.zeros((n, half + rest), F32)], axis=1)
    s_hi = jnp.concatenate([jnp.zeros((n, half), F32), sin, jnp.zeros((n, rest), F32)], axis=1)
    tile = lambda t: jnp.tile(t, (1, HEADS_PER_TILE))
    return (tile(ct), tile(s_hi), tile(s_lo)), (cos.T, sin.T)


def _rope_tile(x, ct, s_hi, s_lo):
    half = ROT_DIM // 2
    return x * ct + pltpu.roll(x, half, 1) * s_hi + pltpu.roll(x, LANES - half, 1) * s_lo


def _rope_t(x_t, cos_t, sin_t):
    half = ROT_DIM // 2
    pieces = []
    for base in range(0, x_t.shape[0], HEAD_DIM):
        x1, x2 = x_t[base:base + half], x_t[base + half:base + ROT_DIM]
        pieces += [x1 * cos_t - x2 * sin_t, x2 * cos_t + x1 * sin_t, x_t[base + ROT_DIM:base + HEAD_DIM]]
    return jnp.concatenate(pieces, axis=0)


_MEAN_GROUP = 8


def _proj_kernel(h_ref, sh_ref, sc_ref, g_ref, w_ref, ct_ref, shi_ref, slo_ref, cos_t_ref, sin_t_ref,
                 *outs, parts):
    i = pl.program_id(0)
    u = _rms_mod(h_ref[...], g_ref[...], sh_ref[...], sc_ref[...]).astype(BF16)
    tm = u.shape[0]
    n_p = w_ref.shape[1] // len(parts)
    blocks = tm // MOBA_BLOCK
    n_out = 0
    for p, (rope, forms) in enumerate(parts):
        part_outs = outs[n_out:n_out + len(forms)]
        n_out += len(forms)
        y = _dot(u, w_ref[:, p * n_p:(p + 1) * n_p])
        val = val_t = None
        if any(f.startswith("t_") for f in forms):
            val_t = y.T
            if rope:
                val_t = _rope_t(val_t, cos_t_ref[...], sin_t_ref[...])
            if any(not f.startswith("t_") for f in forms):
                val = val_t.T
        elif rope:
            ct, s_hi, s_lo = ct_ref[...], shi_ref[...], slo_ref[...]
            val = jnp.concatenate([_rope_tile(y[:, c * LANES:(c + 1) * LANES], ct, s_hi, s_lo)
                                   for c in range(n_p // LANES)], axis=1)
        else:
            val = y
        for form, out in zip(forms, part_outs):
            if form == "f32":
                out[...] = val
            elif form == "bf16":
                out[...] = val.astype(BF16)
            elif form == "t_f32":
                out[...] = val_t
            elif form == "t_bf16":
                for c in range(blocks):
                    out[c] = val_t[:, c * MOBA_BLOCK:(c + 1) * MOBA_BLOCK].astype(BF16)
            elif form == "block_mean":
                base = (i % (_MEAN_GROUP // blocks)) * blocks
                for c in range(blocks):
                    out[pl.ds(base + c, 1), :] = jnp.mean(
                        val[c * MOBA_BLOCK:(c + 1) * MOBA_BLOCK], axis=0, keepdims=True)


def _proj(rows, h, mod, cols, g, w, tables, parts):
    m, d = h.shape
    n_p = w.shape[1] // len(parts)
    tm, tps, t = rows.tm, rows.tiles_per_seq, rows.rows_per_seq
    n_seq = m // t
    out_specs, out_shape = [], []
    for _, forms in parts:
        for form in forms:
            if form in ("f32", "bf16"):
                out_specs.append(pl.BlockSpec((tm, n_p), lambda i, j: (i, 0)))
                out_shape.append(jax.ShapeDtypeStruct((m, n_p), F32 if form == "f32" else BF16))
                continue
            assert not rows.per_row and tm % MOBA_BLOCK == 0
            if form == "t_f32":
                out_specs.append(pl.BlockSpec((None, n_p, tm), lambda i, j: (i // tps, 0, i % tps)))
                out_shape.append(jax.ShapeDtypeStruct((n_seq, n_p, t), F32))
            elif form == "t_bf16":
                out_specs.append(pl.BlockSpec((None, tm // MOBA_BLOCK, n_p, MOBA_BLOCK),
                                              lambda i, j: (i // tps, i % tps, 0, 0)))
                out_shape.append(jax.ShapeDtypeStruct((n_seq, t // MOBA_BLOCK, n_p, MOBA_BLOCK), BF16))
            else:
                steps = _MEAN_GROUP * MOBA_BLOCK // tm
                assert form == "block_mean" and tps % steps == 0
                out_specs.append(pl.BlockSpec((None, _MEAN_GROUP, n_p),
                                              lambda i, j: (i // tps, (i % tps) // steps, 0)))
                out_shape.append(jax.ShapeDtypeStruct((n_seq, t // MOBA_BLOCK, n_p), F32))
    lane_tables, t_tables = tables
    pos_spec = rows.pos_spec(lane_tables[0].shape[0])
    n_pos_tiles = lane_tables[0].shape[0] // tm
    pos_t_spec = pl.BlockSpec((ROT_DIM // 2, tm), lambda i, j: (0, i % n_pos_tiles))
    return pl.pallas_call(
        functools.partial(_proj_kernel, parts=parts),
        grid=(m // tm, 1),
        in_specs=[pl.BlockSpec((tm, d), lambda i, j: (i, 0)),
                  rows.mod_spec(d, cols[0]), rows.mod_spec(d, cols[1]),
                  pl.BlockSpec((1, d), lambda i, j: (0, 0)),
                  pl.BlockSpec(w.shape, lambda i, j: (0, 0)),
                  pos_spec, pos_spec, pos_spec, pos_t_spec, pos_t_spec],
        out_specs=out_specs,
        out_shape=out_shape,
        compiler_params=_params("arbitrary", "arbitrary"),
        name="proj",
    )(h, mod, mod, g.reshape(1, d), w, *lane_tables, *t_tables)


def _oproj_kernel(h_ref, gt_ref, w_ref, *rest, n_groups):
    o_ref = rest[-1]
    if n_groups == 0:
        o = rest[0][...]
    else:
        os_, lses = rest[:n_groups], rest[n_groups:2 * n_groups]
        lse = [r[...] for r in lses]
        mx = functools.reduce(jnp.maximum, lse)
        e = [jnp.exp(x - mx) for x in lse]
        den = functools.reduce(lambda a, b: a + b, e)
        o = functools.reduce(lambda a, b: a + b, [(ei / den) * r[...] for ei, r in zip(e, os_)])
    o_ref[...] = h_ref[...] + gt_ref[...] * _dot(o.astype(BF16), w_ref[...])


def _oproj(rows, h, mod, col, w, o_list, lse_list=()):
    m, d = h.shape
    k = w.shape[0]
    tm = rows.tm
    xs = list(o_list) + list(lse_list)
    return pl.pallas_call(
        functools.partial(_oproj_kernel, n_groups=len(lse_list)),
        grid=(m // tm, 1),
        in_specs=[pl.BlockSpec((tm, d), lambda i, j: (i, 0)),
                  rows.mod_spec(d, col),
                  pl.BlockSpec((k, d), lambda i, j: (0, 0))]
                 + [pl.BlockSpec((tm, k), lambda i, j: (i, 0))] * len(xs),
        out_specs=pl.BlockSpec((tm, d), lambda i, j: (i, 0)),
        out_shape=jax.ShapeDtypeStruct((m, d), F32),
        compiler_params=_params("arbitrary", "arbitrary"),
        name="oproj",
    )(h, mod, w, *xs)


def _split_bf16(x):
    hi = x.astype(BF16)
    lo = (x - hi.astype(F32)).astype(BF16)
    return hi, lo


def _top_blocks_t(gate, n_past):
    blk_id = lax.broadcasted_iota(jnp.int32, gate.shape, 0).astype(F32)
    n_past = n_past.astype(F32)
    g = jnp.where(blk_id < n_past, gate, NEG)
    sel = jnp.zeros(gate.shape, jnp.bool_)
    for _ in range(MOBA_TOPK):
        mx = jnp.max(g, axis=0, keepdims=True)
        idx = jnp.min(jnp.where(g == mx, blk_id, float(gate.shape[0])), axis=0, keepdims=True)
        pick = blk_id == idx
        sel = sel | pick
        g = jnp.where(pick, -jnp.inf, g)
    return sel & (blk_id < n_past)


def _moba_prompt_kernel(q_ref, k_ref, vt_ref, km_ref, o_ref,
                        qt_scr, sel_scr, m_scr, l_scr, acc_scr, s_own, s_even, s_odd):
    blk = MOBA_BLOCK
    n_h = qt_scr.shape[0]
    tiles = [slice(hh // HEADS_PER_TILE * LANES, (hh // HEADS_PER_TILE + 1) * LANES) for hh in range(n_h)]
    nb = vt_ref.shape[0]
    i = pl.program_id(2)
    heads = range(n_h)
    hs = [slice(hh * HEAD_DIM, (hh + 1) * HEAD_DIM) for hh in heads]

    def scores_into(dst, first, n_blocks):
        rows = n_blocks * blk
        jj = jnp.minimum(first, nb - n_blocks)
        kb = k_ref[pl.ds(pl.multiple_of(jj * blk, blk), rows), :]
        for hh in heads:
            dst[hh] = _dot(kb[:, tiles[hh]], qt_scr[hh])

    def consume(src, part, j, own):
        jj = jnp.minimum(j, nb - 1)
        vtb = vt_ref[jj]
        ps, alphas, chosen = [], [], []
        for hh in heads:
            s = src[hh, part * blk:(part + 1) * blk, :]
            if own:
                key_id = lax.broadcasted_iota(jnp.int32, (blk, blk), 0)
                qry_id = lax.broadcasted_iota(jnp.int32, (blk, blk), 1)
                s = jnp.where(key_id <= qry_id, s, NEG)
                ch = jnp.full((1, blk), True)
            else:
                ch = (sel_scr[hh, pl.ds(jj, 1), :] > 0.5) & (j < i)
            m_old = m_scr[hh]
            m_all = jnp.maximum(m_old, jnp.max(s, axis=0, keepdims=True))
            p = jnp.exp2(s - m_all)
            m_new = jnp.where(ch, m_all, m_old)
            alpha = jnp.exp2(m_old - m_new)
            l_scr[hh] = alpha * l_scr[hh] + jnp.where(ch, jnp.sum(p, axis=0, keepdims=True), 0.0)
            m_scr[hh] = m_new
            ps.append(p.astype(BF16))
            alphas.append(alpha)
            chosen.append(ch)
        pvs = [_dot(vtb[hs[hh], :], ps[hh]) for hh in heads]
        for hh in heads:
            acc_scr[hh] = alphas[hh] * acc_scr[hh] + jnp.where(chosen[hh], pvs[hh], 0.0)

    feat = lax.broadcasted_iota(jnp.int32, (LANES, blk), 0)
    for hh in heads:
        qth = jnp.where((feat // HEAD_DIM) == hh % HEADS_PER_TILE, q_ref[tiles[hh], :], 0.0)
        qt_scr[hh] = (qth * (HEAD_DIM ** -0.5 * LOG2_E)).astype(BF16)
        q_hi, q_lo = _split_bf16(qth)
        km_hi, km_lo = _split_bf16(km_ref[:, tiles[hh]])
        by_q_hi = _dot(jnp.concatenate([km_hi, km_lo], axis=0), q_hi)
        gate = by_q_hi[:nb] + by_q_hi[nb:] + _dot(km_hi, q_lo)
        sel_scr[hh] = _top_blocks_t(gate, i).astype(F32)
        m_scr[hh] = jnp.full((1, blk), NEG, F32)
        l_scr[hh] = jnp.zeros((1, blk), F32)
        acc_scr[hh] = jnp.zeros((HEAD_DIM, blk), F32)
    scores_into(s_own, i, 1)
    scores_into(s_even, 0, 2)
    consume(s_own, 0, i, own=True)

    def pair_from(cur, nxt):
        def run(c):
            scores_into(nxt, 2 * c + 2, 2)
            consume(cur, 0, 2 * c, own=False)
            consume(cur, 1, 2 * c + 1, own=False)
        return run

    def two_blocks(c, carry):
        lax.cond(c % 2 == 0, pair_from(s_even, s_odd), pair_from(s_odd, s_even), c)
        return carry

    lax.fori_loop(0, (i + 1) // 2, two_blocks, 0)
    ot = jnp.concatenate([acc_scr[hh] / l_scr[hh] for hh in heads], axis=0)
    o_ref[...] = ot.T


def _moba_prompt(q_t, k_bf, vt_bf, km):
    b, d, t = q_t.shape
    blk = MOBA_BLOCK
    nb = t // blk
    width = min(d, 2 * LANES)
    n_h = width // HEAD_DIM
    assert t % blk == 0 and nb % 8 == 0 and d % width == 0
    tile_spec = pl.BlockSpec((None, blk, width), lambda bi, hg, i: (bi, i, hg))
    return pl.pallas_call(
        _moba_prompt_kernel,
        grid=(b, d // width, nb),
        in_specs=[pl.BlockSpec((None, width, blk), lambda bi, hg, i: (bi, hg, i)),
                  pl.BlockSpec((None, t, width), lambda bi, hg, i: (bi, 0, hg)),
                  pl.BlockSpec((None, nb, width, blk), lambda bi, hg, i: (bi, 0, hg, 0)),
                  pl.BlockSpec((None, nb, width), lambda bi, hg, i: (bi, 0, hg))],
        out_specs=tile_spec,
        out_shape=jax.ShapeDtypeStruct((b, t, d), F32),
        scratch_shapes=[pltpu.VMEM((n_h, LANES, blk), BF16),
                        pltpu.VMEM((n_h, nb, blk), F32),
                        pltpu.VMEM((n_h, 1, blk), F32),
                        pltpu.VMEM((n_h, 1, blk), F32),
                        pltpu.VMEM((n_h, HEAD_DIM, blk), F32)]
                       + [pltpu.VMEM((n_h, blk, blk), F32)] + [pltpu.VMEM((n_h, 2 * blk, blk), F32)] * 2,
        compiler_params=_params("arbitrary", "arbitrary", "arbitrary"),
        name="moba_prompt",
    )(q_t, k_bf, vt_bf, km)


def _pad_rows(x, n):
    return jnp.concatenate([x, jnp.zeros((n - x.shape[0], x.shape[1]), x.dtype)], axis=0)


def _moba_sample_kernel(pt_ref, q_ref, kn_ref, vn_ref, *rest, n_pages, page, n_heads):
    k_pages, v_pages = rest[:n_pages], rest[n_pages:2 * n_pages]
    o_ref, s_scr = rest[2 * n_pages], rest[2 * n_pages + 1]
    t_dec, d = q_ref.shape
    r = n_heads * t_dec
    ppb = MOBA_BLOCK // page
    n_past = n_pages // ppb
    own = lax.broadcasted_iota(jnp.int32, (r, d), 0) // t_dec
    head_mask = (lax.broadcasted_iota(jnp.int32, (r, d), 1) // HEAD_DIM) == own
    qrows = jnp.where(head_mask, jnp.concatenate([q_ref[...]] * n_heads, axis=0), 0.0)
    qs = (qrows * HEAD_DIM ** -0.5).astype(BF16)

    def block_of(pages, j):
        return jnp.concatenate([pages[p][...].astype(BF16) for p in range(j * ppb, (j + 1) * ppb)], axis=1)

    gates = []
    for j in range(n_past):
        sl = slice(j * MOBA_BLOCK, (j + 1) * MOBA_BLOCK)
        s_scr[:, sl] = _dot(qs, block_of(k_pages, j))
        gates.append(jnp.sum(s_scr[:, sl], axis=1, keepdims=True))
    for j in range(n_past):
        rank = jnp.zeros((r, 1), jnp.int32)
        for j2 in range(n_past):
            if j2 != j:
                ahead = (gates[j2] > gates[j]) | ((gates[j2] == gates[j]) & (j2 < j))
                rank = rank + ahead.astype(jnp.int32)
        chosen = rank < MOBA_TOPK
        sl = slice(j * MOBA_BLOCK, (j + 1) * MOBA_BLOCK)
        s_scr[:, sl] = jnp.where(chosen, s_scr[:, sl], NEG)
    t_row = lax.broadcasted_iota(jnp.int32, (r, page), 0) % t_dec
    t_col = lax.broadcasted_iota(jnp.int32, (r, page), 1)
    s_own = _dot_nt(qs, _pad_rows(kn_ref[...], page).astype(BF16))
    s_scr[:, n_pages * page:] = jnp.where(t_col <= t_row, s_own, NEG)

    s = s_scr[...]
    m = jnp.max(s, axis=1, keepdims=True)
    p_all = jnp.exp(s - m)
    l = jnp.sum(p_all, axis=1, keepdims=True)
    acc = _dot(p_all[:, n_pages * page:].astype(BF16), _pad_rows(vn_ref[...], page).astype(BF16))
    for j in range(n_past):
        acc = acc + _dot_nt(p_all[:, j * MOBA_BLOCK:(j + 1) * MOBA_BLOCK].astype(BF16), block_of(v_pages, j))
    o_full = jnp.where(head_mask, acc / l, 0.0)
    out = o_full[0:t_dec]
    for h in range(1, n_heads):
        out = out + o_full[h * t_dec:(h + 1) * t_dec]
    o_ref[...] = out


def _moba_sample(q, k_new, v_new, cache_kt, cache_vt, layer, page_table):
    b, t_dec, d = q.shape
    n_pages = page_table.shape[1]
    page = cache_kt.shape[3]
    n_heads = d // HEAD_DIM
    assert MOBA_BLOCK % page == 0 and (n_pages * page) % MOBA_BLOCK == 0 and t_dec <= page
    tok_spec = pl.BlockSpec((None, t_dec, d), lambda i, pt: (i, 0, 0))
    page_specs = [pl.BlockSpec((None, None, d, page),
                               functools.partial(lambda i, pt, p: (layer, pt[i, p], 0, 0), p=p))
                  for p in range(n_pages)]
    grid_spec = pltpu.PrefetchScalarGridSpec(
        num_scalar_prefetch=1,
        grid=(b,),
        in_specs=[tok_spec, tok_spec, tok_spec] + page_specs + page_specs,
        out_specs=tok_spec,
        scratch_shapes=[pltpu.VMEM((n_heads * t_dec, (n_pages + 1) * page), F32)],
    )
    return pl.pallas_call(
        functools.partial(_moba_sample_kernel, n_pages=n_pages, page=page, n_heads=n_heads),
        grid_spec=grid_spec,
        out_shape=jax.ShapeDtypeStruct((b, t_dec, d), F32),
        compiler_params=_params("arbitrary"),
        name="moba_sample",
    )(page_table, q, k_new, v_new, *([cache_kt] * n_pages), *([cache_vt] * n_pages))


def _dil_prompt_kernel(q_ref, k_ref, v_ref, o_ref, lse_ref, *, dil, tq, sub, width, win):
    length = k_ref.shape[0] // dil
    base = pl.program_id(2) * (sub * tq)
    lane = lax.broadcasted_iota(jnp.int32, (tq, LANES), 1)
    row_col = (lax.broadcasted_iota(jnp.int32, (tq, width), 0)
               - lax.broadcasted_iota(jnp.int32, (tq, width), 1))
    heads = range(HEADS_PER_TILE)

    def rows(start, n):
        return pl.ds(start, n) if dil == 1 else pl.ds(start, n, stride=dil)

    def attend(items):
        qs, kws, vws, valids = [], [], [], []
        for r, u in items:
            q0 = base + u * tq
            ks = jnp.clip(q0 - win, 0, length - width)
            q = q_ref[rows(u * tq * dil + r, tq), :]
            qs.append([(jnp.where((lane // HEAD_DIM) == hh, q, 0.0) * (HEAD_DIM ** -0.5 * LOG2_E)).astype(BF16)
                       for hh in heads])
            kws.append(k_ref[rows(ks * dil + r, width), :].astype(BF16))
            vws.append(v_ref[rows(ks * dil + r, width), :].astype(BF16))
            delta = q0 - ks + row_col
            valids.append((delta >= 0) & (delta <= win))
        n = range(len(items))
        scores = [_dot_nt(jnp.concatenate(qs[it], axis=0), kws[it]) for it in n]
        ps, ls, ms = [], [], []
        for it in n:
            for hh in heads:
                s = jnp.where(valids[it], scores[it][hh * tq:(hh + 1) * tq], NEG)
                m = jnp.max(s, axis=1, keepdims=True)
                p = jnp.exp2(s - m)
                ls.append(jnp.sum(p, axis=1, keepdims=True))
                ms.append(m * (1.0 / LOG2_E))
                ps.append(p.astype(BF16))
        stacked = [_dot(jnp.concatenate(ps[it * len(heads):(it + 1) * len(heads)], axis=0), vws[it]) for it in n]
        pvs = [stacked[it][hh * tq:(hh + 1) * tq] for it in n for hh in heads]
        for it, (r, u) in enumerate(items):
            c0, c1 = it * len(heads), it * len(heads) + 1
            o_ref[rows(u * tq * dil + r, tq), :] = jnp.where(lane < HEAD_DIM, pvs[c0] / ls[c0], pvs[c1] / ls[c1])
            lse_ref[rows(u * tq * dil + r, tq), :] = jnp.where(
                lane < HEAD_DIM, jnp.broadcast_to(ms[c0] + jnp.log(ls[c0]), (tq, LANES)),
                jnp.broadcast_to(ms[c1] + jnp.log(ls[c1]), (tq, LANES)))

    if dil == 1:
        attend([(0, u) for u in range(sub)])
    else:
        per_iter = min(dil, DIL_ITEMS)

        def some_classes(c, carry):
            attend([(per_iter * c + x, u) for x in range(per_iter) for u in range(sub)])
            return carry
        lax.fori_loop(0, dil // per_iter, some_classes, 0)


def _dil_prompt(q, k, v, window, dil):
    b, t, dk = q.shape
    assert t % dil == 0 and HEADS_PER_TILE == 2 and (dil == 1 or dil % 2 == 0)
    length = t // dil
    win = window // dil
    tq = min(LANES, length)
    sub = DIL_ITEMS if dil == 1 and length % (DIL_ITEMS * tq) == 0 else 1
    assert dil == 1 or dil % min(dil, DIL_ITEMS) == 0
    width = min(tq + win, length)
    assert length % (sub * tq) == 0
    tile_spec = pl.BlockSpec((None, sub * tq * dil, LANES), lambda bi, h, i: (bi, i, h))
    seq_spec = pl.BlockSpec((None, t, LANES), lambda bi, h, i: (bi, 0, h))
    return pl.pallas_call(
        functools.partial(_dil_prompt_kernel, dil=dil, tq=tq, sub=sub, width=width, win=win),
        grid=(b, dk // LANES, length // (sub * tq)),
        in_specs=[tile_spec, seq_spec, seq_spec],
        out_specs=[tile_spec, tile_spec],
        out_shape=[jax.ShapeDtypeStruct((b, t, dk), F32)] * 2,
        compiler_params=_params("arbitrary", "arbitrary", "arbitrary"),
        name="dilated_prompt",
    )(q, k, v)


def _dil_sample_kernel(q0_ref, q1_ref, q2_ref, kn_ref, vn_ref, kc_ref, vc_ref, o_ref, s_scr, p_scr, *, n_heads):
    q_refs = (q0_ref, q1_ref, q2_ref)
    t_dec, dk = kn_ref.shape
    past = kc_ref.shape[1]
    pad = LANES
    rg = n_heads * t_dec
    head_mask = _head_lane_mask((rg, dk), t_dec)
    qrows = jnp.concatenate(
        [jnp.where(head_mask, jnp.concatenate([qr[...]] * n_heads, axis=0), 0.0) for qr in q_refs], axis=0)
    qs = (qrows * HEAD_DIM ** -0.5).astype(BF16)
    s_scr[:, :past] = _dot(qs, kc_ref[...].astype(BF16))
    s_scr[:, past:] = _dot_nt(qs, _pad_rows(kn_ref[...], pad).astype(BF16))

    t_row = lax.broadcasted_iota(jnp.int32, (rg, past + pad), 0) % t_dec
    col = lax.broadcasted_iota(jnp.int32, (rg, past + pad), 1)
    delta = past + t_row - col
    in_range = col < past + t_dec
    l_g, lse_g = [], []
    for g, (window, dil) in enumerate(B_CONFIGS):
        rs = slice(g * rg, (g + 1) * rg)
        assert dil & (dil - 1) == 0
        valid = in_range & (delta >= 0) & (delta <= window) & ((delta & (dil - 1)) == 0)
        s = jnp.where(valid, s_scr[rs, :], NEG)
        m = jnp.max(s, axis=1, keepdims=True)
        p = jnp.exp(s - m)
        l = jnp.sum(p, axis=1, keepdims=True)
        p_scr[rs, :] = p.astype(BF16)
        l_g.append(l)
        lse_g.append(m + jnp.log(l))
    acc = (_dot_nt(p_scr[:, :past], vc_ref[...].astype(BF16))
           + _dot(p_scr[:, past:], _pad_rows(vn_ref[...], pad).astype(BF16)))
    mx = functools.reduce(jnp.maximum, lse_g)
    e = [jnp.exp(x - mx) for x in lse_g]
    den = functools.reduce(lambda a, b: a + b, e)
    mixed = jnp.zeros((rg, dk), F32)
    for g in range(len(B_CONFIGS)):
        mixed = mixed + (e[g] / den) * (acc[g * rg:(g + 1) * rg] / l_g[g])
    mixed = jnp.where(head_mask, mixed, 0.0)
    out = mixed[0:t_dec]
    for h in range(1, n_heads):
        out = out + mixed[h * t_dec:(h + 1) * t_dec]
    o_ref[...] = out


def _dil_sample(q_groups, k_new, v_new, cache_kt, cache_vt):
    b, t_dec, dk = k_new.shape
    past = cache_kt.shape[2]
    n_heads = dk // HEAD_DIM
    n_rows = len(B_CONFIGS) * n_heads * t_dec
    tok_spec = pl.BlockSpec((None, t_dec, dk), lambda i: (i, 0, 0))
    cache_spec = pl.BlockSpec((None, dk, past), lambda i: (i, 0, 0))
    return pl.pallas_call(
        functools.partial(_dil_sample_kernel, n_heads=n_heads),
        grid=(b,),
        in_specs=[tok_spec] * 5 + [cache_spec] * 2,
        out_specs=tok_spec,
        out_shape=jax.ShapeDtypeStruct((b, t_dec, dk), F32),
        scratch_shapes=[pltpu.VMEM((n_rows, past + LANES), F32),
                        pltpu.VMEM((n_rows, past + LANES), BF16)],
        compiler_params=_params("arbitrary"),
        name="dilated_sample",
    )(*q_groups, k_new, v_new, cache_kt, cache_vt)


def _trunk(x, mods, kv_mod, pos, past, weights):
    (norm_g, w_ffn_gate, w_ffn_up, w_ffn_down, w_qkv_a, w_o_a,
     kv_norm_g, w_kv_b, w_q_b, w_o_b, final_norm_g) = weights
    n_seq, t, d = x.shape
    m = n_seq * t
    depth = norm_g.shape[0]
    n_a = w_qkv_a.shape[0]
    dk = w_kv_b.shape[1] // 2
    is_prompt = past is None
    short = t < 256
    rows = _Rows(m, min(ROWS_LONG if is_prompt else ROWS_SHORT, m if short else t), t)
    rows_p = _Rows(m, min(ROWS_SHORT, m if short else t), t)
    tables = _rope_tables(pos)
    if rows.per_row:
        tables = (tuple(jnp.tile(tb, (n_seq, 1)) for tb in tables[0]),
                  tuple(jnp.tile(tb, (1, n_seq)) for tb in tables[1]))

    def heads_last(x_t):
        return jnp.transpose(x_t.reshape(n_seq, -1, HEAD_DIM, x_t.shape[-1]), (0, 3, 1, 2))

    h = x.reshape(m, d)
    a_k, a_v = [], []
    b_k = b_v = b_k_out = b_v_out = None
    for l in range(depth):
        if l == n_a:
            kvm = rows.prep(kv_mod)
            if is_prompt:
                b_k, b_kt, b_v, b_vt = _proj(rows_p, h, kvm, (0, 1), kv_norm_g, w_kv_b, tables,
                                             ((True, ("f32", "t_f32")), (False, ("f32", "t_f32"))))
                b_k_out, b_v_out = heads_last(b_kt), heads_last(b_vt)
            else:
                b_k, b_v = _proj(rows_p, h, kvm, (0, 1), kv_norm_g, w_kv_b, tables,
                                 ((True, ("f32",)), (False, ("f32",))))
                b_k_out, b_v_out = (r.reshape(n_seq, t, -1, HEAD_DIM) for r in (b_k, b_v))
        md = rows.prep(mods[l])
        h = _ffn(rows, h, md, (0, 1, 2), norm_g[l, 0], w_ffn_gate, w_ffn_up, w_ffn_down, l, 0)
        if l < n_a:
            shp = (n_seq, t, d)
            if is_prompt:
                q_t, k_t, k_bf, km, v_t, vt_bf = _proj(
                    rows_p, h, md, (3, 4), norm_g[l, 1], w_qkv_a[l], tables,
                    ((True, ("t_f32",)), (True, ("t_f32", "bf16", "block_mean")), (False, ("t_f32", "t_bf16"))))
                o = _moba_prompt(q_t, k_bf.reshape(shp), vt_bf, km)
                a_k.append(heads_last(k_t))
                a_v.append(heads_last(v_t))
            else:
                q, k, v = _proj(rows_p, h, md, (3, 4), norm_g[l, 1], w_qkv_a[l], tables,
                                ((True, ("f32",)), (True, ("f32",)), (False, ("f32",))))
                cache_kt, cache_vt = (
                    jnp.transpose(c, (0, 1, 3, 4, 2)).reshape(c.shape[0], c.shape[1], d, c.shape[2])
                    for c in past[:2])
                o = _moba_sample(q.reshape(shp), k.reshape(shp), v.reshape(shp), cache_kt, cache_vt, l, past[2])
                a_k.append(k.reshape(n_seq, t, -1, HEAD_DIM))
                a_v.append(v.reshape(n_seq, t, -1, HEAD_DIM))
            h = _oproj(rows, h, md, 5,w_o_a[l], [o.reshape(m, d)])
        else:
            lb = l - n_a
            qg = _proj(rows_p, h, md, (3, 4), norm_g[l, 1], w_q_b[lb], tables,
                       ((True, ("f32",)),) * len(B_CONFIGS))
            shp = (n_seq, t, dk)
            if is_prompt:
                res = [_dil_prompt(qi.reshape(shp), b_k.reshape(shp), b_v.reshape(shp), w, dl)
                       for qi, (w, dl) in zip(qg, B_CONFIGS)]
                h = _oproj(rows, h, md, 5,w_o_b[lb], [r[0].reshape(m, dk) for r in res],
                           [r[1].reshape(m, dk) for r in res])
            else:
                cb_kt, cb_vt = (jnp.transpose(c, (0, 2, 3, 1)).reshape(n_seq, dk, c.shape[1]) for c in past[3:5])
                o = _dil_sample([qi.reshape(shp) for qi in qg], b_k.reshape(shp), b_v.reshape(shp), cb_kt, cb_vt)
                h = _oproj(rows, h, md, 5,w_o_b[lb], [o.reshape(m, dk)])
        h = _ffn(rows, h, md, (6, 7, 8), norm_g[l, 2], w_ffn_gate, w_ffn_up, w_ffn_down, l, 1,
                 final_g=final_norm_g if l == depth - 1 else None)
    return h.reshape(n_seq, t, d), jnp.stack(a_k), jnp.stack(a_v), b_k_out, b_v_out


def kernel(x_prompt, x_sample, cache_a_k, cache_a_v, cache_b_k, cache_b_v, page_table, c_prompt, c_sample, norm_g, w_mod, b_mod, w_ffn_gate, w_ffn_up, w_ffn_down, w_qkv_a, w_o_a, kv_norm_g, w_kv_mod, b_kv_mod, w_kv_b, w_q_b, w_o_b, final_norm_g):
    bp, t, d = x_prompt.shape
    bs, t_dec, _ = x_sample.shape
    past_len = page_table.shape[1] * cache_a_k.shape[2]
    assert past_len % MOBA_BLOCK == 0 and t_dec <= MOBA_BLOCK

    n_c = bp + bs
    c_all = _pad_rows(jnp.concatenate([c_prompt, c_sample], axis=0), -(-n_c // 8) * 8)
    mods = _modulation(c_all, w_mod, b_mod)
    kv_mod = _modulation(c_all, w_kv_mod[None], b_kv_mod[None])[0]

    bf = lambda w: w.astype(BF16)
    weights = (norm_g, bf(w_ffn_gate), bf(w_ffn_up), bf(w_ffn_down), bf(w_qkv_a), bf(w_o_a),
               kv_norm_g, bf(w_kv_b), bf(w_q_b), bf(w_o_b), final_norm_g)
    y_p, ak_p, av_p, bk_p, bv_p = _trunk(
        x_prompt, mods[:, :bp], kv_mod[:bp], jnp.arange(t, dtype=jnp.int32), None, weights)
    y_s, ak_s, av_s, bk_s, bv_s = _trunk(
        x_sample, mods[:, bp:n_c], kv_mod[bp:n_c], past_len + jnp.arange(t_dec, dtype=jnp.int32),
        (cache_a_k, cache_a_v, page_table, cache_b_k, cache_b_v), weights)

    keep = min(max(w for w, _ in B_CONFIGS), t)
    return (y_p, y_s, ak_p, av_p, ak_s, av_s, bk_p[:, -keep:], bv_p[:, -keep:], bk_s, bv_s)
```

```python
import functools

import jax
import jax.numpy as jnp
from jax import lax
from jax.experimental import pallas as pl
from jax.experimental.pallas import tpu as pltpu

HEAD_DIM = 64
ROT_DIM = HEAD_DIM // 4
ROPE_THETA = 500000.0
MOBA_BLOCK = 256
MOBA_TOPK = 3
B_CONFIGS = ((128, 1), (512, 4), (2048, 16))
EPS = 1e-6
NEG = -1e30
LOG2_E = 1.4426950408889634

LANES = 128
HEADS_PER_TILE = LANES // HEAD_DIM
MXU_COLS = 256
ROWS_LONG, ROWS_SHORT = 1024, 512
DIL_ITEMS = 4
VMEM_LIMIT = 56 * 1024 * 1024

F32 = jnp.float32
BF16 = jnp.bfloat16


def _params(*sem):
    return pltpu.CompilerParams(dimension_semantics=sem, vmem_limit_bytes=VMEM_LIMIT)


def _dot(a, b):
    return jnp.dot(a, b, preferred_element_type=F32)


def _dot_nt(a, b):
    return lax.dot_general(a, b, (((1,), (1,)), ((), ())), preferred_element_type=F32)


def _silu(x):
    return x * jax.nn.sigmoid(x)


def _rmsnorm(x, g):
    return x * lax.rsqrt(jnp.mean(x * x, axis=-1, keepdims=True) + EPS) * g


def _rms_mod(x, g, shift, scale):
    return _rmsnorm(x, g) * (1.0 + scale) + shift


def _head_lane_mask(shape, rows_per_head):
    row = lax.broadcasted_iota(jnp.int32, shape, 0)
    lane = lax.broadcasted_iota(jnp.int32, shape, 1)
    return (lane // HEAD_DIM) == (row // rows_per_head)


class _Rows:
    def __init__(self, m, tm, rows_per_seq):
        self.m, self.tm = m, tm
        self.per_row = rows_per_seq < tm
        self.rows_per_seq = rows_per_seq
        self.tiles_per_seq = max(rows_per_seq // tm, 1)

    def prep(self, vecs):
        if self.per_row:
            return jnp.repeat(vecs, self.rows_per_seq, axis=0)
        return vecs[:, None, :]

    def mod_spec(self, d, col):
        if self.per_row:
            return pl.BlockSpec((self.tm, d), lambda i, j: (i, col))
        tps = self.tiles_per_seq
        return pl.BlockSpec((None, 1, d), lambda i, j: (i // tps, 0, col))

    def pos_spec(self, table_rows):
        n_pos_tiles = table_rows // self.tm
        return pl.BlockSpec((self.tm, LANES), lambda i, j: (i % n_pos_tiles, 0))


def _mod_kernel(c_ref, w_ref, b_ref, o_ref):
    cs = _silu(c_ref[...]).astype(BF16)
    o_ref[...] = _dot(cs, w_ref[...].astype(BF16)) + b_ref[...]


def _modulation(c, w, b):
    mc, d = c.shape
    nl, _, n = w.shape
    tn = d
    return pl.pallas_call(
        _mod_kernel,
        grid=(nl, n // tn),
        in_specs=[pl.BlockSpec((mc, d), lambda l, j: (0, 0)),
                  pl.BlockSpec((None, d, tn), lambda l, j: (l, 0, j)),
                  pl.BlockSpec((None, 1, tn), lambda l, j: (l, 0, j))],
        out_specs=pl.BlockSpec((None, mc, tn), lambda l, j: (l, 0, j)),
        out_shape=jax.ShapeDtypeStruct((nl, mc, n), F32),
        compiler_params=_params("arbitrary", "arbitrary"),
        name="modulation",
    )(c, w, b.reshape(nl, 1, n))


def _ffn_kernel(h_ref, sh_ref, sc_ref, gt_ref, g_ref, wg_ref, wu_ref, wd_ref, *rest, final):
    if final:
        fg_ref, o_ref, u_scr, act_scr = rest
    else:
        o_ref, u_scr, act_scr = rest
    d, ff = wg_ref.shape
    tf = MXU_COLS if ff % MXU_COLS == 0 else LANES
    to = min(2 * MXU_COLS, d)
    u_scr[...] = _rms_mod(h_ref[...], g_ref[...], sh_ref[...], sc_ref[...]).astype(BF16)
    for c in range(ff // tf):
        sl = slice(c * tf, (c + 1) * tf)
        u = u_scr[...]
        a = _dot(u, wg_ref[:, sl])
        b = _dot(u, wu_ref[:, sl])
        act_scr[:, sl] = (_silu(a) * b).astype(BF16)
    for c in range(d // to):
        sl = slice(c * to, (c + 1) * to)
        down = _dot(act_scr[...], wd_ref[:, sl])
        o_ref[:, sl] = h_ref[:, sl] + 0.5 * gt_ref[:, sl] * down
    if final:
        o_ref[...] = _rmsnorm(o_ref[...], fg_ref[...])


def _ffn(rows, h, mod, cols, g, wg, wu, wd, l, s, final_g=None):
    m, d = h.shape
    ff = wg.shape[-1]
    tm = rows.tm
    final = final_g is not None
    resident = lambda shape: pl.BlockSpec((None, None) + shape, lambda i, f: (l, s, 0, 0),
                                          pipeline_mode=pl.Buffered(1))
    in_specs = [pl.BlockSpec((tm, d), lambda i, f: (i, 0)),
                rows.mod_spec(d, cols[0]), rows.mod_spec(d, cols[1]), rows.mod_spec(d, cols[2]),
                pl.BlockSpec((1, d), lambda i, f: (0, 0)),
                resident((d, ff)), resident((d, ff)), resident((ff, d))]
    args = [h, mod, mod, mod, g.reshape(1, d), wg, wu, wd]
    if final:
        in_specs.append(pl.BlockSpec((1, d), lambda i, f: (0, 0)))
        args.append(final_g.reshape(1, d))
    return pl.pallas_call(
        functools.partial(_ffn_kernel, final=final),
        grid=(m // tm, 1),
        in_specs=in_specs,
        out_specs=pl.BlockSpec((tm, d), lambda i, f: (i, 0)),
        out_shape=jax.ShapeDtypeStruct((m, d), F32),
        scratch_shapes=[pltpu.VMEM((tm, d), BF16), pltpu.VMEM((tm, ff), BF16)],
        compiler_params=_params("arbitrary", "arbitrary"),
        name="ffn",
    )(*args)


def _rope_tables(pos):
    half = ROT_DIM // 2
    n = pos.shape[0]
    freq = ROPE_THETA ** (-jnp.arange(half, dtype=F32) / half)
    ang = pos.astype(F32)[:, None] * freq[None, :]
    cos, sin = jnp.cos(ang), jnp.sin(ang)
    rest = HEAD_DIM - ROT_DIM
    ct = jnp.concatenate([cos, cos, jnp.ones((n, rest), F32)], axis=1)
    s_lo = jnp.concatenate([-sin, jnp.zeros((n, half + rest), F32)], axis=1)
    s_hi = jnp.concatenate([jnp.zeros((n, half), F32), sin, jnp.zeros((n, rest), F32)], axis=1)
    tile = lambda t: jnp.tile(t, (1, HEADS_PER_TILE))
    return (tile(ct), tile(s_hi), tile(s_lo)), (cos.T, sin.T)


def _rope_tile(x, ct, s_hi, s_lo):
    half = ROT_DIM // 2
    return x * ct + pltpu.roll(x, half, 1) * s_hi + pltpu.roll(x, LANES - half, 1) * s_lo


def _rope_t(x_t, cos_t, sin_t):
    half = ROT_DIM // 2
    pieces = []
    for base in range(0, x_t.shape[0], HEAD_DIM):
        x1, x2 = x_t[base:base + half], x_t[base + half:base + ROT_DIM]
        pieces += [x1 * cos_t - x2 * sin_t, x2 * cos_t + x1 * sin_t, x_t[base + ROT_DIM:base + HEAD_DIM]]
    return jnp.concatenate(pieces, axis=0)


_MEAN_GROUP = 8


def _proj_kernel(h_ref, sh_ref, sc_ref, g_ref, w_ref, ct_ref, shi_ref, slo_ref, cos_t_ref, sin_t_ref,
                 *outs, parts):
    i = pl.program_id(0)
    u = _rms_mod(h_ref[...], g_ref[...], sh_ref[...], sc_ref[...]).astype(BF16)
    tm = u.shape[0]
    n_p = w_ref.shape[1] // len(parts)
    blocks = tm // MOBA_BLOCK
    n_out = 0
    for p, (rope, forms) in enumerate(parts):
        part_outs = outs[n_out:n_out + len(forms)]
        n_out += len(forms)
        y = _dot(u, w_ref[:, p * n_p:(p + 1) * n_p])
        val = val_t = None
        if any(f.startswith("t_") for f in forms):
            val_t = y.T
            if rope:
                val_t = _rope_t(val_t, cos_t_ref[...], sin_t_ref[...])
            if any(not f.startswith("t_") for f in forms):
                val = val_t.T
        elif rope:
            ct, s_hi, s_lo = ct_ref[...], shi_ref[...], slo_ref[...]
            val = jnp.concatenate([_rope_tile(y[:, c * LANES:(c + 1) * LANES], ct, s_hi, s_lo)
                                   for c in range(n_p // LANES)], axis=1)
        else:
            val = y
        for form, out in zip(forms, part_outs):
            if form == "f32":
                out[...] = val
            elif form == "bf16":
                out[...] = val.astype(BF16)
            elif form == "t_f32":
                out[...] = val_t
            elif form == "t_bf16":
                for c in range(blocks):
                    out[c] = val_t[:, c * MOBA_BLOCK:(c + 1) * MOBA_BLOCK].astype(BF16)
            elif form == "block_mean":
                base = (i % (_MEAN_GROUP // blocks)) * blocks
                for c in range(blocks):
                    out[pl.ds(base + c, 1), :] = jnp.mean(
                        val[c * MOBA_BLOCK:(c + 1) * MOBA_BLOCK], axis=0, keepdims=True)


def _proj(rows, h, mod, cols, g, w, tables, parts):
    m, d = h.shape
    n_p = w.shape[1] // len(parts)
    tm, tps, t = rows.tm, rows.tiles_per_seq, rows.rows_per_seq
    n_seq = m // t
    out_specs, out_shape = [], []
    for _, forms in parts:
        for form in forms:
            if form in ("f32", "bf16"):
                out_specs.append(pl.BlockSpec((tm, n_p), lambda i, j: (i, 0)))
                out_shape.append(jax.ShapeDtypeStruct((m, n_p), F32 if form == "f32" else BF16))
                continue
            assert not rows.per_row and tm % MOBA_BLOCK == 0
            if form == "t_f32":
                out_specs.append(pl.BlockSpec((None, n_p, tm), lambda i, j: (i // tps, 0, i % tps)))
                out_shape.append(jax.ShapeDtypeStruct((n_seq, n_p, t), F32))
            elif form == "t_bf16":
                out_specs.append(pl.BlockSpec((None, tm // MOBA_BLOCK, n_p, MOBA_BLOCK),
                                              lambda i, j: (i // tps, i % tps, 0, 0)))
                out_shape.append(jax.ShapeDtypeStruct((n_seq, t // MOBA_BLOCK, n_p, MOBA_BLOCK), BF16))
            else:
                steps = _MEAN_GROUP * MOBA_BLOCK // tm
                assert form == "block_mean" and tps % steps == 0
                out_specs.append(pl.BlockSpec((None, _MEAN_GROUP, n_p),
                                              lambda i, j: (i // tps, (i % tps) // steps, 0)))
                out_shape.append(jax.ShapeDtypeStruct((n_seq, t // MOBA_BLOCK, n_p), F32))
    lane_tables, t_tables = tables
    pos_spec = rows.pos_spec(lane_tables[0].shape[0])
    n_pos_tiles = lane_tables[0].shape[0] // tm
    pos_t_spec = pl.BlockSpec((ROT_DIM // 2, tm), lambda i, j: (0, i % n_pos_tiles))
    return pl.pallas_call(
        functools.partial(_proj_kernel, parts=parts),
        grid=(m // tm, 1),
        in_specs=[pl.BlockSpec((tm, d), lambda i, j: (i, 0)),
                  rows.mod_spec(d, cols[0]), rows.mod_spec(d, cols[1]),
                  pl.BlockSpec((1, d), lambda i, j: (0, 0)),
                  pl.BlockSpec(w.shape, lambda i, j: (0, 0)),
                  pos_spec, pos_spec, pos_spec, pos_t_spec, pos_t_spec],
        out_specs=out_specs,
        out_shape=out_shape,
        compiler_params=_params("arbitrary", "arbitrary"),
        name="proj",
    )(h, mod, mod, g.reshape(1, d), w, *lane_tables, *t_tables)


def _oproj_kernel(h_ref, gt_ref, w_ref, *rest, n_groups):
    o_ref = rest[-1]
    if n_groups == 0:
        o = rest[0][...]
    else:
        os_, lses = rest[:n_groups], rest[n_groups:2 * n_groups]
        lse = [r[...] for r in lses]
        mx = functools.reduce(jnp.maximum, lse)
        e = [jnp.exp(x - mx) for x in lse]
        den = functools.reduce(lambda a, b: a + b, e)
        o = functools.reduce(lambda a, b: a + b, [(ei / den) * r[...] for ei, r in zip(e, os_)])
    o_ref[...] = h_ref[...] + gt_ref[...] * _dot(o.astype(BF16), w_ref[...])


def _oproj(rows, h, mod, col, w, o_list, lse_list=()):
    m, d = h.shape
    k = w.shape[0]
    tm = rows.tm
    xs = list(o_list) + list(lse_list)
    return pl.pallas_call(
        functools.partial(_oproj_kernel, n_groups=len(lse_list)),
        grid=(m // tm, 1),
        in_specs=[pl.BlockSpec((tm, d), lambda i, j: (i, 0)),
                  rows.mod_spec(d, col),
                  pl.BlockSpec((k, d), lambda i, j: (0, 0))]
                 + [pl.BlockSpec((tm, k), lambda i, j: (i, 0))] * len(xs),
        out_specs=pl.BlockSpec((tm, d), lambda i, j: (i, 0)),
        out_shape=jax.ShapeDtypeStruct((m, d), F32),
        compiler_params=_params("arbitrary", "arbitrary"),
        name="oproj",
    )(h, mod, w, *xs)


def _split_bf16(x):
    hi = x.astype(BF16)
    lo = (x - hi.astype(F32)).astype(BF16)
    return hi, lo


def _top_blocks_t(gate, n_past):
    blk_id = lax.broadcasted_iota(jnp.int32, gate.shape, 0).astype(F32)
    n_past = n_past.astype(F32)
    g = jnp.where(blk_id < n_past, gate, NEG)
    sel = jnp.zeros(gate.shape, jnp.bool_)
    for _ in range(MOBA_TOPK):
        mx = jnp.max(g, axis=0, keepdims=True)
        idx = jnp.min(jnp.where(g == mx, blk_id, float(gate.shape[0])), axis=0, keepdims=True)
        pick = blk_id == idx
        sel = sel | pick
        g = jnp.where(pick, -jnp.inf, g)
    return sel & (blk_id < n_past)


def _moba_prompt_kernel(q_ref, k_ref, vt_ref, km_ref, o_ref,
                        qt_scr, sel_scr, m_scr, l_scr, acc_scr, s_own, s_even, s_odd):
    blk = MOBA_BLOCK
    n_h = qt_scr.shape[0]
    tiles = [slice(hh // HEADS_PER_TILE * LANES, (hh // HEADS_PER_TILE + 1) * LANES) for hh in range(n_h)]
    nb = vt_ref.shape[0]
    i = pl.program_id(2)
    heads = range(n_h)
    hs = [slice(hh * HEAD_DIM, (hh + 1) * HEAD_DIM) for hh in heads]

    def scores_into(dst, first, n_blocks):
        rows = n_blocks * blk
        jj = jnp.minimum(first, nb - n_blocks)
        kb = k_ref[pl.ds(pl.multiple_of(jj * blk, blk), rows), :]
        for hh in heads:
            dst[hh] = _dot(kb[:, tiles[hh]], qt_scr[hh])

    def consume(src, part, j, own):
        jj = jnp.minimum(j, nb - 1)
        vtb = vt_ref[jj]
        ps, alphas, chosen = [], [], []
        for hh in heads:
            s = src[hh, part * blk:(part + 1) * blk, :]
            if own:
                key_id = lax.broadcasted_iota(jnp.int32, (blk, blk), 0)
                qry_id = lax.broadcasted_iota(jnp.int32, (blk, blk), 1)
                s = jnp.where(key_id <= qry_id, s, NEG)
                ch = jnp.full((1, blk), True)
            else:
                ch = (sel_scr[hh, pl.ds(jj, 1), :] > 0.5) & (j < i)
            m_old = m_scr[hh]
            m_all = jnp.maximum(m_old, jnp.max(s, axis=0, keepdims=True))
            p = jnp.exp2(s - m_all)
            m_new = jnp.where(ch, m_all, m_old)
            alpha = jnp.exp2(m_old - m_new)
            l_scr[hh] = alpha * l_scr[hh] + jnp.where(ch, jnp.sum(p, axis=0, keepdims=True), 0.0)
            m_scr[hh] = m_new
            ps.append(p.astype(BF16))
            alphas.append(alpha)
            chosen.append(ch)
        pvs = [_dot(vtb[hs[hh], :], ps[hh]) for hh in heads]
        for hh in heads:
            acc_scr[hh] = alphas[hh] * acc_scr[hh] + jnp.where(chosen[hh], pvs[hh], 0.0)

    feat = lax.broadcasted_iota(jnp.int32, (LANES, blk), 0)
    for hh in heads:
        qth = jnp.where((feat // HEAD_DIM) == hh % HEADS_PER_TILE, q_ref[tiles[hh], :], 0.0)
        qt_scr[hh] = (qth * (HEAD_DIM ** -0.5 * LOG2_E)).astype(BF16)
        q_hi, q_lo = _split_bf16(qth)
        km_hi, km_lo = _split_bf16(km_ref[:, tiles[hh]])
        by_q_hi = _dot(jnp.concatenate([km_hi, km_lo], axis=0), q_hi)
        gate = by_q_hi[:nb] + by_q_hi[nb:] + _dot(km_hi, q_lo)
        sel_scr[hh] = _top_blocks_t(gate, i).astype(F32)
        m_scr[hh] = jnp.full((1, blk), NEG, F32)
        l_scr[hh] = jnp.zeros((1, blk), F32)
        acc_scr[hh] = jnp.zeros((HEAD_DIM, blk), F32)
    scores_into(s_own, i, 1)
    scores_into(s_even, 0, 2)
    consume(s_own, 0, i, own=True)

    def pair_from(cur, nxt):
        def run(c):
            scores_into(nxt, 2 * c + 2, 2)
            consume(cur, 0, 2 * c, own=False)
            consume(cur, 1, 2 * c + 1, own=False)
        return run

    def two_blocks(c, carry):
        lax.cond(c % 2 == 0, pair_from(s_even, s_odd), pair_from(s_odd, s_even), c)
        return carry

    lax.fori_loop(0, (i + 1) // 2, two_blocks, 0)
    ot = jnp.concatenate([acc_scr[hh] / l_scr[hh] for hh in heads], axis=0)
    o_ref[...] = ot.T


def _moba_prompt(q_t, k_bf, vt_bf, km):
    b, d, t = q_t.shape
    blk = MOBA_BLOCK
    nb = t // blk
    width = min(d, 2 * LANES)
    n_h = width // HEAD_DIM
    assert t % blk == 0 and nb % 8 == 0 and d % width == 0
    tile_spec = pl.BlockSpec((None, blk, width), lambda bi, hg, i: (bi, i, hg))
    return pl.pallas_call(
        _moba_prompt_kernel,
        grid=(b, d // width, nb),
        in_specs=[pl.BlockSpec((None, width, blk), lambda bi, hg, i: (bi, hg, i)),
                  pl.BlockSpec((None, t, width), lambda bi, hg, i: (bi, 0, hg)),
                  pl.BlockSpec((None, nb, width, blk), lambda bi, hg, i: (bi, 0, hg, 0)),
                  pl.BlockSpec((None, nb, width), lambda bi, hg, i: (bi, 0, hg))],
        out_specs=tile_spec,
        out_shape=jax.ShapeDtypeStruct((b, t, d), F32),
        scratch_shapes=[pltpu.VMEM((n_h, LANES, blk), BF16),
                        pltpu.VMEM((n_h, nb, blk), F32),
                        pltpu.VMEM((n_h, 1, blk), F32),
                        pltpu.VMEM((n_h, 1, blk), F32),
                        pltpu.VMEM((n_h, HEAD_DIM, blk), F32)]
                       + [pltpu.VMEM((n_h, blk, blk), F32)] + [pltpu.VMEM((n_h, 2 * blk, blk), F32)] * 2,
        compiler_params=_params("arbitrary", "arbitrary", "arbitrary"),
        name="moba_prompt",
    )(q_t, k_bf, vt_bf, km)


def _pad_rows(x, n):
    return jnp.concatenate([x, jnp.zeros((n - x.shape[0], x.shape[1]), x.dtype)], axis=0)


def _moba_sample_kernel(pt_ref, q_ref, kn_ref, vn_ref, *rest, n_pages, page, n_heads):
    k_pages, v_pages = rest[:n_pages], rest[n_pages:2 * n_pages]
    o_ref, s_scr = rest[2 * n_pages], rest[2 * n_pages + 1]
    t_dec, d = q_ref.shape
    r = n_heads * t_dec
    ppb = MOBA_BLOCK // page
    n_past = n_pages // ppb
    own = lax.broadcasted_iota(jnp.int32, (r, d), 0) // t_dec
    head_mask = (lax.broadcasted_iota(jnp.int32, (r, d), 1) // HEAD_DIM) == own
    qrows = jnp.where(head_mask, jnp.concatenate([q_ref[...]] * n_heads, axis=0), 0.0)
    qs = (qrows * HEAD_DIM ** -0.5).astype(BF16)

    def block_of(pages, j):
        return jnp.concatenate([pages[p][...].astype(BF16) for p in range(j * ppb, (j + 1) * ppb)], axis=1)

    gates = []
    for j in range(n_past):
        sl = slice(j * MOBA_BLOCK, (j + 1) * MOBA_BLOCK)
        s_scr[:, sl] = _dot(qs, block_of(k_pages, j))
        gates.append(jnp.sum(s_scr[:, sl], axis=1, keepdims=True))
    for j in range(n_past):
        rank = jnp.zeros((r, 1), jnp.int32)
        for j2 in range(n_past):
            if j2 != j:
                ahead = (gates[j2] > gates[j]) | ((gates[j2] == gates[j]) & (j2 < j))
                rank = rank + ahead.astype(jnp.int32)
        chosen = rank < MOBA_TOPK
        sl = slice(j * MOBA_BLOCK, (j + 1) * MOBA_BLOCK)
        s_scr[:, sl] = jnp.where(chosen, s_scr[:, sl], NEG)
    t_row = lax.broadcasted_iota(jnp.int32, (r, page), 0) % t_dec
    t_col = lax.broadcasted_iota(jnp.int32, (r, page), 1)
    s_own = _dot_nt(qs, _pad_rows(kn_ref[...], page).astype(BF16))
    s_scr[:, n_pages * page:] = jnp.where(t_col <= t_row, s_own, NEG)

    s = s_scr[...]
    m = jnp.max(s, axis=1, keepdims=True)
    p_all = jnp.exp(s - m)
    l = jnp.sum(p_all, axis=1, keepdims=True)
    acc = _dot(p_all[:, n_pages * page:].astype(BF16), _pad_rows(vn_ref[...], page).astype(BF16))
    for j in range(n_past):
        acc = acc + _dot_nt(p_all[:, j * MOBA_BLOCK:(j + 1) * MOBA_BLOCK].astype(BF16), block_of(v_pages, j))
    o_full = jnp.where(head_mask, acc / l, 0.0)
    out = o_full[0:t_dec]
    for h in range(1, n_heads):
        out = out + o_full[h * t_dec:(h + 1) * t_dec]
    o_ref[...] = out


def _moba_sample(q, k_new, v_new, cache_kt, cache_vt, layer, page_table):
    b, t_dec, d = q.shape
    n_pages = page_table.shape[1]
    page = cache_kt.shape[3]
    n_heads = d // HEAD_DIM
    assert MOBA_BLOCK % page == 0 and (n_pages * page) % MOBA_BLOCK == 0 and t_dec <= page
    tok_spec = pl.BlockSpec((None, t_dec, d), lambda i, pt: (i, 0, 0))
    page_specs = [pl.BlockSpec((None, None, d, page),
                               functools.partial(lambda i, pt, p: (layer, pt[i, p], 0, 0), p=p))
                  for p in range(n_pages)]
    grid_spec = pltpu.PrefetchScalarGridSpec(
        num_scalar_prefetch=1,
        grid=(b,),
        in_specs=[tok_spec, tok_spec, tok_spec] + page_specs + page_specs,
        out_specs=tok_spec,
        scratch_shapes=[pltpu.VMEM((n_heads * t_dec, (n_pages + 1) * page), F32)],
    )
    return pl.pallas_call(
        functools.partial(_moba_sample_kernel, n_pages=n_pages, page=page, n_heads=n_heads),
        grid_spec=grid_spec,
        out_shape=jax.ShapeDtypeStruct((b, t_dec, d), F32),
        compiler_params=_params("arbitrary"),
        name="moba_sample",
    )(page_table, q, k_new, v_new, *([cache_kt] * n_pages), *([cache_vt] * n_pages))


def _dil_prompt_kernel(q_ref, k_ref, v_ref, o_ref, lse_ref, *, dil, tq, sub, width, win):
    length = k_ref.shape[0] // dil
    base = pl.program_id(2) * (sub * tq)
    lane = lax.broadcasted_iota(jnp.int32, (tq, LANES), 1)
    row_col = (lax.broadcasted_iota(jnp.int32, (tq, width), 0)
               - lax.broadcasted_iota(jnp.int32, (tq, width), 1))
    heads = range(HEADS_PER_TILE)

    def rows(start, n):
        return pl.ds(start, n) if dil == 1 else pl.ds(start, n, stride=dil)

    def attend(items):
        qs, kws, vws, valids = [], [], [], []
        for r, u in items:
            q0 = base + u * tq
            ks = jnp.clip(q0 - win, 0, length - width)
            q = q_ref[rows(u * tq * dil + r, tq), :]
            qs.append([(jnp.where((lane // HEAD_DIM) == hh, q, 0.0) * (HEAD_DIM ** -0.5 * LOG2_E)).astype(BF16)
                       for hh in heads])
            kws.append(k_ref[rows(ks * dil + r, width), :].astype(BF16))
            vws.append(v_ref[rows(ks * dil + r, width), :].astype(BF16))
            delta = q0 - ks + row_col
            valids.append((delta >= 0) & (delta <= win))
        n = range(len(items))
        scores = [_dot_nt(jnp.concatenate(qs[it], axis=0), kws[it]) for it in n]
        ps, ls, ms = [], [], []
        for it in n:
            for hh in heads:
                s = jnp.where(valids[it], scores[it][hh * tq:(hh + 1) * tq], NEG)
                m = jnp.max(s, axis=1, keepdims=True)
                p = jnp.exp2(s - m)
                ls.append(jnp.sum(p, axis=1, keepdims=True))
                ms.append(m * (1.0 / LOG2_E))
                ps.append(p.astype(BF16))
        stacked = [_dot(jnp.concatenate(ps[it * len(heads):(it + 1) * len(heads)], axis=0), vws[it]) for it in n]
        pvs = [stacked[it][hh * tq:(hh + 1) * tq] for it in n for hh in heads]
        for it, (r, u) in enumerate(items):
            c0, c1 = it * len(heads), it * len(heads) + 1
            o_ref[rows(u * tq * dil + r, tq), :] = jnp.where(lane < HEAD_DIM, pvs[c0] / ls[c0], pvs[c1] / ls[c1])
            lse_ref[rows(u * tq * dil + r, tq), :] = jnp.where(
                lane < HEAD_DIM, jnp.broadcast_to(ms[c0] + jnp.log(ls[c0]), (tq, LANES)),
                jnp.broadcast_to(ms[c1] + jnp.log(ls[c1]), (tq, LANES)))

    if dil == 1:
        attend([(0, u) for u in range(sub)])
    else:
        per_iter = min(dil, DIL_ITEMS)

        def some_classes(c, carry):
            attend([(per_iter * c + x, u) for x in range(per_iter) for u in range(sub)])
            return carry
        lax.fori_loop(0, dil // per_iter, some_classes, 0)


def _dil_prompt(q, k, v, window, dil):
    b, t, dk = q.shape
    assert t % dil == 0 and HEADS_PER_TILE == 2 and (dil == 1 or dil % 2 == 0)
    length = t // dil
    win = window // dil
    tq = min(LANES, length)
    sub = DIL_ITEMS if dil == 1 and length % (DIL_ITEMS * tq) == 0 else 1
    assert dil == 1 or dil % min(dil, DIL_ITEMS) == 0
    width = min(tq + win, length)
    assert length % (sub * tq) == 0
    tile_spec = pl.BlockSpec((None, sub * tq * dil, LANES), lambda bi, h, i: (bi, i, h))
    seq_spec = pl.BlockSpec((None, t, LANES), lambda bi, h, i: (bi, 0, h))
    return pl.pallas_call(
        functools.partial(_dil_prompt_kernel, dil=dil, tq=tq, sub=sub, width=width, win=win),
        grid=(b, dk // LANES, length // (sub * tq)),
        in_specs=[tile_spec, seq_spec, seq_spec],
        out_specs=[tile_spec, tile_spec],
        out_shape=[jax.ShapeDtypeStruct((b, t, dk), F32)] * 2,
        compiler_params=_params("arbitrary", "arbitrary", "arbitrary"),
        name="dilated_prompt",
    )(q, k, v)


def _dil_sample_kernel(q0_ref, q1_ref, q2_ref, kn_ref, vn_ref, kc_ref, vc_ref, o_ref, s_scr, p_scr, *, n_heads):
    for i in range(o_ref.shape[0]):
        _dil_sample_one((q0_ref.at[i], q1_ref.at[i], q2_ref.at[i]), kn_ref.at[i], vn_ref.at[i],
                        kc_ref.at[i], vc_ref.at[i], o_ref.at[i], s_scr, p_scr, n_heads)


def _dil_sample_one(q_refs, kn_ref, vn_ref, kc_ref, vc_ref, o_ref, s_scr, p_scr, n_heads):
    t_dec, dk = kn_ref.shape
    past = kc_ref.shape[1]
    pad = LANES
    rg = n_heads * t_dec
    head_mask = _head_lane_mask((rg, dk), t_dec)
    qrows = jnp.concatenate(
        [jnp.where(head_mask, jnp.concatenate([qr[...]] * n_heads, axis=0), 0.0) for qr in q_refs], axis=0)
    qs = (qrows * HEAD_DIM ** -0.5).astype(BF16)
    s_scr[:, :past] = _dot(qs, kc_ref[...].astype(BF16))
    s_scr[:, past:] = _dot_nt(qs, _pad_rows(kn_ref[...], pad).astype(BF16))

    t_row = lax.broadcasted_iota(jnp.int32, (rg, past + pad), 0) % t_dec
    col = lax.broadcasted_iota(jnp.int32, (rg, past + pad), 1)
    delta = past + t_row - col
    in_range = col < past + t_dec
    l_g, lse_g = [], []
    for g, (window, dil) in enumerate(B_CONFIGS):
        rs = slice(g * rg, (g + 1) * rg)
        assert dil & (dil - 1) == 0
        valid = in_range & (delta >= 0) & (delta <= window) & ((delta & (dil - 1)) == 0)
        s = jnp.where(valid, s_scr[rs, :], NEG)
        m = jnp.max(s, axis=1, keepdims=True)
        p = jnp.exp(s - m)
        l = jnp.sum(p, axis=1, keepdims=True)
        p_scr[rs, :] = p.astype(BF16)
        l_g.append(l)
        lse_g.append(m + jnp.log(l))
    acc = (_dot_nt(p_scr[:, :past], vc_ref[...].astype(BF16))
           + _dot(p_scr[:, past:], _pad_rows(vn_ref[...], pad).astype(BF16)))
    mx = functools.reduce(jnp.maximum, lse_g)
    e = [jnp.exp(x - mx) for x in lse_g]
    den = functools.reduce(lambda a, b: a + b, e)
    mixed = jnp.zeros((rg, dk), F32)
    for g in range(len(B_CONFIGS)):
        mixed = mixed + (e[g] / den) * (acc[g * rg:(g + 1) * rg] / l_g[g])
    mixed = jnp.where(head_mask, mixed, 0.0)
    out = mixed[0:t_dec]
    for h in range(1, n_heads):
        out = out + mixed[h * t_dec:(h + 1) * t_dec]
    o_ref[...] = out


def _dil_sample(q_groups, k_new, v_new, cache_kt, cache_vt):
    b, t_dec, dk = k_new.shape
    past = cache_kt.shape[2]
    n_heads = dk // HEAD_DIM
    n_rows = len(B_CONFIGS) * n_heads * t_dec
    per_step = 2 if b % 2 == 0 else 1
    tok_spec = pl.BlockSpec((per_step, t_dec, dk), lambda i: (i, 0, 0))
    cache_spec = pl.BlockSpec((per_step, dk, past), lambda i: (i, 0, 0))
    return pl.pallas_call(
        functools.partial(_dil_sample_kernel, n_heads=n_heads),
        grid=(b // per_step,),
        in_specs=[tok_spec] * 5 + [cache_spec] * 2,
        out_specs=tok_spec,
        out_shape=jax.ShapeDtypeStruct((b, t_dec, dk), F32),
        scratch_shapes=[pltpu.VMEM((n_rows, past + LANES), F32),
                        pltpu.VMEM((n_rows, past + LANES), BF16)],
        compiler_params=_params("arbitrary"),
        name="dilated_sample",
    )(*q_groups, k_new, v_new, cache_kt, cache_vt)


def _trunk(x, mods, kv_mod, pos, past, weights):
    (norm_g, w_ffn_gate, w_ffn_up, w_ffn_down, w_qkv_a, w_o_a,
     kv_norm_g, w_kv_b, w_q_b, w_o_b, final_norm_g) = weights
    n_seq, t, d = x.shape
    m = n_seq * t
    depth = norm_g.shape[0]
    n_a = w_qkv_a.shape[0]
    dk = w_kv_b.shape[1] // 2
    is_prompt = past is None
    short = t < 256
    rows = _Rows(m, min(ROWS_LONG if is_prompt else ROWS_SHORT, m if short else t), t)
    rows_p = _Rows(m, min(ROWS_SHORT, m if short else t), t)
    tables = _rope_tables(pos)
    if rows.per_row:
        tables = (tuple(jnp.tile(tb, (n_seq, 1)) for tb in tables[0]),
                  tuple(jnp.tile(tb, (1, n_seq)) for tb in tables[1]))

    def heads_last(x_t):
        return jnp.transpose(x_t.reshape(n_seq, -1, HEAD_DIM, x_t.shape[-1]), (0, 3, 1, 2))

    h = x.reshape(m, d)
    a_k, a_v = [], []
    b_k = b_v = b_k_out = b_v_out = None
    for l in range(depth):
        if l == n_a:
            kvm = rows.prep(kv_mod)
            if is_prompt:
                b_k, b_kt, b_v, b_vt = _proj(rows_p, h, kvm, (0, 1), kv_norm_g, w_kv_b, tables,
                                             ((True, ("f32", "t_f32")), (False, ("f32", "t_f32"))))
                b_k_out, b_v_out = heads_last(b_kt), heads_last(b_vt)
            else:
                b_k, b_v = _proj(rows_p, h, kvm, (0, 1), kv_norm_g, w_kv_b, tables,
                                 ((True, ("f32",)), (False, ("f32",))))
                b_k_out, b_v_out = (r.reshape(n_seq, t, -1, HEAD_DIM) for r in (b_k, b_v))
        md = rows.prep(mods[l])
        h = _ffn(rows, h, md, (0, 1, 2), norm_g[l, 0], w_ffn_gate, w_ffn_up, w_ffn_down, l, 0)
        if l < n_a:
            shp = (n_seq, t, d)
            if is_prompt:
                q_t, k_t, k_bf, km, v_t, vt_bf = _proj(
                    rows_p, h, md, (3, 4), norm_g[l, 1], w_qkv_a[l], tables,
                    ((True, ("t_f32",)), (True, ("t_f32", "bf16", "block_mean")), (False, ("t_f32", "t_bf16"))))
                o = _moba_prompt(q_t, k_bf.reshape(shp), vt_bf, km)
                a_k.append(heads_last(k_t))
                a_v.append(heads_last(v_t))
            else:
                q, k, v = _proj(rows_p, h, md, (3, 4), norm_g[l, 1], w_qkv_a[l], tables,
                                ((True, ("f32",)), (True, ("f32",)), (False, ("f32",))))
                cache_kt, cache_vt = (
                    jnp.transpose(c, (0, 1, 3, 4, 2)).reshape(c.shape[0], c.shape[1], d, c.shape[2])
                    for c in past[:2])
                o = _moba_sample(q.reshape(shp), k.reshape(shp), v.reshape(shp), cache_kt, cache_vt, l, past[2])
                a_k.append(k.reshape(n_seq, t, -1, HEAD_DIM))
                a_v.append(v.reshape(n_seq, t, -1, HEAD_DIM))
            h = _oproj(rows, h, md, 5,w_o_a[l], [o.reshape(m, d)])
        else:
            lb = l - n_a
            qg = _proj(rows_p, h, md, (3, 4), norm_g[l, 1], w_q_b[lb], tables,
                       ((True, ("f32",)),) * len(B_CONFIGS))
            shp = (n_seq, t, dk)
            if is_prompt:
                res = [_dil_prompt(qi.reshape(shp), b_k.reshape(shp), b_v.reshape(shp), w, dl)
                       for qi, (w, dl) in zip(qg, B_CONFIGS)]
                h = _oproj(rows, h, md, 5,w_o_b[lb], [r[0].reshape(m, dk) for r in res],
                           [r[1].reshape(m, dk) for r in res])
            else:
                cb_kt, cb_vt = (jnp.transpose(c, (0, 2, 3, 1)).reshape(n_seq, dk, c.shape[1]) for c in past[3:5])
                o = _dil_sample([qi.reshape(shp) for qi in qg], b_k.reshape(shp), b_v.reshape(shp), cb_kt, cb_vt)
                h = _oproj(rows, h, md, 5,w_o_b[lb], [o.reshape(m, dk)])
        h = _ffn(rows, h, md, (6, 7, 8), norm_g[l, 2], w_ffn_gate, w_ffn_up, w_ffn_down, l, 1,
                 final_g=final_norm_g if l == depth - 1 else None)
    return h.reshape(n_seq, t, d), jnp.stack(a_k), jnp.stack(a_v), b_k_out, b_v_out


def kernel(x_prompt, x_sample, cache_a_k, cache_a_v, cache_b_k, cache_b_v, page_table, c_prompt, c_sample, norm_g, w_mod, b_mod, w_ffn_gate, w_ffn_up, w_ffn_down, w_qkv_a, w_o_a, kv_norm_g, w_kv_mod, b_kv_mod, w_kv_b, w_q_b, w_o_b, final_norm_g):
    bp, t, d = x_prompt.shape
    bs, t_dec, _ = x_sample.shape
    past_len = page_table.shape[1] * cache_a_k.shape[2]
    assert past_len % MOBA_BLOCK == 0 and t_dec <= MOBA_BLOCK

    n_c = bp + bs
    c_all = _pad_rows(jnp.concatenate([c_prompt, c_sample], axis=0), -(-n_c // 8) * 8)
    mods = _modulation(c_all, w_mod, b_mod)
    kv_mod = _modulation(c_all, w_kv_mod[None], b_kv_mod[None])[0]

    bf = lambda w: w.astype(BF16)
    weights = (norm_g, bf(w_ffn_gate), bf(w_ffn_up), bf(w_ffn_down), bf(w_qkv_a), bf(w_o_a),
               kv_norm_g, bf(w_kv_b), bf(w_q_b), bf(w_o_b), final_norm_g)
    y_p, ak_p, av_p, bk_p, bv_p = _trunk(
        x_prompt, mods[:, :bp], kv_mod[:bp], jnp.arange(t, dtype=jnp.int32), None, weights)
    y_s, ak_s, av_s, bk_s, bv_s = _trunk(
        x_sample, mods[:, bp:n_c], kv_mod[bp:n_c], past_len + jnp.arange(t_dec, dtype=jnp.int32),
        (cache_a_k, cache_a_v, page_table, cache_b_k, cache_b_v), weights)

    keep = min(max(w for w, _ in B_CONFIGS), t)
    return (y_p, y_s, ak_p, av_p, ak_s, av_s, bk_p[:, -keep:], bv_p[:, -keep:], bk_s, bv_s)
```

```python
import functools

import jax
import jax.numpy as jnp
from jax import lax
from jax.experimental import pallas as pl
from jax.experimental.pallas import tpu as pltpu

HEAD_DIM = 64
ROT_DIM = HEAD_DIM // 4
ROPE_THETA = 500000.0
MOBA_BLOCK = 256
MOBA_TOPK = 3
B_CONFIGS = ((128, 1), (512, 4), (2048, 16))
EPS = 1e-6
NEG = -1e30
LOG2_E = 1.4426950408889634

LANES = 128
HEADS_PER_TILE = LANES // HEAD_DIM
MXU_COLS = 256
ROWS_LONG, ROWS_SHORT = 1024, 512
DIL_ITEMS = 4
DIL_GROUPS = 2
VMEM_LIMIT = 56 * 1024 * 1024

F32 = jnp.float32
BF16 = jnp.bfloat16


def _params(*sem):
    return pltpu.CompilerParams(dimension_semantics=sem, vmem_limit_bytes=VMEM_LIMIT)


def _dot(a, b):
    return jnp.dot(a, b, preferred_element_type=F32)


def _dot_nt(a, b):
    return lax.dot_general(a, b, (((1,), (1,)), ((), ())), preferred_element_type=F32)


def _silu(x):
    return x * jax.nn.sigmoid(x)


def _rmsnorm(x, g):
    return x * lax.rsqrt(jnp.mean(x * x, axis=-1, keepdims=True) + EPS) * g


def _rms_mod(x, g, shift, scale):
    return _rmsnorm(x, g) * (1.0 + scale) + shift


def _head_lane_mask(shape, rows_per_head):
    row = lax.broadcasted_iota(jnp.int32, shape, 0)
    lane = lax.broadcasted_iota(jnp.int32, shape, 1)
    return (lane // HEAD_DIM) == (row // rows_per_head)


class _Rows:
    def __init__(self, m, tm, rows_per_seq):
        self.m, self.tm = m, tm
        self.per_row = rows_per_seq < tm
        self.rows_per_seq = rows_per_seq
        self.tiles_per_seq = max(rows_per_seq // tm, 1)

    def prep(self, vecs):
        if self.per_row:
            return jnp.repeat(vecs, self.rows_per_seq, axis=0)
        return vecs[:, None, :]

    def mod_spec(self, d, col):
        if self.per_row:
            return pl.BlockSpec((self.tm, d), lambda i, j: (i, col))
        tps = self.tiles_per_seq
        return pl.BlockSpec((None, 1, d), lambda i, j: (i // tps, 0, col))

    def pos_spec(self, table_rows):
        n_pos_tiles = table_rows // self.tm
        return pl.BlockSpec((self.tm, LANES), lambda i, j: (i % n_pos_tiles, 0))


def _mod_kernel(c_ref, w_ref, b_ref, o_ref):
    cs = _silu(c_ref[...]).astype(BF16)
    o_ref[...] = _dot(cs, w_ref[...].astype(BF16)) + b_ref[...]


def _modulation(c, w, b):
    mc, d = c.shape
    nl, _, n = w.shape
    tn = d
    return pl.pallas_call(
        _mod_kernel,
        grid=(nl, n // tn),
        in_specs=[pl.BlockSpec((mc, d), lambda l, j: (0, 0)),
                  pl.BlockSpec((None, d, tn), lambda l, j: (l, 0, j)),
                  pl.BlockSpec((None, 1, tn), lambda l, j: (l, 0, j))],
        out_specs=pl.BlockSpec((None, mc, tn), lambda l, j: (l, 0, j)),
        out_shape=jax.ShapeDtypeStruct((nl, mc, n), F32),
        compiler_params=_params("arbitrary", "arbitrary"),
        name="modulation",
    )(c, w, b.reshape(nl, 1, n))


def _ffn_kernel(h_ref, sh_ref, sc_ref, gt_ref, g_ref, wg_ref, wu_ref, wd_ref, *rest, final):
    if final:
        fg_ref, o_ref, u_scr, act_scr = rest
    else:
        o_ref, u_scr, act_scr = rest
    d, ff = wg_ref.shape
    tf = MXU_COLS if ff % MXU_COLS == 0 else LANES
    to = min(2 * MXU_COLS, d)
    u_scr[...] = _rms_mod(h_ref[...], g_ref[...], sh_ref[...], sc_ref[...]).astype(BF16)
    for c in range(ff // tf):
        sl = slice(c * tf, (c + 1) * tf)
        u = u_scr[...]
        a = _dot(u, wg_ref[:, sl])
        b = _dot(u, wu_ref[:, sl])
        act_scr[:, sl] = (_silu(a) * b).astype(BF16)
    for c in range(d // to):
        sl = slice(c * to, (c + 1) * to)
        down = _dot(act_scr[...], wd_ref[:, sl])
        o_ref[:, sl] = h_ref[:, sl] + 0.5 * gt_ref[:, sl] * down
    if final:
        o_ref[...] = _rmsnorm(o_ref[...], fg_ref[...])


def _ffn(rows, h, mod, cols, g, wg, wu, wd, l, s, final_g=None):
    m, d = h.shape
    ff = wg.shape[-1]
    tm = rows.tm
    final = final_g is not None
    resident = lambda shape: pl.BlockSpec((None, None) + shape, lambda i, f: (l, s, 0, 0),
                                          pipeline_mode=pl.Buffered(1))
    in_specs = [pl.BlockSpec((tm, d), lambda i, f: (i, 0)),
                rows.mod_spec(d, cols[0]), rows.mod_spec(d, cols[1]), rows.mod_spec(d, cols[2]),
                pl.BlockSpec((1, d), lambda i, f: (0, 0)),
                resident((d, ff)), resident((d, ff)), resident((ff, d))]
    args = [h, mod, mod, mod, g.reshape(1, d), wg, wu, wd]
    if final:
        in_specs.append(pl.BlockSpec((1, d), lambda i, f: (0, 0)))
        args.append(final_g.reshape(1, d))
    return pl.pallas_call(
        functools.partial(_ffn_kernel, final=final),
        grid=(m // tm, 1),
        in_specs=in_specs,
        out_specs=pl.BlockSpec((tm, d), lambda i, f: (i, 0)),
        out_shape=jax.ShapeDtypeStruct((m, d), F32),
        scratch_shapes=[pltpu.VMEM((tm, d), BF16), pltpu.VMEM((tm, ff), BF16)],
        compiler_params=_params("arbitrary", "arbitrary"),
        name="ffn",
    )(*args)


def _rope_tables(pos):
    half = ROT_DIM // 2
    n = pos.shape[0]
    freq = ROPE_THETA ** (-jnp.arange(half, dtype=F32) / half)
    ang = pos.astype(F32)[:, None] * freq[None, :]
    cos, sin = jnp.cos(ang), jnp.sin(ang)
    rest = HEAD_DIM - ROT_DIM
    ct = jnp.concatenate([cos, cos, jnp.ones((n, rest), F32)], axis=1)
    s_lo = jnp.concatenate([-sin, jnp.zeros((n, half + rest), F32)], axis=1)
    s_hi = jnp.concatenate([jnp.zeros((n, half), F32), sin, jnp.zeros((n, rest), F32)], axis=1)
    tile = lambda t: jnp.tile(t, (1, HEADS_PER_TILE))
    return (tile(ct), tile(s_hi), tile(s_lo)), (cos.T, sin.T)


def _rope_tile(x, ct, s_hi, s_lo):
    half = ROT_DIM // 2
    return x * ct + pltpu.roll(x, half, 1) * s_hi + pltpu.roll(x, LANES - half, 1) * s_lo


def _rope_t(x_t, cos_t, sin_t):
    half = ROT_DIM // 2
    pieces = []
    for base in range(0, x_t.shape[0], HEAD_DIM):
        x1, x2 = x_t[base:base + half], x_t[base + half:base + ROT_DIM]
        pieces += [x1 * cos_t - x2 * sin_t, x2 * cos_t + x1 * sin_t, x_t[base + ROT_DIM:base + HEAD_DIM]]
    return jnp.concatenate(pieces, axis=0)


_MEAN_GROUP = 8


def _proj_kernel(h_ref, sh_ref, sc_ref, g_ref, w_ref, ct_ref, shi_ref, slo_ref, cos_t_ref, sin_t_ref,
                 *outs, parts):
    i = pl.program_id(0)
    u = _rms_mod(h_ref[...], g_ref[...], sh_ref[...], sc_ref[...]).astype(BF16)
    tm = u.shape[0]
    n_p = w_ref.shape[1] // len(parts)
    blocks = tm // MOBA_BLOCK
    n_out = 0
    for p, (rope, forms) in enumerate(parts):
        part_outs = outs[n_out:n_out + len(forms)]
        n_out += len(forms)
        y = _dot(u, w_ref[:, p * n_p:(p + 1) * n_p])
        val = val_t = None
        if any(f.startswith("t_") for f in forms):
            val_t = y.T
            if rope:
                val_t = _rope_t(val_t, cos_t_ref[...], sin_t_ref[...])
            if any(not f.startswith("t_") for f in forms):
                val = val_t.T
        elif rope:
            ct, s_hi, s_lo = ct_ref[...], shi_ref[...], slo_ref[...]
            val = jnp.concatenate([_rope_tile(y[:, c * LANES:(c + 1) * LANES], ct, s_hi, s_lo)
                                   for c in range(n_p // LANES)], axis=1)
        else:
            val = y
        for form, out in zip(forms, part_outs):
            if form == "f32":
                out[...] = val
            elif form == "bf16":
                out[...] = val.astype(BF16)
            elif form == "t_f32":
                out[...] = val_t
            elif form == "t_bf16":
                for c in range(blocks):
                    out[c] = val_t[:, c * MOBA_BLOCK:(c + 1) * MOBA_BLOCK].astype(BF16)
            elif form == "block_mean":
                base = (i % (_MEAN_GROUP // blocks)) * blocks
                for c in range(blocks):
                    out[pl.ds(base + c, 1), :] = jnp.mean(
                        val[c * MOBA_BLOCK:(c + 1) * MOBA_BLOCK], axis=0, keepdims=True)


def _proj(rows, h, mod, cols, g, w, tables, parts):
    m, d = h.shape
    n_p = w.shape[1] // len(parts)
    tm, tps, t = rows.tm, rows.tiles_per_seq, rows.rows_per_seq
    n_seq = m // t
    out_specs, out_shape = [], []
    for _, forms in parts:
        for form in forms:
            if form in ("f32", "bf16"):
                out_specs.append(pl.BlockSpec((tm, n_p), lambda i, j: (i, 0)))
                out_shape.append(jax.ShapeDtypeStruct((m, n_p), F32 if form == "f32" else BF16))
                continue
            assert not rows.per_row and tm % MOBA_BLOCK == 0
            if form == "t_f32":
                out_specs.append(pl.BlockSpec((None, n_p, tm), lambda i, j: (i // tps, 0, i % tps)))
                out_shape.append(jax.ShapeDtypeStruct((n_seq, n_p, t), F32))
            elif form == "t_bf16":
                out_specs.append(pl.BlockSpec((None, tm // MOBA_BLOCK, n_p, MOBA_BLOCK),
                                              lambda i, j: (i // tps, i % tps, 0, 0)))
                out_shape.append(jax.ShapeDtypeStruct((n_seq, t // MOBA_BLOCK, n_p, MOBA_BLOCK), BF16))
            else:
                steps = _MEAN_GROUP * MOBA_BLOCK // tm
                assert form == "block_mean" and tps % steps == 0
                out_specs.append(pl.BlockSpec((None, _MEAN_GROUP, n_p),
                                              lambda i, j: (i // tps, (i % tps) // steps, 0)))
                out_shape.append(jax.ShapeDtypeStruct((n_seq, t // MOBA_BLOCK, n_p), F32))
    lane_tables, t_tables = tables
    pos_spec = rows.pos_spec(lane_tables[0].shape[0])
    n_pos_tiles = lane_tables[0].shape[0] // tm
    pos_t_spec = pl.BlockSpec((ROT_DIM // 2, tm), lambda i, j: (0, i % n_pos_tiles))
    return pl.pallas_call(
        functools.partial(_proj_kernel, parts=parts),
        grid=(m // tm, 1),
        in_specs=[pl.BlockSpec((tm, d), lambda i, j: (i, 0)),
                  rows.mod_spec(d, cols[0]), rows.mod_spec(d, cols[1]),
                  pl.BlockSpec((1, d), lambda i, j: (0, 0)),
                  pl.BlockSpec(w.shape, lambda i, j: (0, 0)),
                  pos_spec, pos_spec, pos_spec, pos_t_spec, pos_t_spec],
        out_specs=out_specs,
        out_shape=out_shape,
        compiler_params=_params("arbitrary", "arbitrary"),
        name="proj",
    )(h, mod, mod, g.reshape(1, d), w, *lane_tables, *t_tables)


def _oproj_kernel(h_ref, gt_ref, w_ref, *rest, n_groups):
    o_ref = rest[-1]
    if n_groups == 0:
        o = rest[0][...]
    else:
        os_, lses = rest[:n_groups], rest[n_groups:2 * n_groups]
        lse = [r[...] for r in lses]
        mx = functools.reduce(jnp.maximum, lse)
        e = [jnp.exp(x - mx) for x in lse]
        den = functools.reduce(lambda a, b: a + b, e)
        o = functools.reduce(lambda a, b: a + b, [(ei / den) * r[...] for ei, r in zip(e, os_)])
    o_ref[...] = h_ref[...] + gt_ref[...] * _dot(o.astype(BF16), w_ref[...])


def _oproj(rows, h, mod, col, w, o_list, lse_list=()):
    m, d = h.shape
    k = w.shape[0]
    tm = rows.tm
    xs = list(o_list) + list(lse_list)
    return pl.pallas_call(
        functools.partial(_oproj_kernel, n_groups=len(lse_list)),
        grid=(m // tm, 1),
        in_specs=[pl.BlockSpec((tm, d), lambda i, j: (i, 0)),
                  rows.mod_spec(d, col),
                  pl.BlockSpec((k, d), lambda i, j: (0, 0))]
                 + [pl.BlockSpec((tm, k), lambda i, j: (i, 0))] * len(xs),
        out_specs=pl.BlockSpec((tm, d), lambda i, j: (i, 0)),
        out_shape=jax.ShapeDtypeStruct((m, d), F32),
        compiler_params=_params("arbitrary", "arbitrary"),
        name="oproj",
    )(h, mod, w, *xs)


def _split_bf16(x):
    hi = x.astype(BF16)
    lo = (x - hi.astype(F32)).astype(BF16)
    return hi, lo


def _top_blocks_t(gate, n_past):
    blk_id = lax.broadcasted_iota(jnp.int32, gate.shape, 0).astype(F32)
    n_past = n_past.astype(F32)
    g = jnp.where(blk_id < n_past, gate, NEG)
    sel = jnp.zeros(gate.shape, jnp.bool_)
    for _ in range(MOBA_TOPK):
        mx = jnp.max(g, axis=0, keepdims=True)
        idx = jnp.min(jnp.where(g == mx, blk_id, float(gate.shape[0])), axis=0, keepdims=True)
        pick = blk_id == idx
        sel = sel | pick
        g = jnp.where(pick, -jnp.inf, g)
    return sel & (blk_id < n_past)


def _moba_prompt_kernel(q_ref, k_ref, vt_ref, km_ref, o_ref,
                        qt_scr, sel_scr, m_scr, l_scr, acc_scr, s_own, s_even, s_odd):
    blk = MOBA_BLOCK
    n_h = qt_scr.shape[0]
    tiles = [slice(hh // HEADS_PER_TILE * LANES, (hh // HEADS_PER_TILE + 1) * LANES) for hh in range(n_h)]
    nb = vt_ref.shape[0]
    i = pl.program_id(2)
    heads = range(n_h)
    hs = [slice(hh * HEAD_DIM, (hh + 1) * HEAD_DIM) for hh in heads]

    def scores_into(dst, first, n_blocks):
        rows = n_blocks * blk
        jj = jnp.minimum(first, nb - n_blocks)
        kb = k_ref[pl.ds(pl.multiple_of(jj * blk, blk), rows), :]
        for hh in heads:
            dst[hh] = _dot(kb[:, tiles[hh]], qt_scr[hh])

    def consume(src, part, j, own):
        jj = jnp.minimum(j, nb - 1)
        vtb = vt_ref[jj]
        ps, alphas, chosen = [], [], []
        for hh in heads:
            s = src[hh, part * blk:(part + 1) * blk, :]
            if own:
                key_id = lax.broadcasted_iota(jnp.int32, (blk, blk), 0)
                qry_id = lax.broadcasted_iota(jnp.int32, (blk, blk), 1)
                s = jnp.where(key_id <= qry_id, s, NEG)
                ch = jnp.full((1, blk), True)
            else:
                ch = (sel_scr[hh, pl.ds(jj, 1), :] > 0.5) & (j < i)
            m_old = m_scr[hh]
            m_all = jnp.maximum(m_old, jnp.max(s, axis=0, keepdims=True))
            p = jnp.exp2(s - m_all)
            m_new = jnp.where(ch, m_all, m_old)
            alpha = jnp.exp2(m_old - m_new)
            l_scr[hh] = alpha * l_scr[hh] + jnp.where(ch, jnp.sum(p, axis=0, keepdims=True), 0.0)
            m_scr[hh] = m_new
            ps.append(p.astype(BF16))
            alphas.append(alpha)
            chosen.append(ch)
        pvs = [_dot(vtb[hs[hh], :], ps[hh]) for hh in heads]
        for hh in heads:
            acc_scr[hh] = alphas[hh] * acc_scr[hh] + jnp.where(chosen[hh], pvs[hh], 0.0)

    feat = lax.broadcasted_iota(jnp.int32, (LANES, blk), 0)
    for hh in heads:
        qth = jnp.where((feat // HEAD_DIM) == hh % HEADS_PER_TILE, q_ref[tiles[hh], :], 0.0)
        qt_scr[hh] = (qth * (HEAD_DIM ** -0.5 * LOG2_E)).astype(BF16)
        q_hi, q_lo = _split_bf16(qth)
        km_hi, km_lo = _split_bf16(km_ref[:, tiles[hh]])
        by_q_hi = _dot(jnp.concatenate([km_hi, km_lo], axis=0), q_hi)
        gate = by_q_hi[:nb] + by_q_hi[nb:] + _dot(km_hi, q_lo)
        sel_scr[hh] = _top_blocks_t(gate, i).astype(F32)
        m_scr[hh] = jnp.full((1, blk), NEG, F32)
        l_scr[hh] = jnp.zeros((1, blk), F32)
        acc_scr[hh] = jnp.zeros((HEAD_DIM, blk), F32)
    scores_into(s_own, i, 1)
    scores_into(s_even, 0, 2)
    consume(s_own, 0, i, own=True)

    def pair_from(cur, nxt):
        def run(c):
            scores_into(nxt, 2 * c + 2, 2)
            consume(cur, 0, 2 * c, own=False)
            consume(cur, 1, 2 * c + 1, own=False)
        return run

    def two_blocks(c, carry):
        lax.cond(c % 2 == 0, pair_from(s_even, s_odd), pair_from(s_odd, s_even), c)
        return carry

    lax.fori_loop(0, (i + 1) // 2, two_blocks, 0)
    ot = jnp.concatenate([acc_scr[hh] / l_scr[hh] for hh in heads], axis=0)
    o_ref[...] = ot.T


def _moba_prompt(q_t, k_bf, vt_bf, km):
    b, d, t = q_t.shape
    blk = MOBA_BLOCK
    nb = t // blk
    width = min(d, 2 * LANES)
    n_h = width // HEAD_DIM
    assert t % blk == 0 and nb % 8 == 0 and d % width == 0
    tile_spec = pl.BlockSpec((None, blk, width), lambda bi, hg, i: (bi, i, hg))
    return pl.pallas_call(
        _moba_prompt_kernel,
        grid=(b, d // width, nb),
        in_specs=[pl.BlockSpec((None, width, blk), lambda bi, hg, i: (bi, hg, i)),
                  pl.BlockSpec((None, t, width), lambda bi, hg, i: (bi, 0, hg)),
                  pl.BlockSpec((None, nb, width, blk), lambda bi, hg, i: (bi, 0, hg, 0)),
                  pl.BlockSpec((None, nb, width), lambda bi, hg, i: (bi, 0, hg))],
        out_specs=tile_spec,
        out_shape=jax.ShapeDtypeStruct((b, t, d), F32),
        scratch_shapes=[pltpu.VMEM((n_h, LANES, blk), BF16),
                        pltpu.VMEM((n_h, nb, blk), F32),
                        pltpu.VMEM((n_h, 1, blk), F32),
                        pltpu.VMEM((n_h, 1, blk), F32),
                        pltpu.VMEM((n_h, HEAD_DIM, blk), F32)]
                       + [pltpu.VMEM((n_h, blk, blk), F32)] + [pltpu.VMEM((n_h, 2 * blk, blk), F32)] * 2,
        compiler_params=_params("arbitrary", "arbitrary", "arbitrary"),
        name="moba_prompt",
    )(q_t, k_bf, vt_bf, km)


def _pad_rows(x, n):
    return jnp.concatenate([x, jnp.zeros((n - x.shape[0], x.shape[1]), x.dtype)], axis=0)


def _moba_sample_kernel(pt_ref, q_ref, kn_ref, vn_ref, *rest, n_pages, page, n_heads):
    k_pages, v_pages = rest[:n_pages], rest[n_pages:2 * n_pages]
    o_ref, s_scr = rest[2 * n_pages], rest[2 * n_pages + 1]
    t_dec, d = q_ref.shape
    r = n_heads * t_dec
    ppb = MOBA_BLOCK // page
    n_past = n_pages // ppb
    own = lax.broadcasted_iota(jnp.int32, (r, d), 0) // t_dec
    head_mask = (lax.broadcasted_iota(jnp.int32, (r, d), 1) // HEAD_DIM) == own
    qrows = jnp.where(head_mask, jnp.concatenate([q_ref[...]] * n_heads, axis=0), 0.0)
    qs = (qrows * HEAD_DIM ** -0.5).astype(BF16)

    def block_of(pages, j):
        return jnp.concatenate([pages[p][...].astype(BF16) for p in range(j * ppb, (j + 1) * ppb)], axis=1)

    gates = []
    for j in range(n_past):
        sl = slice(j * MOBA_BLOCK, (j + 1) * MOBA_BLOCK)
        s_scr[:, sl] = _dot(qs, block_of(k_pages, j))
        gates.append(jnp.sum(s_scr[:, sl], axis=1, keepdims=True))
    for j in range(n_past):
        rank = jnp.zeros((r, 1), jnp.int32)
        for j2 in range(n_past):
            if j2 != j:
                ahead = (gates[j2] > gates[j]) | ((gates[j2] == gates[j]) & (j2 < j))
                rank = rank + ahead.astype(jnp.int32)
        chosen = rank < MOBA_TOPK
        sl = slice(j * MOBA_BLOCK, (j + 1) * MOBA_BLOCK)
        s_scr[:, sl] = jnp.where(chosen, s_scr[:, sl], NEG)
    t_row = lax.broadcasted_iota(jnp.int32, (r, page), 0) % t_dec
    t_col = lax.broadcasted_iota(jnp.int32, (r, page), 1)
    s_own = _dot_nt(qs, _pad_rows(kn_ref[...], page).astype(BF16))
    s_scr[:, n_pages * page:] = jnp.where(t_col <= t_row, s_own, NEG)

    s = s_scr[...]
    m = jnp.max(s, axis=1, keepdims=True)
    p_all = jnp.exp(s - m)
    l = jnp.sum(p_all, axis=1, keepdims=True)
    acc = _dot(p_all[:, n_pages * page:].astype(BF16), _pad_rows(vn_ref[...], page).astype(BF16))
    for j in range(n_past):
        acc = acc + _dot_nt(p_all[:, j * MOBA_BLOCK:(j + 1) * MOBA_BLOCK].astype(BF16), block_of(v_pages, j))
    o_full = jnp.where(head_mask, acc / l, 0.0)
    out = o_full[0:t_dec]
    for h in range(1, n_heads):
        out = out + o_full[h * t_dec:(h + 1) * t_dec]
    o_ref[...] = out


def _moba_sample(q, k_new, v_new, cache_kt, cache_vt, layer, page_table):
    b, t_dec, d = q.shape
    n_pages = page_table.shape[1]
    page = cache_kt.shape[3]
    n_heads = d // HEAD_DIM
    assert MOBA_BLOCK % page == 0 and (n_pages * page) % MOBA_BLOCK == 0 and t_dec <= page
    tok_spec = pl.BlockSpec((None, t_dec, d), lambda i, pt: (i, 0, 0))
    page_specs = [pl.BlockSpec((None, None, d, page),
                               functools.partial(lambda i, pt, p: (layer, pt[i, p], 0, 0), p=p))
                  for p in range(n_pages)]
    grid_spec = pltpu.PrefetchScalarGridSpec(
        num_scalar_prefetch=1,
        grid=(b,),
        in_specs=[tok_spec, tok_spec, tok_spec] + page_specs + page_specs,
        out_specs=tok_spec,
        scratch_shapes=[pltpu.VMEM((n_heads * t_dec, (n_pages + 1) * page), F32)],
    )
    return pl.pallas_call(
        functools.partial(_moba_sample_kernel, n_pages=n_pages, page=page, n_heads=n_heads),
        grid_spec=grid_spec,
        out_shape=jax.ShapeDtypeStruct((b, t_dec, d), F32),
        compiler_params=_params("arbitrary"),
        name="moba_sample",
    )(page_table, q, k_new, v_new, *([cache_kt] * n_pages), *([cache_vt] * n_pages))


def _dil_prompt_kernel(q_ref, k_ref, v_ref, o_ref, lse_ref, *, dil, tq, sub, width, win):
    length = k_ref.shape[0] // dil
    base = pl.program_id(2) * (sub * tq)
    lane = lax.broadcasted_iota(jnp.int32, (tq, LANES), 1)
    row_col = (lax.broadcasted_iota(jnp.int32, (tq, width), 0)
               - lax.broadcasted_iota(jnp.int32, (tq, width), 1))
    heads = range(HEADS_PER_TILE)

    def rows(start, n):
        return pl.ds(start, n) if dil == 1 else pl.ds(start, n, stride=dil)

    def attend(items):
        qs, kws, vws, valids = [], [], [], []
        for r, u in items:
            q0 = base + u * tq
            ks = jnp.clip(q0 - win, 0, length - width)
            q = q_ref[rows(u * tq * dil + r, tq), :]
            qs.append([(jnp.where((lane // HEAD_DIM) == hh, q, 0.0) * (HEAD_DIM ** -0.5 * LOG2_E)).astype(BF16)
                       for hh in heads])
            kws.append(k_ref[rows(ks * dil + r, width), :].astype(BF16))
            vws.append(v_ref[rows(ks * dil + r, width), :].astype(BF16))
            delta = q0 - ks + row_col
            valids.append((delta >= 0) & (delta <= win))
        n = range(len(items))
        scores = [_dot_nt(jnp.concatenate(qs[it], axis=0), kws[it]) for it in n]
        ps, ls, ms = [], [], []
        for it in n:
            for hh in heads:
                s = jnp.where(valids[it], scores[it][hh * tq:(hh + 1) * tq], NEG)
                m = jnp.max(s, axis=1, keepdims=True)
                p = jnp.exp2(s - m)
                ls.append(jnp.sum(p, axis=1, keepdims=True))
                ms.append(m * (1.0 / LOG2_E))
                ps.append(p.astype(BF16))
        stacked = [_dot(jnp.concatenate(ps[it * len(heads):(it + 1) * len(heads)], axis=0), vws[it]) for it in n]
        pvs = [stacked[it][hh * tq:(hh + 1) * tq] for it in n for hh in heads]
        for it, (r, u) in enumerate(items):
            c0, c1 = it * len(heads), it * len(heads) + 1
            o_ref[rows(u * tq * dil + r, tq), :] = jnp.where(lane < HEAD_DIM, pvs[c0] / ls[c0], pvs[c1] / ls[c1])
            lse_ref[rows(u * tq * dil + r, tq), :] = jnp.where(
                lane < HEAD_DIM, jnp.broadcast_to(ms[c0] + jnp.log(ls[c0]), (tq, LANES)),
                jnp.broadcast_to(ms[c1] + jnp.log(ls[c1]), (tq, LANES)))

    if dil * sub <= DIL_ITEMS * DIL_GROUPS:
        items = [(r, u) for u in range(sub) for r in range(dil)]
        for c in range(0, len(items), DIL_ITEMS):
            attend(items[c:c + DIL_ITEMS])
    else:
        per_iter = min(dil, DIL_ITEMS)

        def some_classes(c, carry):
            attend([(per_iter * c + x, u) for x in range(per_iter) for u in range(sub)])
            return carry
        lax.fori_loop(0, dil // per_iter, some_classes, 0)


def _dil_prompt(q, k, v, window, dil):
    b, t, dk = q.shape
    assert t % dil == 0 and HEADS_PER_TILE == 2 and (dil == 1 or dil % 2 == 0)
    length = t // dil
    win = window // dil
    tq = min(LANES, length)
    sub = max(DIL_ITEMS * DIL_GROUPS // dil, 1)
    if length % (sub * tq):
        sub = 1
    assert dil * sub <= DIL_ITEMS * DIL_GROUPS or (sub == 1 and dil % DIL_ITEMS == 0)
    width = min(tq + win, length)
    assert length % (sub * tq) == 0
    tile_spec = pl.BlockSpec((None, sub * tq * dil, LANES), lambda bi, h, i: (bi, i, h))
    seq_spec = pl.BlockSpec((None, t, LANES), lambda bi, h, i: (bi, 0, h))
    return pl.pallas_call(
        functools.partial(_dil_prompt_kernel, dil=dil, tq=tq, sub=sub, width=width, win=win),
        grid=(b, dk // LANES, length // (sub * tq)),
        in_specs=[tile_spec, seq_spec, seq_spec],
        out_specs=[tile_spec, tile_spec],
        out_shape=[jax.ShapeDtypeStruct((b, t, dk), F32)] * 2,
        compiler_params=_params("arbitrary", "arbitrary", "arbitrary"),
        name="dilated_prompt",
    )(q, k, v)


def _dil_sample_kernel(q0_ref, q1_ref, q2_ref, kn_ref, vn_ref, kc_ref, vc_ref, o_ref, s_scr, p_scr, *, n_heads):
    for i in range(o_ref.shape[0]):
        _dil_sample_one((q0_ref.at[i], q1_ref.at[i], q2_ref.at[i]), kn_ref.at[i], vn_ref.at[i],
                        kc_ref.at[i], vc_ref.at[i], o_ref.at[i], s_scr, p_scr, n_heads)


def _dil_sample_one(q_refs, kn_ref, vn_ref, kc_ref, vc_ref, o_ref, s_scr, p_scr, n_heads):
    t_dec, dk = kn_ref.shape
    past = kc_ref.shape[1]
    pad = LANES
    rg = n_heads * t_dec
    head_mask = _head_lane_mask((rg, dk), t_dec)
    qrows = jnp.concatenate(
        [jnp.where(head_mask, jnp.concatenate([qr[...]] * n_heads, axis=0), 0.0) for qr in q_refs], axis=0)
    qs = (qrows * HEAD_DIM ** -0.5).astype(BF16)
    s_scr[:, :past] = _dot(qs, kc_ref[...].astype(BF16))
    s_scr[:, past:] = _dot_nt(qs, _pad_rows(kn_ref[...], pad).astype(BF16))

    t_row = lax.broadcasted_iota(jnp.int32, (rg, past + pad), 0) % t_dec
    col = lax.broadcasted_iota(jnp.int32, (rg, past + pad), 1)
    delta = past + t_row - col
    in_range = col < past + t_dec
    l_g, lse_g = [], []
    for g, (window, dil) in enumerate(B_CONFIGS):
        rs = slice(g * rg, (g + 1) * rg)
        assert dil & (dil - 1) == 0
        valid = in_range & (delta >= 0) & (delta <= window) & ((delta & (dil - 1)) == 0)
        s = jnp.where(valid, s_scr[rs, :], NEG)
        m = jnp.max(s, axis=1, keepdims=True)
        p = jnp.exp(s - m)
        l = jnp.sum(p, axis=1, keepdims=True)
        p_scr[rs, :] = p.astype(BF16)
        l_g.append(l)
        lse_g.append(m + jnp.log(l))
    acc = (_dot_nt(p_scr[:, :past], vc_ref[...].astype(BF16))
           + _dot(p_scr[:, past:], _pad_rows(vn_ref[...], pad).astype(BF16)))
    mx = functools.reduce(jnp.maximum, lse_g)
    e = [jnp.exp(x - mx) for x in lse_g]
    den = functools.reduce(lambda a, b: a + b, e)
    mixed = jnp.zeros((rg, dk), F32)
    for g in range(len(B_CONFIGS)):
        mixed = mixed + (e[g] / den) * (acc[g * rg:(g + 1) * rg] / l_g[g])
    mixed = jnp.where(head_mask, mixed, 0.0)
    out = mixed[0:t_dec]
    for h in range(1, n_heads):
        out = out + mixed[h * t_dec:(h + 1) * t_dec]
    o_ref[...] = out


def _dil_sample(q_groups, k_new, v_new, cache_kt, cache_vt):
    b, t_dec, dk = k_new.shape
    past = cache_kt.shape[2]
    n_heads = dk // HEAD_DIM
    n_rows = len(B_CONFIGS) * n_heads * t_dec
    per_step = 2 if b % 2 == 0 else 1
    tok_spec = pl.BlockSpec((per_step, t_dec, dk), lambda i: (i, 0, 0))
    cache_spec = pl.BlockSpec((per_step, dk, past), lambda i: (i, 0, 0))
    return pl.pallas_call(
        functools.partial(_dil_sample_kernel, n_heads=n_heads),
        grid=(b // per_step,),
        in_specs=[tok_spec] * 5 + [cache_spec] * 2,
        out_specs=tok_spec,
        out_shape=jax.ShapeDtypeStruct((b, t_dec, dk), F32),
        scratch_shapes=[pltpu.VMEM((n_rows, past + LANES), F32),
                        pltpu.VMEM((n_rows, past + LANES), BF16)],
        compiler_params=_params("arbitrary"),
        name="dilated_sample",
    )(*q_groups, k_new, v_new, cache_kt, cache_vt)


def _trunk(x, mods, kv_mod, pos, past, weights):
    (norm_g, w_ffn_gate, w_ffn_up, w_ffn_down, w_qkv_a, w_o_a,
     kv_norm_g, w_kv_b, w_q_b, w_o_b, final_norm_g) = weights
    n_seq, t, d = x.shape
    m = n_seq * t
    depth = norm_g.shape[0]
    n_a = w_qkv_a.shape[0]
    dk = w_kv_b.shape[1] // 2
    is_prompt = past is None
    short = t < 256
    rows = _Rows(m, min(ROWS_LONG if is_prompt else ROWS_SHORT, m if short else t), t)
    rows_p = _Rows(m, min(ROWS_SHORT, m if short else t), t)
    tables = _rope_tables(pos)
    if rows.per_row:
        tables = (tuple(jnp.tile(tb, (n_seq, 1)) for tb in tables[0]),
                  tuple(jnp.tile(tb, (1, n_seq)) for tb in tables[1]))

    def heads_last(x_t):
        return jnp.transpose(x_t.reshape(n_seq, -1, HEAD_DIM, x_t.shape[-1]), (0, 3, 1, 2))

    h = x.reshape(m, d)
    a_k, a_v = [], []
    b_k = b_v = b_k_out = b_v_out = None
    for l in range(depth):
        if l == n_a:
            kvm = rows.prep(kv_mod)
            if is_prompt:
                b_k, b_kt, b_v, b_vt = _proj(rows_p, h, kvm, (0, 1), kv_norm_g, w_kv_b, tables,
                                             ((True, ("f32", "t_f32")), (False, ("f32", "t_f32"))))
                b_k_out, b_v_out = heads_last(b_kt), heads_last(b_vt)
            else:
                b_k, b_v = _proj(rows_p, h, kvm, (0, 1), kv_norm_g, w_kv_b, tables,
                                 ((True, ("f32",)), (False, ("f32",))))
                b_k_out, b_v_out = (r.reshape(n_seq, t, -1, HEAD_DIM) for r in (b_k, b_v))
        md = rows.prep(mods[l])
        h = _ffn(rows, h, md, (0, 1, 2), norm_g[l, 0], w_ffn_gate, w_ffn_up, w_ffn_down, l, 0)
        if l < n_a:
            shp = (n_seq, t, d)
            if is_prompt:
                q_t, k_t, k_bf, km, v_t, vt_bf = _proj(
                    rows_p, h, md, (3, 4), norm_g[l, 1], w_qkv_a[l], tables,
                    ((True, ("t_f32",)), (True, ("t_f32", "bf16", "block_mean")), (False, ("t_f32", "t_bf16"))))
                o = _moba_prompt(q_t, k_bf.reshape(shp), vt_bf, km)
                a_k.append(heads_last(k_t))
                a_v.append(heads_last(v_t))
            else:
                q, k, v = _proj(rows_p, h, md, (3, 4), norm_g[l, 1], w_qkv_a[l], tables,
                                ((True, ("f32",)), (True, ("f32",)), (False, ("f32",))))
                cache_kt, cache_vt = (
                    jnp.transpose(c, (0, 1, 3, 4, 2)).reshape(c.shape[0], c.shape[1], d, c.shape[2])
                    for c in past[:2])
                o = _moba_sample(q.reshape(shp), k.reshape(shp), v.reshape(shp), cache_kt, cache_vt, l, past[2])
                a_k.append(k.reshape(n_seq, t, -1, HEAD_DIM))
                a_v.append(v.reshape(n_seq, t, -1, HEAD_DIM))
            h = _oproj(rows, h, md, 5,w_o_a[l], [o.reshape(m, d)])
        else:
            lb = l - n_a
            qg = _proj(rows_p, h, md, (3, 4), norm_g[l, 1], w_q_b[lb], tables,
                       ((True, ("f32",)),) * len(B_CONFIGS))
            shp = (n_seq, t, dk)
            if is_prompt:
                res = [_dil_prompt(qi.reshape(shp), b_k.reshape(shp), b_v.reshape(shp), w, dl)
                       for qi, (w, dl) in zip(qg, B_CONFIGS)]
                h = _oproj(rows, h, md, 5,w_o_b[lb], [r[0].reshape(m, dk) for r in res],
                           [r[1].reshape(m, dk) for r in res])
            else:
                cb_kt, cb_vt = (jnp.transpose(c, (0, 2, 3, 1)).reshape(n_seq, dk, c.shape[1]) for c in past[3:5])
                o = _dil_sample([qi.reshape(shp) for qi in qg], b_k.reshape(shp), b_v.reshape(shp), cb_kt, cb_vt)
                h = _oproj(rows, h, md, 5,w_o_b[lb], [o.reshape(m, dk)])
        h = _ffn(rows, h, md, (6, 7, 8), norm_g[l, 2], w_ffn_gate, w_ffn_up, w_ffn_down, l, 1,
                 final_g=final_norm_g if l == depth - 1 else None)
    return h.reshape(n_seq, t, d), jnp.stack(a_k), jnp.stack(a_v), b_k_out, b_v_out


def kernel(x_prompt, x_sample, cache_a_k, cache_a_v, cache_b_k, cache_b_v, page_table, c_prompt, c_sample, norm_g, w_mod, b_mod, w_ffn_gate, w_ffn_up, w_ffn_down, w_qkv_a, w_o_a, kv_norm_g, w_kv_mod, b_kv_mod, w_kv_b, w_q_b, w_o_b, final_norm_g):
    bp, t, d = x_prompt.shape
    bs, t_dec, _ = x_sample.shape
    past_len = page_table.shape[1] * cache_a_k.shape[2]
    assert past_len % MOBA_BLOCK == 0 and t_dec <= MOBA_BLOCK

    n_c = bp + bs
    c_all = _pad_rows(jnp.concatenate([c_prompt, c_sample], axis=0), -(-n_c // 8) * 8)
    mods = _modulation(c_all, w_mod, b_mod)
    kv_mod = _modulation(c_all, w_kv_mod[None], b_kv_mod[None])[0]

    bf = lambda w: w.astype(BF16)
    weights = (norm_g, bf(w_ffn_gate), bf(w_ffn_up), bf(w_ffn_down), bf(w_qkv_a), bf(w_o_a),
               kv_norm_g, bf(w_kv_b), bf(w_q_b), bf(w_o_b), final_norm_g)
    y_p, ak_p, av_p, bk_p, bv_p = _trunk(
        x_prompt, mods[:, :bp], kv_mod[:bp], jnp.arange(t, dtype=jnp.int32), None, weights)
    y_s, ak_s, av_s, bk_s, bv_s = _trunk(
        x_sample, mods[:, bp:n_c], kv_mod[bp:n_c], past_len + jnp.arange(t_dec, dtype=jnp.int32),
        (cache_a_k, cache_a_v, page_table, cache_b_k, cache_b_v), weights)

    keep = min(max(w for w, _ in B_CONFIGS), t)
    return (y_p, y_s, ak_p, av_p, ak_s, av_s, bk_p[:, -keep:], bv_p[:, -keep:], bk_s, bv_s)
```

```python
import functools

import jax
import jax.numpy as jnp
from jax import lax
from jax.experimental import pallas as pl
from jax.experimental.pallas import tpu as pltpu

HEAD_DIM = 64
ROT_DIM = HEAD_DIM // 4
ROPE_THETA = 500000.0
MOBA_BLOCK = 256
MOBA_TOPK = 3
B_CONFIGS = ((128, 1), (512, 4), (2048, 16))
EPS = 1e-6
NEG = -1e30
LOG2_E = 1.4426950408889634

LANES = 128
HEADS_PER_TILE = LANES // HEAD_DIM
MXU_COLS = 256
ROWS_LONG, ROWS_SHORT = 1024, 512
DIL_ITEMS = 4
DIL_GROUPS = 4
VMEM_LIMIT = 56 * 1024 * 1024

F32 = jnp.float32
BF16 = jnp.bfloat16


def _params(*sem):
    return pltpu.CompilerParams(dimension_semantics=sem, vmem_limit_bytes=VMEM_LIMIT)


def _dot(a, b):
    return jnp.dot(a, b, preferred_element_type=F32)


def _dot_nt(a, b):
    return lax.dot_general(a, b, (((1,), (1,)), ((), ())), preferred_element_type=F32)


def _silu(x):
    return x * jax.nn.sigmoid(x)


def _rmsnorm(x, g):
    return x * lax.rsqrt(jnp.mean(x * x, axis=-1, keepdims=True) + EPS) * g


def _rms_mod(x, g, shift, scale):
    return _rmsnorm(x, g) * (1.0 + scale) + shift


def _head_lane_mask(shape, rows_per_head):
    row = lax.broadcasted_iota(jnp.int32, shape, 0)
    lane = lax.broadcasted_iota(jnp.int32, shape, 1)
    return (lane // HEAD_DIM) == (row // rows_per_head)


class _Rows:
    def __init__(self, m, tm, rows_per_seq):
        self.m, self.tm = m, tm
        self.per_row = rows_per_seq < tm
        self.rows_per_seq = rows_per_seq
        self.tiles_per_seq = max(rows_per_seq // tm, 1)

    def prep(self, vecs):
        if self.per_row:
            return jnp.repeat(vecs, self.rows_per_seq, axis=0)
        return vecs[:, None, :]

    def mod_spec(self, d, col):
        if self.per_row:
            return pl.BlockSpec((self.tm, d), lambda i, j: (i, col))
        tps = self.tiles_per_seq
        return pl.BlockSpec((None, 1, d), lambda i, j: (i // tps, 0, col))

    def pos_spec(self, table_rows):
        n_pos_tiles = table_rows // self.tm
        return pl.BlockSpec((self.tm, LANES), lambda i, j: (i % n_pos_tiles, 0))


def _mod_kernel(c_ref, w_ref, b_ref, o_ref):
    cs = _silu(c_ref[...]).astype(BF16)
    o_ref[...] = _dot(cs, w_ref[...].astype(BF16)) + b_ref[...]


def _modulation(c, w, b):
    mc, d = c.shape
    nl, _, n = w.shape
    tn = d
    return pl.pallas_call(
        _mod_kernel,
        grid=(nl, n // tn),
        in_specs=[pl.BlockSpec((mc, d), lambda l, j: (0, 0)),
                  pl.BlockSpec((None, d, tn), lambda l, j: (l, 0, j)),
                  pl.BlockSpec((None, 1, tn), lambda l, j: (l, 0, j))],
        out_specs=pl.BlockSpec((None, mc, tn), lambda l, j: (l, 0, j)),
        out_shape=jax.ShapeDtypeStruct((nl, mc, n), F32),
        compiler_params=_params("arbitrary", "arbitrary"),
        name="modulation",
    )(c, w, b.reshape(nl, 1, n))


def _ffn_kernel(h_ref, sh_ref, sc_ref, gt_ref, g_ref, wg_ref, wu_ref, wd_ref, *rest, final):
    if final:
        fg_ref, o_ref, u_scr, act_scr = rest
    else:
        o_ref, u_scr, act_scr = rest
    d, ff = wg_ref.shape
    tf = MXU_COLS if ff % MXU_COLS == 0 else LANES
    to = min(2 * MXU_COLS, d)
    u_scr[...] = _rms_mod(h_ref[...], g_ref[...], sh_ref[...], sc_ref[...]).astype(BF16)
    for c in range(ff // tf):
        sl = slice(c * tf, (c + 1) * tf)
        u = u_scr[...]
        a = _dot(u, wg_ref[:, sl])
        b = _dot(u, wu_ref[:, sl])
        act_scr[:, sl] = (_silu(a) * b).astype(BF16)
    for c in range(d // to):
        sl = slice(c * to, (c + 1) * to)
        down = _dot(act_scr[...], wd_ref[:, sl])
        o_ref[:, sl] = h_ref[:, sl] + 0.5 * gt_ref[:, sl] * down
    if final:
        o_ref[...] = _rmsnorm(o_ref[...], fg_ref[...])


def _ffn(rows, h, mod, cols, g, wg, wu, wd, l, s, final_g=None):
    m, d = h.shape
    ff = wg.shape[-1]
    tm = rows.tm
    final = final_g is not None
    resident = lambda shape: pl.BlockSpec((None, None) + shape, lambda i, f: (l, s, 0, 0),
                                          pipeline_mode=pl.Buffered(1))
    in_specs = [pl.BlockSpec((tm, d), lambda i, f: (i, 0)),
                rows.mod_spec(d, cols[0]), rows.mod_spec(d, cols[1]), rows.mod_spec(d, cols[2]),
                pl.BlockSpec((1, d), lambda i, f: (0, 0)),
                resident((d, ff)), resident((d, ff)), resident((ff, d))]
    args = [h, mod, mod, mod, g.reshape(1, d), wg, wu, wd]
    if final:
        in_specs.append(pl.BlockSpec((1, d), lambda i, f: (0, 0)))
        args.append(final_g.reshape(1, d))
    return pl.pallas_call(
        functools.partial(_ffn_kernel, final=final),
        grid=(m // tm, 1),
        in_specs=in_specs,
        out_specs=pl.BlockSpec((tm, d), lambda i, f: (i, 0)),
        out_shape=jax.ShapeDtypeStruct((m, d), F32),
        scratch_shapes=[pltpu.VMEM((tm, d), BF16), pltpu.VMEM((tm, ff), BF16)],
        compiler_params=_params("arbitrary", "arbitrary"),
        name="ffn",
    )(*args)


def _rope_tables(pos):
    half = ROT_DIM // 2
    n = pos.shape[0]
    freq = ROPE_THETA ** (-jnp.arange(half, dtype=F32) / half)
    ang = pos.astype(F32)[:, None] * freq[None, :]
    cos, sin = jnp.cos(ang), jnp.sin(ang)
    rest = HEAD_DIM - ROT_DIM
    ct = jnp.concatenate([cos, cos, jnp.ones((n, rest), F32)], axis=1)
    s_lo = jnp.concatenate([-sin, jnp.zeros((n, half + rest), F32)], axis=1)
    s_hi = jnp.concatenate([jnp.zeros((n, half), F32), sin, jnp.zeros((n, rest), F32)], axis=1)
    tile = lambda t: jnp.tile(t, (1, HEADS_PER_TILE))
    return (tile(ct), tile(s_hi), tile(s_lo)), (cos.T, sin.T)


def _rope_tile(x, ct, s_hi, s_lo):
    half = ROT_DIM // 2
    return x * ct + pltpu.roll(x, half, 1) * s_hi + pltpu.roll(x, LANES - half, 1) * s_lo


def _rope_t(x_t, cos_t, sin_t):
    half = ROT_DIM // 2
    pieces = []
    for base in range(0, x_t.shape[0], HEAD_DIM):
        x1, x2 = x_t[base:base + half], x_t[base + half:base + ROT_DIM]
        pieces += [x1 * cos_t - x2 * sin_t, x2 * cos_t + x1 * sin_t, x_t[base + ROT_DIM:base + HEAD_DIM]]
    return jnp.concatenate(pieces, axis=0)


_MEAN_GROUP = 8


def _proj_kernel(h_ref, sh_ref, sc_ref, g_ref, w_ref, ct_ref, shi_ref, slo_ref, cos_t_ref, sin_t_ref,
                 *outs, parts):
    i = pl.program_id(0)
    u = _rms_mod(h_ref[...], g_ref[...], sh_ref[...], sc_ref[...]).astype(BF16)
    tm = u.shape[0]
    n_p = w_ref.shape[1] // len(parts)
    blocks = tm // MOBA_BLOCK
    n_out = 0
    for p, (rope, forms) in enumerate(parts):
        part_outs = outs[n_out:n_out + len(forms)]
        n_out += len(forms)
        y = _dot(u, w_ref[:, p * n_p:(p + 1) * n_p])
        val = val_t = None
        if any(f.startswith("t_") for f in forms):
            val_t = y.T
            if rope:
                val_t = _rope_t(val_t, cos_t_ref[...], sin_t_ref[...])
            if any(not f.startswith("t_") for f in forms):
                val = val_t.T
        elif rope:
            ct, s_hi, s_lo = ct_ref[...], shi_ref[...], slo_ref[...]
            val = jnp.concatenate([_rope_tile(y[:, c * LANES:(c + 1) * LANES], ct, s_hi, s_lo)
                                   for c in range(n_p // LANES)], axis=1)
        else:
            val = y
        for form, out in zip(forms, part_outs):
            if form == "f32":
                out[...] = val
            elif form == "bf16":
                out[...] = val.astype(BF16)
            elif form == "t_f32":
                out[...] = val_t
            elif form == "t_bf16":
                for c in range(blocks):
                    out[c] = val_t[:, c * MOBA_BLOCK:(c + 1) * MOBA_BLOCK].astype(BF16)
            elif form == "block_mean":
                base = (i % (_MEAN_GROUP // blocks)) * blocks
                for c in range(blocks):
                    out[pl.ds(base + c, 1), :] = jnp.mean(
                        val[c * MOBA_BLOCK:(c + 1) * MOBA_BLOCK], axis=0, keepdims=True)


def _proj(rows, h, mod, cols, g, w, tables, parts):
    m, d = h.shape
    n_p = w.shape[1] // len(parts)
    tm, tps, t = rows.tm, rows.tiles_per_seq, rows.rows_per_seq
    n_seq = m // t
    out_specs, out_shape = [], []
    for _, forms in parts:
        for form in forms:
            if form in ("f32", "bf16"):
                out_specs.append(pl.BlockSpec((tm, n_p), lambda i, j: (i, 0)))
                out_shape.append(jax.ShapeDtypeStruct((m, n_p), F32 if form == "f32" else BF16))
                continue
            assert not rows.per_row and tm % MOBA_BLOCK == 0
            if form == "t_f32":
                out_specs.append(pl.BlockSpec((None, n_p, tm), lambda i, j: (i // tps, 0, i % tps)))
                out_shape.append(jax.ShapeDtypeStruct((n_seq, n_p, t), F32))
            elif form == "t_bf16":
                out_specs.append(pl.BlockSpec((None, tm // MOBA_BLOCK, n_p, MOBA_BLOCK),
                                              lambda i, j: (i // tps, i % tps, 0, 0)))
                out_shape.append(jax.ShapeDtypeStruct((n_seq, t // MOBA_BLOCK, n_p, MOBA_BLOCK), BF16))
            else:
                steps = _MEAN_GROUP * MOBA_BLOCK // tm
                assert form == "block_mean" and tps % steps == 0
                out_specs.append(pl.BlockSpec((None, _MEAN_GROUP, n_p),
                                              lambda i, j: (i // tps, (i % tps) // steps, 0)))
                out_shape.append(jax.ShapeDtypeStruct((n_seq, t // MOBA_BLOCK, n_p), F32))
    lane_tables, t_tables = tables
    pos_spec = rows.pos_spec(lane_tables[0].shape[0])
    n_pos_tiles = lane_tables[0].shape[0] // tm
    pos_t_spec = pl.BlockSpec((ROT_DIM // 2, tm), lambda i, j: (0, i % n_pos_tiles))
    return pl.pallas_call(
        functools.partial(_proj_kernel, parts=parts),
        grid=(m // tm, 1),
        in_specs=[pl.BlockSpec((tm, d), lambda i, j: (i, 0)),
                  rows.mod_spec(d, cols[0]), rows.mod_spec(d, cols[1]),
                  pl.BlockSpec((1, d), lambda i, j: (0, 0)),
                  pl.BlockSpec(w.shape, lambda i, j: (0, 0)),
                  pos_spec, pos_spec, pos_spec, pos_t_spec, pos_t_spec],
        out_specs=out_specs,
        out_shape=out_shape,
        compiler_params=_params("arbitrary", "arbitrary"),
        name="proj",
    )(h, mod, mod, g.reshape(1, d), w, *lane_tables, *t_tables)


def _oproj_kernel(h_ref, gt_ref, w_ref, *rest, n_groups):
    o_ref = rest[-1]
    if n_groups == 0:
        o = rest[0][...]
    else:
        os_, lses = rest[:n_groups], rest[n_groups:2 * n_groups]
        lse = [r[...] for r in lses]
        mx = functools.reduce(jnp.maximum, lse)
        e = [jnp.exp(x - mx) for x in lse]
        den = functools.reduce(lambda a, b: a + b, e)
        o = functools.reduce(lambda a, b: a + b, [(ei / den) * r[...] for ei, r in zip(e, os_)])
    o_ref[...] = h_ref[...] + gt_ref[...] * _dot(o.astype(BF16), w_ref[...])


def _oproj(rows, h, mod, col, w, o_list, lse_list=()):
    m, d = h.shape
    k = w.shape[0]
    tm = rows.tm
    xs = list(o_list) + list(lse_list)
    return pl.pallas_call(
        functools.partial(_oproj_kernel, n_groups=len(lse_list)),
        grid=(m // tm, 1),
        in_specs=[pl.BlockSpec((tm, d), lambda i, j: (i, 0)),
                  rows.mod_spec(d, col),
                  pl.BlockSpec((k, d), lambda i, j: (0, 0))]
                 + [pl.BlockSpec((tm, k), lambda i, j: (i, 0))] * len(xs),
        out_specs=pl.BlockSpec((tm, d), lambda i, j: (i, 0)),
        out_shape=jax.ShapeDtypeStruct((m, d), F32),
        compiler_params=_params("arbitrary", "arbitrary"),
        name="oproj",
    )(h, mod, w, *xs)


def _split_bf16(x):
    hi = x.astype(BF16)
    lo = (x - hi.astype(F32)).astype(BF16)
    return hi, lo


def _top_blocks_t(gate, n_past):
    blk_id = lax.broadcasted_iota(jnp.int32, gate.shape, 0).astype(F32)
    n_past = n_past.astype(F32)
    g = jnp.where(blk_id < n_past, gate, NEG)
    sel = jnp.zeros(gate.shape, jnp.bool_)
    for _ in range(MOBA_TOPK):
        mx = jnp.max(g, axis=0, keepdims=True)
        idx = jnp.min(jnp.where(g == mx, blk_id, float(gate.shape[0])), axis=0, keepdims=True)
        pick = blk_id == idx
        sel = sel | pick
        g = jnp.where(pick, -jnp.inf, g)
    return sel & (blk_id < n_past)


def _moba_prompt_kernel(q_ref, k_ref, vt_ref, km_ref, o_ref,
                        qt_scr, sel_scr, m_scr, l_scr, acc_scr, s_own, s_even, s_odd):
    blk = MOBA_BLOCK
    n_h = qt_scr.shape[0]
    tiles = [slice(hh // HEADS_PER_TILE * LANES, (hh // HEADS_PER_TILE + 1) * LANES) for hh in range(n_h)]
    nb = vt_ref.shape[0]
    i = pl.program_id(2)
    heads = range(n_h)
    hs = [slice(hh * HEAD_DIM, (hh + 1) * HEAD_DIM) for hh in heads]

    def scores_into(dst, first, n_blocks):
        rows = n_blocks * blk
        jj = jnp.minimum(first, nb - n_blocks)
        kb = k_ref[pl.ds(pl.multiple_of(jj * blk, blk), rows), :]
        for hh in heads:
            dst[hh] = _dot(kb[:, tiles[hh]], qt_scr[hh])

    def consume(src, part, j, own):
        jj = jnp.minimum(j, nb - 1)
        vtb = vt_ref[jj]
        ps, alphas, chosen = [], [], []
        for hh in heads:
            s = src[hh, part * blk:(part + 1) * blk, :]
            if own:
                key_id = lax.broadcasted_iota(jnp.int32, (blk, blk), 0)
                qry_id = lax.broadcasted_iota(jnp.int32, (blk, blk), 1)
                s = jnp.where(key_id <= qry_id, s, NEG)
                ch = jnp.full((1, blk), True)
            else:
                ch = (sel_scr[hh, pl.ds(jj, 1), :] > 0.5) & (j < i)
            m_old = m_scr[hh]
            m_all = jnp.maximum(m_old, jnp.max(s, axis=0, keepdims=True))
            p = jnp.exp2(s - m_all)
            m_new = jnp.where(ch, m_all, m_old)
            alpha = jnp.exp2(m_old - m_new)
            l_scr[hh] = alpha * l_scr[hh] + jnp.where(ch, jnp.sum(p, axis=0, keepdims=True), 0.0)
            m_scr[hh] = m_new
            ps.append(p.astype(BF16))
            alphas.append(alpha)
            chosen.append(ch)
        pvs = [_dot(vtb[hs[hh], :], ps[hh]) for hh in heads]
        for hh in heads:
            acc_scr[hh] = alphas[hh] * acc_scr[hh] + jnp.where(chosen[hh], pvs[hh], 0.0)

    feat = lax.broadcasted_iota(jnp.int32, (LANES, blk), 0)
    for hh in heads:
        qth = jnp.where((feat // HEAD_DIM) == hh % HEADS_PER_TILE, q_ref[tiles[hh], :], 0.0)
        qt_scr[hh] = (qth * (HEAD_DIM ** -0.5 * LOG2_E)).astype(BF16)
        q_hi, q_lo = _split_bf16(qth)
        km_hi, km_lo = _split_bf16(km_ref[:, tiles[hh]])
        by_q_hi = _dot(jnp.concatenate([km_hi, km_lo], axis=0), q_hi)
        gate = by_q_hi[:nb] + by_q_hi[nb:] + _dot(km_hi, q_lo)
        sel_scr[hh] = _top_blocks_t(gate, i).astype(F32)
        m_scr[hh] = jnp.full((1, blk), NEG, F32)
        l_scr[hh] = jnp.zeros((1, blk), F32)
        acc_scr[hh] = jnp.zeros((HEAD_DIM, blk), F32)
    scores_into(s_own, i, 1)
    scores_into(s_even, 0, 2)
    consume(s_own, 0, i, own=True)

    def pair_from(cur, nxt):
        def run(c):
            scores_into(nxt, 2 * c + 2, 2)
            consume(cur, 0, 2 * c, own=False)
            consume(cur, 1, 2 * c + 1, own=False)
        return run

    def two_blocks(c, carry):
        lax.cond(c % 2 == 0, pair_from(s_even, s_odd), pair_from(s_odd, s_even), c)
        return carry

    lax.fori_loop(0, (i + 1) // 2, two_blocks, 0)
    ot = jnp.concatenate([acc_scr[hh] / l_scr[hh] for hh in heads], axis=0)
    o_ref[...] = ot.T


def _moba_prompt(q_t, k_bf, vt_bf, km):
    b, d, t = q_t.shape
    blk = MOBA_BLOCK
    nb = t // blk
    width = min(d, 2 * LANES)
    n_h = width // HEAD_DIM
    assert t % blk == 0 and nb % 8 == 0 and d % width == 0
    tile_spec = pl.BlockSpec((None, blk, width), lambda bi, hg, i: (bi, i, hg))
    return pl.pallas_call(
        _moba_prompt_kernel,
        grid=(b, d // width, nb),
        in_specs=[pl.BlockSpec((None, width, blk), lambda bi, hg, i: (bi, hg, i)),
                  pl.BlockSpec((None, t, width), lambda bi, hg, i: (bi, 0, hg)),
                  pl.BlockSpec((None, nb, width, blk), lambda bi, hg, i: (bi, 0, hg, 0)),
                  pl.BlockSpec((None, nb, width), lambda bi, hg, i: (bi, 0, hg))],
        out_specs=tile_spec,
        out_shape=jax.ShapeDtypeStruct((b, t, d), F32),
        scratch_shapes=[pltpu.VMEM((n_h, LANES, blk), BF16),
                        pltpu.VMEM((n_h, nb, blk), F32),
                        pltpu.VMEM((n_h, 1, blk), F32),
                        pltpu.VMEM((n_h, 1, blk), F32),
                        pltpu.VMEM((n_h, HEAD_DIM, blk), F32)]
                       + [pltpu.VMEM((n_h, blk, blk), F32)] + [pltpu.VMEM((n_h, 2 * blk, blk), F32)] * 2,
        compiler_params=_params("arbitrary", "arbitrary", "arbitrary"),
        name="moba_prompt",
    )(q_t, k_bf, vt_bf, km)


def _pad_rows(x, n):
    return jnp.concatenate([x, jnp.zeros((n - x.shape[0], x.shape[1]), x.dtype)], axis=0)


def _moba_sample_kernel(pt_ref, q_ref, kn_ref, vn_ref, *rest, n_pages, page, n_heads):
    k_pages, v_pages = rest[:n_pages], rest[n_pages:2 * n_pages]
    o_ref, s_scr = rest[2 * n_pages], rest[2 * n_pages + 1]
    t_dec, d = q_ref.shape
    r = n_heads * t_dec
    ppb = MOBA_BLOCK // page
    n_past = n_pages // ppb
    own = lax.broadcasted_iota(jnp.int32, (r, d), 0) // t_dec
    head_mask = (lax.broadcasted_iota(jnp.int32, (r, d), 1) // HEAD_DIM) == own
    qrows = jnp.where(head_mask, jnp.concatenate([q_ref[...]] * n_heads, axis=0), 0.0)
    qs = (qrows * HEAD_DIM ** -0.5).astype(BF16)

    def block_of(pages, j):
        return jnp.concatenate([pages[p][...].astype(BF16) for p in range(j * ppb, (j + 1) * ppb)], axis=1)

    gates = []
    for j in range(n_past):
        sl = slice(j * MOBA_BLOCK, (j + 1) * MOBA_BLOCK)
        s_scr[:, sl] = _dot(qs, block_of(k_pages, j))
        gates.append(jnp.sum(s_scr[:, sl], axis=1, keepdims=True))
    for j in range(n_past):
        rank = jnp.zeros((r, 1), jnp.int32)
        for j2 in range(n_past):
            if j2 != j:
                ahead = (gates[j2] > gates[j]) | ((gates[j2] == gates[j]) & (j2 < j))
                rank = rank + ahead.astype(jnp.int32)
        chosen = rank < MOBA_TOPK
        sl = slice(j * MOBA_BLOCK, (j + 1) * MOBA_BLOCK)
        s_scr[:, sl] = jnp.where(chosen, s_scr[:, sl], NEG)
    t_row = lax.broadcasted_iota(jnp.int32, (r, page), 0) % t_dec
    t_col = lax.broadcasted_iota(jnp.int32, (r, page), 1)
    s_own = _dot_nt(qs, _pad_rows(kn_ref[...], page).astype(BF16))
    s_scr[:, n_pages * page:] = jnp.where(t_col <= t_row, s_own, NEG)

    s = s_scr[...]
    m = jnp.max(s, axis=1, keepdims=True)
    p_all = jnp.exp(s - m)
    l = jnp.sum(p_all, axis=1, keepdims=True)
    acc = _dot(p_all[:, n_pages * page:].astype(BF16), _pad_rows(vn_ref[...], page).astype(BF16))
    for j in range(n_past):
        acc = acc + _dot_nt(p_all[:, j * MOBA_BLOCK:(j + 1) * MOBA_BLOCK].astype(BF16), block_of(v_pages, j))
    o_full = jnp.where(head_mask, acc / l, 0.0)
    out = o_full[0:t_dec]
    for h in range(1, n_heads):
        out = out + o_full[h * t_dec:(h + 1) * t_dec]
    o_ref[...] = out


def _moba_sample(q, k_new, v_new, cache_kt, cache_vt, layer, page_table):
    b, t_dec, d = q.shape
    n_pages = page_table.shape[1]
    page = cache_kt.shape[3]
    n_heads = d // HEAD_DIM
    assert MOBA_BLOCK % page == 0 and (n_pages * page) % MOBA_BLOCK == 0 and t_dec <= page
    tok_spec = pl.BlockSpec((None, t_dec, d), lambda i, pt: (i, 0, 0))
    page_specs = [pl.BlockSpec((None, None, d, page),
                               functools.partial(lambda i, pt, p: (layer, pt[i, p], 0, 0), p=p))
                  for p in range(n_pages)]
    grid_spec = pltpu.PrefetchScalarGridSpec(
        num_scalar_prefetch=1,
        grid=(b,),
        in_specs=[tok_spec, tok_spec, tok_spec] + page_specs + page_specs,
        out_specs=tok_spec,
        scratch_shapes=[pltpu.VMEM((n_heads * t_dec, (n_pages + 1) * page), F32)],
    )
    return pl.pallas_call(
        functools.partial(_moba_sample_kernel, n_pages=n_pages, page=page, n_heads=n_heads),
        grid_spec=grid_spec,
        out_shape=jax.ShapeDtypeStruct((b, t_dec, d), F32),
        compiler_params=_params("arbitrary"),
        name="moba_sample",
    )(page_table, q, k_new, v_new, *([cache_kt] * n_pages), *([cache_vt] * n_pages))


def _dil_prompt_kernel(q_ref, k_ref, v_ref, o_ref, lse_ref, *, dil, tq, sub, width, win):
    length = k_ref.shape[0] // dil
    base = pl.program_id(2) * (sub * tq)
    lane = lax.broadcasted_iota(jnp.int32, (tq, LANES), 1)
    row_col = (lax.broadcasted_iota(jnp.int32, (tq, width), 0)
               - lax.broadcasted_iota(jnp.int32, (tq, width), 1))
    heads = range(HEADS_PER_TILE)

    def rows(start, n):
        return pl.ds(start, n) if dil == 1 else pl.ds(start, n, stride=dil)

    def attend(items):
        qs, kws, vws, valids = [], [], [], []
        for r, u in items:
            q0 = base + u * tq
            ks = jnp.clip(q0 - win, 0, length - width)
            q = q_ref[rows(u * tq * dil + r, tq), :]
            qs.append([(jnp.where((lane // HEAD_DIM) == hh, q, 0.0) * (HEAD_DIM ** -0.5 * LOG2_E)).astype(BF16)
                       for hh in heads])
            kws.append(k_ref[rows(ks * dil + r, width), :].astype(BF16))
            vws.append(v_ref[rows(ks * dil + r, width), :].astype(BF16))
            delta = q0 - ks + row_col
            valids.append((delta >= 0) & (delta <= win))
        n = range(len(items))
        scores = [_dot_nt(jnp.concatenate(qs[it], axis=0), kws[it]) for it in n]
        ps, ls, ms = [], [], []
        for it in n:
            for hh in heads:
                s = jnp.where(valids[it], scores[it][hh * tq:(hh + 1) * tq], NEG)
                m = jnp.max(s, axis=1, keepdims=True)
                p = jnp.exp2(s - m)
                ls.append(jnp.sum(p, axis=1, keepdims=True))
                ms.append(m * (1.0 / LOG2_E))
                ps.append(p.astype(BF16))
        stacked = [_dot(jnp.concatenate(ps[it * len(heads):(it + 1) * len(heads)], axis=0), vws[it]) for it in n]
        pvs = [stacked[it][hh * tq:(hh + 1) * tq] for it in n for hh in heads]
        for it, (r, u) in enumerate(items):
            c0, c1 = it * len(heads), it * len(heads) + 1
            o_ref[rows(u * tq * dil + r, tq), :] = jnp.where(lane < HEAD_DIM, pvs[c0] / ls[c0], pvs[c1] / ls[c1])
            lse_ref[rows(u * tq * dil + r, tq), :] = jnp.where(
                lane < HEAD_DIM, jnp.broadcast_to(ms[c0] + jnp.log(ls[c0]), (tq, LANES)),
                jnp.broadcast_to(ms[c1] + jnp.log(ls[c1]), (tq, LANES)))

    if dil * sub <= DIL_ITEMS * DIL_GROUPS:
        items = [(r, u) for u in range(sub) for r in range(dil)]
        for c in range(0, len(items), DIL_ITEMS):
            attend(items[c:c + DIL_ITEMS])
    else:
        per_iter = min(dil, DIL_ITEMS)

        def some_classes(c, carry):
            attend([(per_iter * c + x, u) for x in range(per_iter) for u in range(sub)])
            return carry
        lax.fori_loop(0, dil // per_iter, some_classes, 0)


def _dil_prompt(q, k, v, window, dil):
    b, t, dk = q.shape
    assert t % dil == 0 and HEADS_PER_TILE == 2 and (dil == 1 or dil % 2 == 0)
    length = t // dil
    win = window // dil
    tq = min(LANES, length)
    sub = max(DIL_ITEMS * DIL_GROUPS // dil, 1)
    if length % (sub * tq):
        sub = 1
    assert dil * sub <= DIL_ITEMS * DIL_GROUPS or (sub == 1 and dil % DIL_ITEMS == 0)
    width = min(tq + win, length)
    assert length % (sub * tq) == 0
    tile_spec = pl.BlockSpec((None, sub * tq * dil, LANES), lambda bi, h, i: (bi, i, h))
    seq_spec = pl.BlockSpec((None, t, LANES), lambda bi, h, i: (bi, 0, h))
    return pl.pallas_call(
        functools.partial(_dil_prompt_kernel, dil=dil, tq=tq, sub=sub, width=width, win=win),
        grid=(b, dk // LANES, length // (sub * tq)),
        in_specs=[tile_spec, seq_spec, seq_spec],
        out_specs=[tile_spec, tile_spec],
        out_shape=[jax.ShapeDtypeStruct((b, t, dk), F32)] * 2,
        compiler_params=_params("arbitrary", "arbitrary", "arbitrary"),
        name="dilated_prompt",
    )(q, k, v)


def _dil_sample_kernel(q0_ref, q1_ref, q2_ref, kn_ref, vn_ref, kc_ref, vc_ref, o_ref, s_scr, p_scr, *, n_heads):
    for i in range(o_ref.shape[0]):
        _dil_sample_one((q0_ref.at[i], q1_ref.at[i], q2_ref.at[i]), kn_ref.at[i], vn_ref.at[i],
                        kc_ref.at[i], vc_ref.at[i], o_ref.at[i], s_scr, p_scr, n_heads)


def _dil_sample_one(q_refs, kn_ref, vn_ref, kc_ref, vc_ref, o_ref, s_scr, p_scr, n_heads):
    t_dec, dk = kn_ref.shape
    past = kc_ref.shape[1]
    pad = LANES
    rg = n_heads * t_dec
    head_mask = _head_lane_mask((rg, dk), t_dec)
    qrows = jnp.concatenate(
        [jnp.where(head_mask, jnp.concatenate([qr[...]] * n_heads, axis=0), 0.0) for qr in q_refs], axis=0)
    qs = (qrows * HEAD_DIM ** -0.5).astype(BF16)
    s_scr[:, :past] = _dot(qs, kc_ref[...].astype(BF16))
    s_scr[:, past:] = _dot_nt(qs, _pad_rows(kn_ref[...], pad).astype(BF16))

    t_row = lax.broadcasted_iota(jnp.int32, (rg, past + pad), 0) % t_dec
    col = lax.broadcasted_iota(jnp.int32, (rg, past + pad), 1)
    delta = past + t_row - col
    in_range = col < past + t_dec
    l_g, lse_g = [], []
    for g, (window, dil) in enumerate(B_CONFIGS):
        rs = slice(g * rg, (g + 1) * rg)
        assert dil & (dil - 1) == 0
        valid = in_range & (delta >= 0) & (delta <= window) & ((delta & (dil - 1)) == 0)
        s = jnp.where(valid, s_scr[rs, :], NEG)
        m = jnp.max(s, axis=1, keepdims=True)
        p = jnp.exp(s - m)
        l = jnp.sum(p, axis=1, keepdims=True)
        p_scr[rs, :] = p.astype(BF16)
        l_g.append(l)
        lse_g.append(m + jnp.log(l))
    acc = (_dot_nt(p_scr[:, :past], vc_ref[...].astype(BF16))
           + _dot(p_scr[:, past:], _pad_rows(vn_ref[...], pad).astype(BF16)))
    mx = functools.reduce(jnp.maximum, lse_g)
    e = [jnp.exp(x - mx) for x in lse_g]
    den = functools.reduce(lambda a, b: a + b, e)
    mixed = jnp.zeros((rg, dk), F32)
    for g in range(len(B_CONFIGS)):
        mixed = mixed + (e[g] / den) * (acc[g * rg:(g + 1) * rg] / l_g[g])
    mixed = jnp.where(head_mask, mixed, 0.0)
    out = mixed[0:t_dec]
    for h in range(1, n_heads):
        out = out + mixed[h * t_dec:(h + 1) * t_dec]
    o_ref[...] = out


def _dil_sample(q_groups, k_new, v_new, cache_kt, cache_vt):
    b, t_dec, dk = k_new.shape
    past = cache_kt.shape[2]
    n_heads = dk // HEAD_DIM
    n_rows = len(B_CONFIGS) * n_heads * t_dec
    per_step = 2 if b % 2 == 0 else 1
    tok_spec = pl.BlockSpec((per_step, t_dec, dk), lambda i: (i, 0, 0))
    cache_spec = pl.BlockSpec((per_step, dk, past), lambda i: (i, 0, 0))
    return pl.pallas_call(
        functools.partial(_dil_sample_kernel, n_heads=n_heads),
        grid=(b // per_step,),
        in_specs=[tok_spec] * 5 + [cache_spec] * 2,
        out_specs=tok_spec,
        out_shape=jax.ShapeDtypeStruct((b, t_dec, dk), F32),
        scratch_shapes=[pltpu.VMEM((n_rows, past + LANES), F32),
                        pltpu.VMEM((n_rows, past + LANES), BF16)],
        compiler_params=_params("arbitrary"),
        name="dilated_sample",
    )(*q_groups, k_new, v_new, cache_kt, cache_vt)


def _trunk(x, mods, kv_mod, pos, past, weights):
    (norm_g, w_ffn_gate, w_ffn_up, w_ffn_down, w_qkv_a, w_o_a,
     kv_norm_g, w_kv_b, w_q_b, w_o_b, final_norm_g) = weights
    n_seq, t, d = x.shape
    m = n_seq * t
    depth = norm_g.shape[0]
    n_a = w_qkv_a.shape[0]
    dk = w_kv_b.shape[1] // 2
    is_prompt = past is None
    short = t < 256
    rows = _Rows(m, min(ROWS_LONG if is_prompt else ROWS_SHORT, m if short else t), t)
    rows_p = _Rows(m, min(ROWS_SHORT, m if short else t), t)
    tables = _rope_tables(pos)
    if rows.per_row:
        tables = (tuple(jnp.tile(tb, (n_seq, 1)) for tb in tables[0]),
                  tuple(jnp.tile(tb, (1, n_seq)) for tb in tables[1]))

    def heads_last(x_t):
        return jnp.transpose(x_t.reshape(n_seq, -1, HEAD_DIM, x_t.shape[-1]), (0, 3, 1, 2))

    h = x.reshape(m, d)
    a_k, a_v = [], []
    b_k = b_v = b_k_out = b_v_out = None
    for l in range(depth):
        if l == n_a:
            kvm = rows.prep(kv_mod)
            if is_prompt:
                b_k, b_kt, b_v, b_vt = _proj(rows_p, h, kvm, (0, 1), kv_norm_g, w_kv_b, tables,
                                             ((True, ("f32", "t_f32")), (False, ("f32", "t_f32"))))
                b_k_out, b_v_out = heads_last(b_kt), heads_last(b_vt)
            else:
                b_k, b_v = _proj(rows_p, h, kvm, (0, 1), kv_norm_g, w_kv_b, tables,
                                 ((True, ("f32",)), (False, ("f32",))))
                b_k_out, b_v_out = (r.reshape(n_seq, t, -1, HEAD_DIM) for r in (b_k, b_v))
        md = rows.prep(mods[l])
        h = _ffn(rows, h, md, (0, 1, 2), norm_g[l, 0], w_ffn_gate, w_ffn_up, w_ffn_down, l, 0)
        if l < n_a:
            shp = (n_seq, t, d)
            if is_prompt:
                q_t, k_t, k_bf, km, v_t, vt_bf = _proj(
                    rows_p, h, md, (3, 4), norm_g[l, 1], w_qkv_a[l], tables,
                    ((True, ("t_f32",)), (True, ("t_f32", "bf16", "block_mean")), (False, ("t_f32", "t_bf16"))))
                o = _moba_prompt(q_t, k_bf.reshape(shp), vt_bf, km)
                a_k.append(heads_last(k_t))
                a_v.append(heads_last(v_t))
            else:
                q, k, v = _proj(rows_p, h, md, (3, 4), norm_g[l, 1], w_qkv_a[l], tables,
                                ((True, ("f32",)), (True, ("f32",)), (False, ("f32",))))
                cache_kt, cache_vt = (
                    jnp.transpose(c, (0, 1, 3, 4, 2)).reshape(c.shape[0], c.shape[1], d, c.shape[2])
                    for c in past[:2])
                o = _moba_sample(q.reshape(shp), k.reshape(shp), v.reshape(shp), cache_kt, cache_vt, l, past[2])
                a_k.append(k.reshape(n_seq, t, -1, HEAD_DIM))
                a_v.append(v.reshape(n_seq, t, -1, HEAD_DIM))
            h = _oproj(rows, h, md, 5,w_o_a[l], [o.reshape(m, d)])
        else:
            lb = l - n_a
            qg = _proj(rows_p, h, md, (3, 4), norm_g[l, 1], w_q_b[lb], tables,
                       ((True, ("f32",)),) * len(B_CONFIGS))
            shp = (n_seq, t, dk)
            if is_prompt:
                res = [_dil_prompt(qi.reshape(shp), b_k.reshape(shp), b_v.reshape(shp), w, dl)
                       for qi, (w, dl) in zip(qg, B_CONFIGS)]
                h = _oproj(rows, h, md, 5,w_o_b[lb], [r[0].reshape(m, dk) for r in res],
                           [r[1].reshape(m, dk) for r in res])
            else:
                cb_kt, cb_vt = (jnp.transpose(c, (0, 2, 3, 1)).reshape(n_seq, dk, c.shape[1]) for c in past[3:5])
                o = _dil_sample([qi.reshape(shp) for qi in qg], b_k.reshape(shp), b_v.reshape(shp), cb_kt, cb_vt)
                h = _oproj(rows, h, md, 5,w_o_b[lb], [o.reshape(m, dk)])
        h = _ffn(rows, h, md, (6, 7, 8), norm_g[l, 2], w_ffn_gate, w_ffn_up, w_ffn_down, l, 1,
                 final_g=final_norm_g if l == depth - 1 else None)
    return h.reshape(n_seq, t, d), jnp.stack(a_k), jnp.stack(a_v), b_k_out, b_v_out


def kernel(x_prompt, x_sample, cache_a_k, cache_a_v, cache_b_k, cache_b_v, page_table, c_prompt, c_sample, norm_g, w_mod, b_mod, w_ffn_gate, w_ffn_up, w_ffn_down, w_qkv_a, w_o_a, kv_norm_g, w_kv_mod, b_kv_mod, w_kv_b, w_q_b, w_o_b, final_norm_g):
    bp, t, d = x_prompt.shape
    bs, t_dec, _ = x_sample.shape
    past_len = page_table.shape[1] * cache_a_k.shape[2]
    assert past_len % MOBA_BLOCK == 0 and t_dec <= MOBA_BLOCK

    n_c = bp + bs
    c_all = _pad_rows(jnp.concatenate([c_prompt, c_sample], axis=0), -(-n_c // 8) * 8)
    mods = _modulation(c_all, w_mod, b_mod)
    kv_mod = _modulation(c_all, w_kv_mod[None], b_kv_mod[None])[0]

    bf = lambda w: w.astype(BF16)
    weights = (norm_g, bf(w_ffn_gate), bf(w_ffn_up), bf(w_ffn_down), bf(w_qkv_a), bf(w_o_a),
               kv_norm_g, bf(w_kv_b), bf(w_q_b), bf(w_o_b), final_norm_g)
    y_p, ak_p, av_p, bk_p, bv_p = _trunk(
        x_prompt, mods[:, :bp], kv_mod[:bp], jnp.arange(t, dtype=jnp.int32), None, weights)
    y_s, ak_s, av_s, bk_s, bv_s = _trunk(
        x_sample, mods[:, bp:n_c], kv_mod[bp:n_c], past_len + jnp.arange(t_dec, dtype=jnp.int32),
        (cache_a_k, cache_a_v, page_table, cache_b_k, cache_b_v), weights)

    keep = min(max(w for w, _ in B_CONFIGS), t)
    return (y_p, y_s, ak_p, av_p, ak_s, av_s, bk_p[:, -keep:], bv_p[:, -keep:], bk_s, bv_s)
```

```python
import functools

import jax
import jax.numpy as jnp
from jax import lax
from jax.experimental import pallas as pl
from jax.experimental.pallas import tpu as pltpu

HEAD_DIM = 64
ROT_DIM = HEAD_DIM // 4
ROPE_THETA = 500000.0
MOBA_BLOCK = 256
MOBA_TOPK = 3
B_CONFIGS = ((128, 1), (512, 4), (2048, 16))
EPS = 1e-6
NEG = -1e30
LOG2_E = 1.4426950408889634

LANES = 128
HEADS_PER_TILE = LANES // HEAD_DIM
MXU_COLS = 256
ROWS_LONG, ROWS_SHORT = 1024, 512
DIL_ITEMS = 4
DIL_GROUPS = 8
VMEM_LIMIT = 56 * 1024 * 1024

F32 = jnp.float32
BF16 = jnp.bfloat16


def _params(*sem):
    return pltpu.CompilerParams(dimension_semantics=sem, vmem_limit_bytes=VMEM_LIMIT)


def _dot(a, b):
    return jnp.dot(a, b, preferred_element_type=F32)


def _dot_nt(a, b):
    return lax.dot_general(a, b, (((1,), (1,)), ((), ())), preferred_element_type=F32)


def _silu(x):
    return x * jax.nn.sigmoid(x)


def _rmsnorm(x, g):
    return x * lax.rsqrt(jnp.mean(x * x, axis=-1, keepdims=True) + EPS) * g


def _rms_mod(x, g, shift, scale):
    return _rmsnorm(x, g) * (1.0 + scale) + shift


def _head_lane_mask(shape, rows_per_head):
    row = lax.broadcasted_iota(jnp.int32, shape, 0)
    lane = lax.broadcasted_iota(jnp.int32, shape, 1)
    return (lane // HEAD_DIM) == (row // rows_per_head)


class _Rows:
    def __init__(self, m, tm, rows_per_seq):
        self.m, self.tm = m, tm
        self.per_row = rows_per_seq < tm
        self.rows_per_seq = rows_per_seq
        self.tiles_per_seq = max(rows_per_seq // tm, 1)

    def prep(self, vecs):
        if self.per_row:
            return jnp.repeat(vecs, self.rows_per_seq, axis=0)
        return vecs[:, None, :]

    def mod_spec(self, d, col):
        if self.per_row:
            return pl.BlockSpec((self.tm, d), lambda i, j: (i, col))
        tps = self.tiles_per_seq
        return pl.BlockSpec((None, 1, d), lambda i, j: (i // tps, 0, col))

    def pos_spec(self, table_rows):
        n_pos_tiles = table_rows // self.tm
        return pl.BlockSpec((self.tm, LANES), lambda i, j: (i % n_pos_tiles, 0))


def _mod_kernel(c_ref, w_ref, b_ref, o_ref):
    cs = _silu(c_ref[...]).astype(BF16)
    o_ref[...] = _dot(cs, w_ref[...].astype(BF16)) + b_ref[...]


def _modulation(c, w, b):
    mc, d = c.shape
    nl, _, n = w.shape
    tn = d
    return pl.pallas_call(
        _mod_kernel,
        grid=(nl, n // tn),
        in_specs=[pl.BlockSpec((mc, d), lambda l, j: (0, 0)),
                  pl.BlockSpec((None, d, tn), lambda l, j: (l, 0, j)),
                  pl.BlockSpec((None, 1, tn), lambda l, j: (l, 0, j))],
        out_specs=pl.BlockSpec((None, mc, tn), lambda l, j: (l, 0, j)),
        out_shape=jax.ShapeDtypeStruct((nl, mc, n), F32),
        compiler_params=_params("arbitrary", "arbitrary"),
        name="modulation",
    )(c, w, b.reshape(nl, 1, n))


def _ffn_kernel(h_ref, sh_ref, sc_ref, gt_ref, g_ref, wg_ref, wu_ref, wd_ref, *rest, final):
    if final:
        fg_ref, o_ref, u_scr, act_scr = rest
    else:
        o_ref, u_scr, act_scr = rest
    d, ff = wg_ref.shape
    tf = MXU_COLS if ff % MXU_COLS == 0 else LANES
    to = min(2 * MXU_COLS, d)
    u_scr[...] = _rms_mod(h_ref[...], g_ref[...], sh_ref[...], sc_ref[...]).astype(BF16)
    for c in range(ff // tf):
        sl = slice(c * tf, (c + 1) * tf)
        u = u_scr[...]
        a = _dot(u, wg_ref[:, sl])
        b = _dot(u, wu_ref[:, sl])
        act_scr[:, sl] = (_silu(a) * b).astype(BF16)
    for c in range(d // to):
        sl = slice(c * to, (c + 1) * to)
        down = _dot(act_scr[...], wd_ref[:, sl])
        o_ref[:, sl] = h_ref[:, sl] + 0.5 * gt_ref[:, sl] * down
    if final:
        o_ref[...] = _rmsnorm(o_ref[...], fg_ref[...])


def _ffn(rows, h, mod, cols, g, wg, wu, wd, l, s, final_g=None):
    m, d = h.shape
    ff = wg.shape[-1]
    tm = rows.tm
    final = final_g is not None
    resident = lambda shape: pl.BlockSpec((None, None) + shape, lambda i, f: (l, s, 0, 0),
                                          pipeline_mode=pl.Buffered(1))
    in_specs = [pl.BlockSpec((tm, d), lambda i, f: (i, 0)),
                rows.mod_spec(d, cols[0]), rows.mod_spec(d, cols[1]), rows.mod_spec(d, cols[2]),
                pl.BlockSpec((1, d), lambda i, f: (0, 0)),
                resident((d, ff)), resident((d, ff)), resident((ff, d))]
    args = [h, mod, mod, mod, g.reshape(1, d), wg, wu, wd]
    if final:
        in_specs.append(pl.BlockSpec((1, d), lambda i, f: (0, 0)))
        args.append(final_g.reshape(1, d))
    return pl.pallas_call(
        functools.partial(_ffn_kernel, final=final),
        grid=(m // tm, 1),
        in_specs=in_specs,
        out_specs=pl.BlockSpec((tm, d), lambda i, f: (i, 0)),
        out_shape=jax.ShapeDtypeStruct((m, d), F32),
        scratch_shapes=[pltpu.VMEM((tm, d), BF16), pltpu.VMEM((tm, ff), BF16)],
        compiler_params=_params("arbitrary", "arbitrary"),
        name="ffn",
    )(*args)


def _rope_tables(pos):
    half = ROT_DIM // 2
    n = pos.shape[0]
    freq = ROPE_THETA ** (-jnp.arange(half, dtype=F32) / half)
    ang = pos.astype(F32)[:, None] * freq[None, :]
    cos, sin = jnp.cos(ang), jnp.sin(ang)
    rest = HEAD_DIM - ROT_DIM
    ct = jnp.concatenate([cos, cos, jnp.ones((n, rest), F32)], axis=1)
    s_lo = jnp.concatenate([-sin, jnp.zeros((n, half + rest), F32)], axis=1)
    s_hi = jnp.concatenate([jnp.zeros((n, half), F32), sin, jnp.zeros((n, rest), F32)], axis=1)
    tile = lambda t: jnp.tile(t, (1, HEADS_PER_TILE))
    return (tile(ct), tile(s_hi), tile(s_lo)), (cos.T, sin.T)


def _rope_tile(x, ct, s_hi, s_lo):
    half = ROT_DIM // 2
    return x * ct + pltpu.roll(x, half, 1) * s_hi + pltpu.roll(x, LANES - half, 1) * s_lo


def _rope_t(x_t, cos_t, sin_t):
    half = ROT_DIM // 2
    pieces = []
    for base in range(0, x_t.shape[0], HEAD_DIM):
        x1, x2 = x_t[base:base + half], x_t[base + half:base + ROT_DIM]
        pieces += [x1 * cos_t - x2 * sin_t, x2 * cos_t + x1 * sin_t, x_t[base + ROT_DIM:base + HEAD_DIM]]
    return jnp.concatenate(pieces, axis=0)


_MEAN_GROUP = 8


def _proj_kernel(h_ref, sh_ref, sc_ref, g_ref, w_ref, ct_ref, shi_ref, slo_ref, cos_t_ref, sin_t_ref,
                 *outs, parts):
    i = pl.program_id(0)
    u = _rms_mod(h_ref[...], g_ref[...], sh_ref[...], sc_ref[...]).astype(BF16)
    tm = u.shape[0]
    n_p = w_ref.shape[1] // len(parts)
    blocks = tm // MOBA_BLOCK
    n_out = 0
    for p, (rope, forms) in enumerate(parts):
        part_outs = outs[n_out:n_out + len(forms)]
        n_out += len(forms)
        y = _dot(u, w_ref[:, p * n_p:(p + 1) * n_p])
        val = val_t = None
        if any(f.startswith("t_") for f in forms):
            val_t = y.T
            if rope:
                val_t = _rope_t(val_t, cos_t_ref[...], sin_t_ref[...])
            if any(not f.startswith("t_") for f in forms):
                val = val_t.T
        elif rope:
            ct, s_hi, s_lo = ct_ref[...], shi_ref[...], slo_ref[...]
            val = jnp.concatenate([_rope_tile(y[:, c * LANES:(c + 1) * LANES], ct, s_hi, s_lo)
                                   for c in range(n_p // LANES)], axis=1)
        else:
            val = y
        for form, out in zip(forms, part_outs):
            if form == "f32":
                out[...] = val
            elif form == "bf16":
                out[...] = val.astype(BF16)
            elif form == "t_f32":
                out[...] = val_t
            elif form == "t_bf16":
                for c in range(blocks):
                    out[c] = val_t[:, c * MOBA_BLOCK:(c + 1) * MOBA_BLOCK].astype(BF16)
            elif form == "block_mean":
                base = (i % (_MEAN_GROUP // blocks)) * blocks
                for c in range(blocks):
                    out[pl.ds(base + c, 1), :] = jnp.mean(
                        val[c * MOBA_BLOCK:(c + 1) * MOBA_BLOCK], axis=0, keepdims=True)


def _proj(rows, h, mod, cols, g, w, tables, parts):
    m, d = h.shape
    n_p = w.shape[1] // len(parts)
    tm, tps, t = rows.tm, rows.tiles_per_seq, rows.rows_per_seq
    n_seq = m // t
    out_specs, out_shape = [], []
    for _, forms in parts:
        for form in forms:
            if form in ("f32", "bf16"):
                out_specs.append(pl.BlockSpec((tm, n_p), lambda i, j: (i, 0)))
                out_shape.append(jax.ShapeDtypeStruct((m, n_p), F32 if form == "f32" else BF16))
                continue
            assert not rows.per_row and tm % MOBA_BLOCK == 0
            if form == "t_f32":
                out_specs.append(pl.BlockSpec((None, n_p, tm), lambda i, j: (i // tps, 0, i % tps)))
                out_shape.append(jax.ShapeDtypeStruct((n_seq, n_p, t), F32))
            elif form == "t_bf16":
                out_specs.append(pl.BlockSpec((None, tm // MOBA_BLOCK, n_p, MOBA_BLOCK),
                                              lambda i, j: (i // tps, i % tps, 0, 0)))
                out_shape.append(jax.ShapeDtypeStruct((n_seq, t // MOBA_BLOCK, n_p, MOBA_BLOCK), BF16))
            else:
                steps = _MEAN_GROUP * MOBA_BLOCK // tm
                assert form == "block_mean" and tps % steps == 0
                out_specs.append(pl.BlockSpec((None, _MEAN_GROUP, n_p),
                                              lambda i, j: (i // tps, (i % tps) // steps, 0)))
                out_shape.append(jax.ShapeDtypeStruct((n_seq, t // MOBA_BLOCK, n_p), F32))
    lane_tables, t_tables = tables
    pos_spec = rows.pos_spec(lane_tables[0].shape[0])
    n_pos_tiles = lane_tables[0].shape[0] // tm
    pos_t_spec = pl.BlockSpec((ROT_DIM // 2, tm), lambda i, j: (0, i % n_pos_tiles))
    return pl.pallas_call(
        functools.partial(_proj_kernel, parts=parts),
        grid=(m // tm, 1),
        in_specs=[pl.BlockSpec((tm, d), lambda i, j: (i, 0)),
                  rows.mod_spec(d, cols[0]), rows.mod_spec(d, cols[1]),
                  pl.BlockSpec((1, d), lambda i, j: (0, 0)),
                  pl.BlockSpec(w.shape, lambda i, j: (0, 0)),
                  pos_spec, pos_spec, pos_spec, pos_t_spec, pos_t_spec],
        out_specs=out_specs,
        out_shape=out_shape,
        compiler_params=_params("arbitrary", "arbitrary"),
        name="proj",
    )(h, mod, mod, g.reshape(1, d), w, *lane_tables, *t_tables)


def _oproj_kernel(h_ref, gt_ref, w_ref, *rest, n_groups):
    o_ref = rest[-1]
    if n_groups == 0:
        o = rest[0][...]
    else:
        os_, lses = rest[:n_groups], rest[n_groups:2 * n_groups]
        lse = [r[...] for r in lses]
        mx = functools.reduce(jnp.maximum, lse)
        e = [jnp.exp(x - mx) for x in lse]
        den = functools.reduce(lambda a, b: a + b, e)
        o = functools.reduce(lambda a, b: a + b, [(ei / den) * r[...] for ei, r in zip(e, os_)])
    o_ref[...] = h_ref[...] + gt_ref[...] * _dot(o.astype(BF16), w_ref[...])


def _oproj(rows, h, mod, col, w, o_list, lse_list=()):
    m, d = h.shape
    k = w.shape[0]
    tm = rows.tm
    xs = list(o_list) + list(lse_list)
    return pl.pallas_call(
        functools.partial(_oproj_kernel, n_groups=len(lse_list)),
        grid=(m // tm, 1),
        in_specs=[pl.BlockSpec((tm, d), lambda i, j: (i, 0)),
                  rows.mod_spec(d, col),
                  pl.BlockSpec((k, d), lambda i, j: (0, 0))]
                 + [pl.BlockSpec((tm, k), lambda i, j: (i, 0))] * len(xs),
        out_specs=pl.BlockSpec((tm, d), lambda i, j: (i, 0)),
        out_shape=jax.ShapeDtypeStruct((m, d), F32),
        compiler_params=_params("arbitrary", "arbitrary"),
        name="oproj",
    )(h, mod, w, *xs)


def _split_bf16(x):
    hi = x.astype(BF16)
    lo = (x - hi.astype(F32)).astype(BF16)
    return hi, lo


def _top_blocks_t(gate, n_past):
    blk_id = lax.broadcasted_iota(jnp.int32, gate.shape, 0).astype(F32)
    n_past = n_past.astype(F32)
    g = jnp.where(blk_id < n_past, gate, NEG)
    sel = jnp.zeros(gate.shape, jnp.bool_)
    for _ in range(MOBA_TOPK):
        mx = jnp.max(g, axis=0, keepdims=True)
        idx = jnp.min(jnp.where(g == mx, blk_id, float(gate.shape[0])), axis=0, keepdims=True)
        pick = blk_id == idx
        sel = sel | pick
        g = jnp.where(pick, -jnp.inf, g)
    return sel & (blk_id < n_past)


def _moba_prompt_kernel(q_ref, k_ref, vt_ref, km_ref, o_ref,
                        qt_scr, sel_scr, m_scr, l_scr, acc_scr, s_own, s_even, s_odd):
    blk = MOBA_BLOCK
    n_h = qt_scr.shape[0]
    tiles = [slice(hh // HEADS_PER_TILE * LANES, (hh // HEADS_PER_TILE + 1) * LANES) for hh in range(n_h)]
    nb = vt_ref.shape[0]
    i = pl.program_id(2)
    heads = range(n_h)
    hs = [slice(hh * HEAD_DIM, (hh + 1) * HEAD_DIM) for hh in heads]

    def scores_into(dst, first, n_blocks):
        rows = n_blocks * blk
        jj = jnp.minimum(first, nb - n_blocks)
        kb = k_ref[pl.ds(pl.multiple_of(jj * blk, blk), rows), :]
        for hh in heads:
            dst[hh] = _dot(kb[:, tiles[hh]], qt_scr[hh])

    def consume(src, part, j, own):
        jj = jnp.minimum(j, nb - 1)
        vtb = vt_ref[jj]
        ps, alphas, chosen = [], [], []
        for hh in heads:
            s = src[hh, part * blk:(part + 1) * blk, :]
            if own:
                key_id = lax.broadcasted_iota(jnp.int32, (blk, blk), 0)
                qry_id = lax.broadcasted_iota(jnp.int32, (blk, blk), 1)
                s = jnp.where(key_id <= qry_id, s, NEG)
                ch = jnp.full((1, blk), True)
            else:
                ch = (sel_scr[hh, pl.ds(jj, 1), :] > 0.5) & (j < i)
            m_old = m_scr[hh]
            m_all = jnp.maximum(m_old, jnp.max(s, axis=0, keepdims=True))
            p = jnp.exp2(s - m_all)
            m_new = jnp.where(ch, m_all, m_old)
            alpha = jnp.exp2(m_old - m_new)
            l_scr[hh] = alpha * l_scr[hh] + jnp.where(ch, jnp.sum(p, axis=0, keepdims=True), 0.0)
            m_scr[hh] = m_new
            ps.append(p.astype(BF16))
            alphas.append(alpha)
            chosen.append(ch)
        pvs = [_dot(vtb[hs[hh], :], ps[hh]) for hh in heads]
        for hh in heads:
            acc_scr[hh] = alphas[hh] * acc_scr[hh] + jnp.where(chosen[hh], pvs[hh], 0.0)

    feat = lax.broadcasted_iota(jnp.int32, (LANES, blk), 0)
    for hh in heads:
        qth = jnp.where((feat // HEAD_DIM) == hh % HEADS_PER_TILE, q_ref[tiles[hh], :], 0.0)
        qt_scr[hh] = (qth * (HEAD_DIM ** -0.5 * LOG2_E)).astype(BF16)
        q_hi, q_lo = _split_bf16(qth)
        km_hi, km_lo = _split_bf16(km_ref[:, tiles[hh]])
        by_q_hi = _dot(jnp.concatenate([km_hi, km_lo], axis=0), q_hi)
        gate = by_q_hi[:nb] + by_q_hi[nb:] + _dot(km_hi, q_lo)
        sel_scr[hh] = _top_blocks_t(gate, i).astype(F32)
        m_scr[hh] = jnp.full((1, blk), NEG, F32)
        l_scr[hh] = jnp.zeros((1, blk), F32)
        acc_scr[hh] = jnp.zeros((HEAD_DIM, blk), F32)
    scores_into(s_own, i, 1)
    scores_into(s_even, 0, 2)
    consume(s_own, 0, i, own=True)

    def pair_from(cur, nxt):
        def run(c):
            scores_into(nxt, 2 * c + 2, 2)
            consume(cur, 0, 2 * c, own=False)
            consume(cur, 1, 2 * c + 1, own=False)
        return run

    def two_blocks(c, carry):
        lax.cond(c % 2 == 0, pair_from(s_even, s_odd), pair_from(s_odd, s_even), c)
        return carry

    lax.fori_loop(0, (i + 1) // 2, two_blocks, 0)
    ot = jnp.concatenate([acc_scr[hh] / l_scr[hh] for hh in heads], axis=0)
    o_ref[...] = ot.T


def _moba_prompt(q_t, k_bf, vt_bf, km):
    b, d, t = q_t.shape
    blk = MOBA_BLOCK
    nb = t // blk
    width = min(d, 2 * LANES)
    n_h = width // HEAD_DIM
    assert t % blk == 0 and nb % 8 == 0 and d % width == 0
    tile_spec = pl.BlockSpec((None, blk, width), lambda bi, hg, i: (bi, i, hg))
    return pl.pallas_call(
        _moba_prompt_kernel,
        grid=(b, d // width, nb),
        in_specs=[pl.BlockSpec((None, width, blk), lambda bi, hg, i: (bi, hg, i)),
                  pl.BlockSpec((None, t, width), lambda bi, hg, i: (bi, 0, hg)),
                  pl.BlockSpec((None, nb, width, blk), lambda bi, hg, i: (bi, 0, hg, 0)),
                  pl.BlockSpec((None, nb, width), lambda bi, hg, i: (bi, 0, hg))],
        out_specs=tile_spec,
        out_shape=jax.ShapeDtypeStruct((b, t, d), F32),
        scratch_shapes=[pltpu.VMEM((n_h, LANES, blk), BF16),
                        pltpu.VMEM((n_h, nb, blk), F32),
                        pltpu.VMEM((n_h, 1, blk), F32),
                        pltpu.VMEM((n_h, 1, blk), F32),
                        pltpu.VMEM((n_h, HEAD_DIM, blk), F32)]
                       + [pltpu.VMEM((n_h, blk, blk), F32)] + [pltpu.VMEM((n_h, 2 * blk, blk), F32)] * 2,
        compiler_params=_params("arbitrary", "arbitrary", "arbitrary"),
        name="moba_prompt",
    )(q_t, k_bf, vt_bf, km)


def _pad_rows(x, n):
    return jnp.concatenate([x, jnp.zeros((n - x.shape[0], x.shape[1]), x.dtype)], axis=0)


def _moba_sample_kernel(pt_ref, q_ref, kn_ref, vn_ref, *rest, n_pages, page, n_heads):
    k_pages, v_pages = rest[:n_pages], rest[n_pages:2 * n_pages]
    o_ref, s_scr = rest[2 * n_pages], rest[2 * n_pages + 1]
    t_dec, d = q_ref.shape
    r = n_heads * t_dec
    ppb = MOBA_BLOCK // page
    n_past = n_pages // ppb
    own = lax.broadcasted_iota(jnp.int32, (r, d), 0) // t_dec
    head_mask = (lax.broadcasted_iota(jnp.int32, (r, d), 1) // HEAD_DIM) == own
    qrows = jnp.where(head_mask, jnp.concatenate([q_ref[...]] * n_heads, axis=0), 0.0)
    qs = (qrows * HEAD_DIM ** -0.5).astype(BF16)

    def block_of(pages, j):
        return jnp.concatenate([pages[p][...].astype(BF16) for p in range(j * ppb, (j + 1) * ppb)], axis=1)

    gates = []
    for j in range(n_past):
        sl = slice(j * MOBA_BLOCK, (j + 1) * MOBA_BLOCK)
        s_scr[:, sl] = _dot(qs, block_of(k_pages, j))
        gates.append(jnp.sum(s_scr[:, sl], axis=1, keepdims=True))
    for j in range(n_past):
        rank = jnp.zeros((r, 1), jnp.int32)
        for j2 in range(n_past):
            if j2 != j:
                ahead = (gates[j2] > gates[j]) | ((gates[j2] == gates[j]) & (j2 < j))
                rank = rank + ahead.astype(jnp.int32)
        chosen = rank < MOBA_TOPK
        sl = slice(j * MOBA_BLOCK, (j + 1) * MOBA_BLOCK)
        s_scr[:, sl] = jnp.where(chosen, s_scr[:, sl], NEG)
    t_row = lax.broadcasted_iota(jnp.int32, (r, page), 0) % t_dec
    t_col = lax.broadcasted_iota(jnp.int32, (r, page), 1)
    s_own = _dot_nt(qs, _pad_rows(kn_ref[...], page).astype(BF16))
    s_scr[:, n_pages * page:] = jnp.where(t_col <= t_row, s_own, NEG)

    s = s_scr[...]
    m = jnp.max(s, axis=1, keepdims=True)
    p_all = jnp.exp(s - m)
    l = jnp.sum(p_all, axis=1, keepdims=True)
    acc = _dot(p_all[:, n_pages * page:].astype(BF16), _pad_rows(vn_ref[...], page).astype(BF16))
    for j in range(n_past):
        acc = acc + _dot_nt(p_all[:, j * MOBA_BLOCK:(j + 1) * MOBA_BLOCK].astype(BF16), block_of(v_pages, j))
    o_full = jnp.where(head_mask, acc / l, 0.0)
    out = o_full[0:t_dec]
    for h in range(1, n_heads):
        out = out + o_full[h * t_dec:(h + 1) * t_dec]
    o_ref[...] = out


def _moba_sample(q, k_new, v_new, cache_kt, cache_vt, layer, page_table):
    b, t_dec, d = q.shape
    n_pages = page_table.shape[1]
    page = cache_kt.shape[3]
    n_heads = d // HEAD_DIM
    assert MOBA_BLOCK % page == 0 and (n_pages * page) % MOBA_BLOCK == 0 and t_dec <= page
    tok_spec = pl.BlockSpec((None, t_dec, d), lambda i, pt: (i, 0, 0))
    page_specs = [pl.BlockSpec((None, None, d, page),
                               functools.partial(lambda i, pt, p: (layer, pt[i, p], 0, 0), p=p))
                  for p in range(n_pages)]
    grid_spec = pltpu.PrefetchScalarGridSpec(
        num_scalar_prefetch=1,
        grid=(b,),
        in_specs=[tok_spec, tok_spec, tok_spec] + page_specs + page_specs,
        out_specs=tok_spec,
        scratch_shapes=[pltpu.VMEM((n_heads * t_dec, (n_pages + 1) * page), F32)],
    )
    return pl.pallas_call(
        functools.partial(_moba_sample_kernel, n_pages=n_pages, page=page, n_heads=n_heads),
        grid_spec=grid_spec,
        out_shape=jax.ShapeDtypeStruct((b, t_dec, d), F32),
        compiler_params=_params("arbitrary"),
        name="moba_sample",
    )(page_table, q, k_new, v_new, *([cache_kt] * n_pages), *([cache_vt] * n_pages))


def _dil_prompt_kernel(q_ref, k_ref, v_ref, o_ref, lse_ref, *, dil, tq, sub, width, win):
    length = k_ref.shape[0] // dil
    base = pl.program_id(2) * (sub * tq)
    lane = lax.broadcasted_iota(jnp.int32, (tq, LANES), 1)
    row_col = (lax.broadcasted_iota(jnp.int32, (tq, width), 0)
               - lax.broadcasted_iota(jnp.int32, (tq, width), 1))
    heads = range(HEADS_PER_TILE)

    def rows(start, n):
        return pl.ds(start, n) if dil == 1 else pl.ds(start, n, stride=dil)

    def attend(items):
        qs, kws, vws, valids = [], [], [], []
        for r, u in items:
            q0 = base + u * tq
            ks = jnp.clip(q0 - win, 0, length - width)
            q = q_ref[rows(u * tq * dil + r, tq), :]
            qs.append([(jnp.where((lane // HEAD_DIM) == hh, q, 0.0) * (HEAD_DIM ** -0.5 * LOG2_E)).astype(BF16)
                       for hh in heads])
            kws.append(k_ref[rows(ks * dil + r, width), :].astype(BF16))
            vws.append(v_ref[rows(ks * dil + r, width), :].astype(BF16))
            delta = q0 - ks + row_col
            valids.append((delta >= 0) & (delta <= win))
        n = range(len(items))
        scores = [_dot_nt(jnp.concatenate(qs[it], axis=0), kws[it]) for it in n]
        ps, ls, ms = [], [], []
        for it in n:
            for hh in heads:
                s = jnp.where(valids[it], scores[it][hh * tq:(hh + 1) * tq], NEG)
                m = jnp.max(s, axis=1, keepdims=True)
                p = jnp.exp2(s - m)
                ls.append(jnp.sum(p, axis=1, keepdims=True))
                ms.append(m * (1.0 / LOG2_E))
                ps.append(p.astype(BF16))
        stacked = [_dot(jnp.concatenate(ps[it * len(heads):(it + 1) * len(heads)], axis=0), vws[it]) for it in n]
        pvs = [stacked[it][hh * tq:(hh + 1) * tq] for it in n for hh in heads]
        for it, (r, u) in enumerate(items):
            c0, c1 = it * len(heads), it * len(heads) + 1
            o_ref[rows(u * tq * dil + r, tq), :] = jnp.where(lane < HEAD_DIM, pvs[c0] / ls[c0], pvs[c1] / ls[c1])
            lse_ref[rows(u * tq * dil + r, tq), :] = jnp.where(
                lane < HEAD_DIM, jnp.broadcast_to(ms[c0] + jnp.log(ls[c0]), (tq, LANES)),
                jnp.broadcast_to(ms[c1] + jnp.log(ls[c1]), (tq, LANES)))

    if dil * sub <= DIL_ITEMS * DIL_GROUPS:
        items = [(r, u) for u in range(sub) for r in range(dil)]
        for c in range(0, len(items), DIL_ITEMS):
            attend(items[c:c + DIL_ITEMS])
    else:
        per_iter = min(dil, DIL_ITEMS)

        def some_classes(c, carry):
            attend([(per_iter * c + x, u) for x in range(per_iter) for u in range(sub)])
            return carry
        lax.fori_loop(0, dil // per_iter, some_classes, 0)


def _dil_prompt(q, k, v, window, dil):
    b, t, dk = q.shape
    assert t % dil == 0 and HEADS_PER_TILE == 2 and (dil == 1 or dil % 2 == 0)
    length = t // dil
    win = window // dil
    tq = min(LANES, length)
    sub = max(DIL_ITEMS * DIL_GROUPS // dil, 1)
    if length % (sub * tq):
        sub = 1
    assert dil * sub <= DIL_ITEMS * DIL_GROUPS or (sub == 1 and dil % DIL_ITEMS == 0)
    width = min(tq + win, length)
    assert length % (sub * tq) == 0
    tile_spec = pl.BlockSpec((None, sub * tq * dil, LANES), lambda bi, h, i: (bi, i, h))
    seq_spec = pl.BlockSpec((None, t, LANES), lambda bi, h, i: (bi, 0, h))
    return pl.pallas_call(
        functools.partial(_dil_prompt_kernel, dil=dil, tq=tq, sub=sub, width=width, win=win),
        grid=(b, dk // LANES, length // (sub * tq)),
        in_specs=[tile_spec, seq_spec, seq_spec],
        out_specs=[tile_spec, tile_spec],
        out_shape=[jax.ShapeDtypeStruct((b, t, dk), F32)] * 2,
        compiler_params=_params("arbitrary", "arbitrary", "arbitrary"),
        name="dilated_prompt",
    )(q, k, v)


def _dil_sample_kernel(q0_ref, q1_ref, q2_ref, kn_ref, vn_ref, kc_ref, vc_ref, o_ref, s_scr, p_scr, *, n_heads):
    for i in range(o_ref.shape[0]):
        _dil_sample_one((q0_ref.at[i], q1_ref.at[i], q2_ref.at[i]), kn_ref.at[i], vn_ref.at[i],
                        kc_ref.at[i], vc_ref.at[i], o_ref.at[i], s_scr, p_scr, n_heads)


def _dil_sample_one(q_refs, kn_ref, vn_ref, kc_ref, vc_ref, o_ref, s_scr, p_scr, n_heads):
    t_dec, dk = kn_ref.shape
    past = kc_ref.shape[1]
    pad = LANES
    rg = n_heads * t_dec
    head_mask = _head_lane_mask((rg, dk), t_dec)
    qrows = jnp.concatenate(
        [jnp.where(head_mask, jnp.concatenate([qr[...]] * n_heads, axis=0), 0.0) for qr in q_refs], axis=0)
    qs = (qrows * HEAD_DIM ** -0.5).astype(BF16)
    s_scr[:, :past] = _dot(qs, kc_ref[...].astype(BF16))
    s_scr[:, past:] = _dot_nt(qs, _pad_rows(kn_ref[...], pad).astype(BF16))

    t_row = lax.broadcasted_iota(jnp.int32, (rg, past + pad), 0) % t_dec
    col = lax.broadcasted_iota(jnp.int32, (rg, past + pad), 1)
    delta = past + t_row - col
    in_range = col < past + t_dec
    l_g, lse_g = [], []
    for g, (window, dil) in enumerate(B_CONFIGS):
        rs = slice(g * rg, (g + 1) * rg)
        assert dil & (dil - 1) == 0
        valid = in_range & (delta >= 0) & (delta <= window) & ((delta & (dil - 1)) == 0)
        s = jnp.where(valid, s_scr[rs, :], NEG)
        m = jnp.max(s, axis=1, keepdims=True)
        p = jnp.exp(s - m)
        l = jnp.sum(p, axis=1, keepdims=True)
        p_scr[rs, :] = p.astype(BF16)
        l_g.append(l)
        lse_g.append(m + jnp.log(l))
    acc = (_dot_nt(p_scr[:, :past], vc_ref[...].astype(BF16))
           + _dot(p_scr[:, past:], _pad_rows(vn_ref[...], pad).astype(BF16)))
    mx = functools.reduce(jnp.maximum, lse_g)
    e = [jnp.exp(x - mx) for x in lse_g]
    den = functools.reduce(lambda a, b: a + b, e)
    mixed = jnp.zeros((rg, dk), F32)
    for g in range(len(B_CONFIGS)):
        mixed = mixed + (e[g] / den) * (acc[g * rg:(g + 1) * rg] / l_g[g])
    mixed = jnp.where(head_mask, mixed, 0.0)
    out = mixed[0:t_dec]
    for h in range(1, n_heads):
        out = out + mixed[h * t_dec:(h + 1) * t_dec]
    o_ref[...] = out


def _dil_sample(q_groups, k_new, v_new, cache_kt, cache_vt):
    b, t_dec, dk = k_new.shape
    past = cache_kt.shape[2]
    n_heads = dk // HEAD_DIM
    n_rows = len(B_CONFIGS) * n_heads * t_dec
    per_step = 2 if b % 2 == 0 else 1
    tok_spec = pl.BlockSpec((per_step, t_dec, dk), lambda i: (i, 0, 0))
    cache_spec = pl.BlockSpec((per_step, dk, past), lambda i: (i, 0, 0))
    return pl.pallas_call(
        functools.partial(_dil_sample_kernel, n_heads=n_heads),
        grid=(b // per_step,),
        in_specs=[tok_spec] * 5 + [cache_spec] * 2,
        out_specs=tok_spec,
        out_shape=jax.ShapeDtypeStruct((b, t_dec, dk), F32),
        scratch_shapes=[pltpu.VMEM((n_rows, past + LANES), F32),
                        pltpu.VMEM((n_rows, past + LANES), BF16)],
        compiler_params=_params("arbitrary"),
        name="dilated_sample",
    )(*q_groups, k_new, v_new, cache_kt, cache_vt)


def _trunk(x, mods, kv_mod, pos, past, weights):
    (norm_g, w_ffn_gate, w_ffn_up, w_ffn_down, w_qkv_a, w_o_a,
     kv_norm_g, w_kv_b, w_q_b, w_o_b, final_norm_g) = weights
    n_seq, t, d = x.shape
    m = n_seq * t
    depth = norm_g.shape[0]
    n_a = w_qkv_a.shape[0]
    dk = w_kv_b.shape[1] // 2
    is_prompt = past is None
    short = t < 256
    rows = _Rows(m, min(ROWS_LONG if is_prompt else ROWS_SHORT, m if short else t), t)
    rows_p = _Rows(m, min(ROWS_SHORT, m if short else t), t)
    tables = _rope_tables(pos)
    if rows.per_row:
        tables = (tuple(jnp.tile(tb, (n_seq, 1)) for tb in tables[0]),
                  tuple(jnp.tile(tb, (1, n_seq)) for tb in tables[1]))

    def heads_last(x_t):
        return jnp.transpose(x_t.reshape(n_seq, -1, HEAD_DIM, x_t.shape[-1]), (0, 3, 1, 2))

    h = x.reshape(m, d)
    a_k, a_v = [], []
    b_k = b_v = b_k_out = b_v_out = None
    for l in range(depth):
        if l == n_a:
            kvm = rows.prep(kv_mod)
            if is_prompt:
                b_k, b_kt, b_v, b_vt = _proj(rows_p, h, kvm, (0, 1), kv_norm_g, w_kv_b, tables,
                                             ((True, ("f32", "t_f32")), (False, ("f32", "t_f32"))))
                b_k_out, b_v_out = heads_last(b_kt), heads_last(b_vt)
            else:
                b_k, b_v = _proj(rows_p, h, kvm, (0, 1), kv_norm_g, w_kv_b, tables,
                                 ((True, ("f32",)), (False, ("f32",))))
                b_k_out, b_v_out = (r.reshape(n_seq, t, -1, HEAD_DIM) for r in (b_k, b_v))
        md = rows.prep(mods[l])
        h = _ffn(rows, h, md, (0, 1, 2), norm_g[l, 0], w_ffn_gate, w_ffn_up, w_ffn_down, l, 0)
        if l < n_a:
            shp = (n_seq, t, d)
            if is_prompt:
                q_t, k_t, k_bf, km, v_t, vt_bf = _proj(
                    rows_p, h, md, (3, 4), norm_g[l, 1], w_qkv_a[l], tables,
                    ((True, ("t_f32",)), (True, ("t_f32", "bf16", "block_mean")), (False, ("t_f32", "t_bf16"))))
                o = _moba_prompt(q_t, k_bf.reshape(shp), vt_bf, km)
                a_k.append(heads_last(k_t))
                a_v.append(heads_last(v_t))
            else:
                q, k, v = _proj(rows_p, h, md, (3, 4), norm_g[l, 1], w_qkv_a[l], tables,
                                ((True, ("f32",)), (True, ("f32",)), (False, ("f32",))))
                cache_kt, cache_vt = (
                    jnp.transpose(c, (0, 1, 3, 4, 2)).reshape(c.shape[0], c.shape[1], d, c.shape[2])
                    for c in past[:2])
                o = _moba_sample(q.reshape(shp), k.reshape(shp), v.reshape(shp), cache_kt, cache_vt, l, past[2])
                a_k.append(k.reshape(n_seq, t, -1, HEAD_DIM))
                a_v.append(v.reshape(n_seq, t, -1, HEAD_DIM))
            h = _oproj(rows, h, md, 5,w_o_a[l], [o.reshape(m, d)])
        else:
            lb = l - n_a
            qg = _proj(rows_p, h, md, (3, 4), norm_g[l, 1], w_q_b[lb], tables,
                       ((True, ("f32",)),) * len(B_CONFIGS))
            shp = (n_seq, t, dk)
            if is_prompt:
                res = [_dil_prompt(qi.reshape(shp), b_k.reshape(shp), b_v.reshape(shp), w, dl)
                       for qi, (w, dl) in zip(qg, B_CONFIGS)]
                h = _oproj(rows, h, md, 5,w_o_b[lb], [r[0].reshape(m, dk) for r in res],
                           [r[1].reshape(m, dk) for r in res])
            else:
                cb_kt, cb_vt = (jnp.transpose(c, (0, 2, 3, 1)).reshape(n_seq, dk, c.shape[1]) for c in past[3:5])
                o = _dil_sample([qi.reshape(shp) for qi in qg], b_k.reshape(shp), b_v.reshape(shp), cb_kt, cb_vt)
                h = _oproj(rows, h, md, 5,w_o_b[lb], [o.reshape(m, dk)])
        h = _ffn(rows, h, md, (6, 7, 8), norm_g[l, 2], w_ffn_gate, w_ffn_up, w_ffn_down, l, 1,
                 final_g=final_norm_g if l == depth - 1 else None)
    return h.reshape(n_seq, t, d), jnp.stack(a_k), jnp.stack(a_v), b_k_out, b_v_out


def kernel(x_prompt, x_sample, cache_a_k, cache_a_v, cache_b_k, cache_b_v, page_table, c_prompt, c_sample, norm_g, w_mod, b_mod, w_ffn_gate, w_ffn_up, w_ffn_down, w_qkv_a, w_o_a, kv_norm_g, w_kv_mod, b_kv_mod, w_kv_b, w_q_b, w_o_b, final_norm_g):
    bp, t, d = x_prompt.shape
    bs, t_dec, _ = x_sample.shape
    past_len = page_table.shape[1] * cache_a_k.shape[2]
    assert past_len % MOBA_BLOCK == 0 and t_dec <= MOBA_BLOCK

    n_c = bp + bs
    c_all = _pad_rows(jnp.concatenate([c_prompt, c_sample], axis=0), -(-n_c // 8) * 8)
    mods = _modulation(c_all, w_mod, b_mod)
    kv_mod = _modulation(c_all, w_kv_mod[None], b_kv_mod[None])[0]

    bf = lambda w: w.astype(BF16)
    weights = (norm_g, bf(w_ffn_gate), bf(w_ffn_up), bf(w_ffn_down), bf(w_qkv_a), bf(w_o_a),
               kv_norm_g, bf(w_kv_b), bf(w_q_b), bf(w_o_b), final_norm_g)
    y_p, ak_p, av_p, bk_p, bv_p = _trunk(
        x_prompt, mods[:, :bp], kv_mod[:bp], jnp.arange(t, dtype=jnp.int32), None, weights)
    y_s, ak_s, av_s, bk_s, bv_s = _trunk(
        x_sample, mods[:, bp:n_c], kv_mod[bp:n_c], past_len + jnp.arange(t_dec, dtype=jnp.int32),
        (cache_a_k, cache_a_v, page_table, cache_b_k, cache_b_v), weights)

    keep = min(max(w for w, _ in B_CONFIGS), t)
    return (y_p, y_s, ak_p, av_p, ak_s, av_s, bk_p[:, -keep:], bv_p[:, -keep:], bk_s, bv_s)
```
